```python
import math
import jax, jax.numpy as jnp
from jax import lax
import numpy as np

D_MODEL = 2048
BATCH = 8
SEQ = 2048
DEPTH = 1

A_HEADS = 16
A_HEAD_DIM = 64
A_KV_GROUPS = 4
A_REP = A_HEADS // A_KV_GROUPS
A_WIDTH = A_HEADS * A_HEAD_DIM
A_KV_WIDTH = A_KV_GROUPS * A_HEAD_DIM
CMP_BLOCK = 32
CMP_STRIDE = 16
CMP_HIDDEN = 4 * A_HEAD_DIM
SEL_BLOCK = 64
SEL_TOPK = 16
SEL_QCHUNK = 32
WIN_SIZE = 512
FORCE_BONUS = 1.0e4
B_HEADS = 8
B_HEAD_DIM = 128
B_WIDTH = B_HEADS * B_HEAD_DIM
DILATION_PATTERNS = ((128, 1), (512, 4), (2048, 16))
BAND_BLOCK = 128
ROPE_THETA = 500000.0
ROPE_FRACTION = 4
EPS = 1e-6
NEG = -1e30
IN_SIZES = (A_WIDTH,) + (A_KV_WIDTH,) * 6 + (A_HEADS * 3, A_WIDTH) + (B_WIDTH,) * 4 + (D_MODEL, D_MODEL)
N_IN = sum(IN_SIZES)
IN_SPLITS = [int(s) for s in np.cumsum(IN_SIZES)[:-1]]

kernel_name = 'hybrid_nsa_dilated_gated_block'

F32 = jnp.float32


def rms_norm(x, g):
    xf = x.astype(F32)
    y = xf * lax.rsqrt(jnp.mean(xf * xf, axis=-1, keepdims=True) + EPS)
    return (y * g.astype(F32)).astype(x.dtype)


def partial_rope(x, pos):
    dh = x.shape[-1]
    rot = dh // ROPE_FRACTION
    half = rot // 2
    inv = (ROPE_THETA ** (-np.arange(0, rot, 2) / rot)).astype(np.float32)
    ang = pos.astype(F32)[..., None] * inv
    ang = ang.reshape(ang.shape[:2] + (1,) * (x.ndim - 3) + (half,))
    cos, sin = jnp.cos(ang), jnp.sin(ang)
    x1 = x[..., :half].astype(F32)
    x2 = x[..., half:rot].astype(F32)
    out = jnp.concatenate([x1 * cos - x2 * sin, x2 * cos + x1 * sin, x[..., rot:].astype(F32)], axis=-1)
    return out.astype(x.dtype)


def banded_attention(q, k, v, max_dist):
    N, L, G, R, dh = q.shape
    bs = math.gcd(L, BAND_BLOCK)
    nb = L // bs
    span = max_dist + bs
    kp = jnp.pad(k, ((0, 0), (max_dist, 0), (0, 0), (0, 0)))
    vp = jnp.pad(v, ((0, 0), (max_dist, 0), (0, 0), (0, 0)))
    idx = np.arange(nb)[:, None] * bs + np.arange(span)[None, :]
    kb = kp[:, idx]
    vb = vp[:, idx]
    qb = q.reshape(N, nb, bs, G, R, dh)
    s = jnp.einsum('nbqgrd,nbkgd->nbgrqk', qb, kb, preferred_element_type=F32) * (dh ** -0.5)
    qpos = np.arange(nb)[:, None] * bs + np.arange(bs)[None, :]
    kpos = idx - max_dist
    diff = qpos[:, :, None] - kpos[:, None, :]
    mask = (diff >= 0) & (diff <= max_dist) & (kpos[:, None, :] >= 0)
    s = jnp.where(mask[None, :, None, None], s, NEG)
    lse = jax.nn.logsumexp(s, axis=-1)
    p = jnp.exp(s - lse[..., None])
    o = jnp.einsum('nbgrqk,nbkgd->nbqgrd', p.astype(vb.dtype), vb, preferred_element_type=F32)
    o = o.reshape(N, L, G, R, dh)
    lse = lse.transpose(0, 1, 4, 2, 3).reshape(N, L, G, R)
    return o, lse


def dilated_mixture(q, k, v):
    B, S, H, dh = q.shape
    outs, lses = [], []
    for w, r in DILATION_PATTERNS:
        L = S // r
        def to_cls(t):
            return t.reshape(B, L, r, H, dh).transpose(0, 2, 1, 3, 4).reshape(B * r, L, H, dh)
        o, lse = banded_attention(to_cls(q)[:, :, :, None], to_cls(k), to_cls(v), w // r)
        outs.append(o.reshape(B, r, L, H, dh).transpose(0, 2, 1, 3, 4).reshape(B, S, H, dh))
        lses.append(lse.reshape(B, r, L, H).transpose(0, 2, 1, 3).reshape(B, S, H))
    wts = jax.nn.softmax(jnp.stack(lses, axis=0), axis=0)
    return jnp.sum(wts[..., None] * jnp.stack(outs, axis=0), axis=0)


def nsa_attention(q, k_cmp, v_cmp, k_sel, v_sel, k_win, v_win, gate_logits, pos,
                  pe_ck, pe_cv, w_ck1, w_ck2, w_cv1, w_cv2):
    B, S, G, R, dh = q.shape
    scale = dh ** -0.5
    tq = np.arange(S)
    n_c = (S - CMP_BLOCK) // CMP_STRIDE + 1
    cidx = np.arange(n_c)[:, None] * CMP_STRIDE + np.arange(CMP_BLOCK)[None, :]
    cmp_end = cidx[:, -1]
    def compress(t, pe, w1, w2):
        blk = t[:, cidx] + pe[None, None, :, None, :]
        blk = blk.transpose(0, 1, 3, 2, 4).reshape(B, n_c, G, CMP_BLOCK * dh)
        return jax.nn.gelu(blk @ w1) @ w2
    kc = partial_rope(compress(k_cmp, pe_ck, w_ck1, w_ck2), pos[:, cmp_end])
    vc = compress(v_cmp, pe_cv, w_cv1, w_cv2)
    cmask = (cmp_end[None, :] <= tq[:, None])[None, :, None, None, :]
    s = jnp.einsum('bsgrd,bcgd->bsgrc', q, kc, preferred_element_type=F32) * scale
    s = jnp.where(cmask, s, NEG)
    e = jnp.exp(s - jnp.max(s, axis=-1, keepdims=True)) * cmask
    p_cmp = e / jnp.maximum(jnp.sum(e, axis=-1, keepdims=True), 1.0)
    o_cmp = jnp.einsum('bsgrc,bcgd->bsgrd', p_cmp.astype(vc.dtype), vc, preferred_element_type=F32)
    n_s = S // SEL_BLOCK
    cs = cidx[:, 0]
    ss = np.arange(n_s) * SEL_BLOCK
    overlap = np.clip(np.minimum(cs[:, None] + CMP_BLOCK, ss[None, :] + SEL_BLOCK)
                      - np.maximum(cs[:, None], ss[None, :]), 0, None).astype(np.float32) / CMP_BLOCK
    imp = jnp.einsum('bsgc,cj->bsgj', jnp.sum(p_cmp, axis=3), overlap)
    jb = np.arange(n_s)[None, :]
    cur = (tq // SEL_BLOCK)[:, None]
    valid = ss[None, :] <= tq[:, None]
    forced = (valid & ((jb == 0) | (jb == cur) | (jb == cur - 1))).astype(np.float32)
    score = jnp.where(valid[None, :, None, :], imp + FORCE_BONUS * forced[None, :, None, :], NEG)
    n_sel = min(SEL_TOPK, n_s)
    _, sel_idx = lax.top_k(score, n_sel)
    K = n_sel * SEL_BLOCK
    Qc = SEL_QCHUNK
    nq = S // Qc
    ksT = k_sel.transpose(0, 2, 1, 3)
    vsT = v_sel.transpose(0, 2, 1, 3)
    gather = jax.vmap(jax.vmap(lambda a, i: jnp.take(a, i, axis=0)))
    def sel_chunk(args):
        qb, ib, tb = args
        tok = ib[..., None] * SEL_BLOCK + jnp.arange(SEL_BLOCK, dtype=jnp.int32)
        tok = tok.reshape(B, Qc, G, K).transpose(0, 2, 1, 3)
        flat = tok.reshape(B, G, Qc * K)
        kg = gather(ksT, flat).reshape(B, G, Qc, K, dh)
        vg = gather(vsT, flat).reshape(B, G, Qc, K, dh)
        sc = jnp.einsum('bqgrd,bgqkd->bgrqk', qb, kg, preferred_element_type=F32) * scale
        m = (tok <= tb[None, None, :, None])[:, :, None]
        pr = jax.nn.softmax(jnp.where(m, sc, NEG), axis=-1)
        return jnp.einsum('bgrqk,bgqkd->bqgrd', pr.astype(vg.dtype), vg, preferred_element_type=F32)
    qc = q.reshape(B, nq, Qc, G, R, dh).transpose(1, 0, 2, 3, 4, 5)
    ic = sel_idx.reshape(B, nq, Qc, G, n_sel).transpose(1, 0, 2, 3, 4)
    tc = jnp.asarray(tq.reshape(nq, Qc), dtype=jnp.int32)
    o_sel = lax.map(sel_chunk, (qc, ic, tc)).transpose(1, 0, 2, 3, 4, 5).reshape(B, S, G, R, dh)
    o_win, _ = banded_attention(q, k_win, v_win, WIN_SIZE - 1)
    g = jax.nn.sigmoid(gate_logits.astype(F32)).reshape(B, S, G, R, 3)
    o = g[..., 0:1] * o_cmp + g[..., 1:2] * o_sel + g[..., 2:3] * o_win
    return o.reshape(B, S, G * R * dh)


def hybrid_layer(x, c, pos, w_ada, b_ada, g_pre, g_post, w_in, pe_ck, pe_cv, w_ck1, w_ck2,
                 w_cv1, w_cv2, w_br_a, w_br_b, w_out):
    B, S, D = x.shape
    ada = c @ w_ada + b_ada
    shift, scale, gate = jnp.split(ada, 3, axis=-1)
    h = rms_norm(x, g_pre) * (1.0 + scale[:, None, :]) + shift[:, None, :]
    proj = h @ w_in
    (qa, kc, vc, ks, vs, kw, vw, ga, za, qb, kb, vb, zb, ma, mb) = jnp.split(proj, IN_SPLITS, axis=-1)
    kv = lambda t: t.reshape(B, S, A_KV_GROUPS, A_HEAD_DIM)
    qa = partial_rope(qa.reshape(B, S, A_KV_GROUPS, A_REP, A_HEAD_DIM), pos)
    oa = nsa_attention(qa, kv(kc), kv(vc), partial_rope(kv(ks), pos), kv(vs),
                       partial_rope(kv(kw), pos), kv(vw), ga, pos,
                       pe_ck, pe_cv, w_ck1, w_ck2, w_cv1, w_cv2)
    ya = (oa * jax.nn.silu(za.astype(F32))).astype(x.dtype) @ w_br_a
    hd = lambda t: t.reshape(B, S, B_HEADS, B_HEAD_DIM)
    ob = dilated_mixture(partial_rope(hd(qb), pos), partial_rope(hd(kb), pos), hd(vb))
    ob = ob.reshape(B, S, B_WIDTH)
    yb = (ob * jax.nn.silu(zb.astype(F32))).astype(x.dtype) @ w_br_b
    merged = jax.nn.sigmoid(ma) * ya + jax.nn.sigmoid(mb) * yb
    out = merged @ w_out
    return x + (gate[:, None, :] * rms_norm(out, g_post)).astype(x.dtype)


def setup_inputs(seed: int = 0) -> dict:
    key = jax.random.key(seed)
    ks = jax.random.split(key, 20)
    nrm = lambda k, shape, s: jax.random.normal(k, shape, dtype=F32) * s
    D = D_MODEL
    x = nrm(ks[0], (BATCH, SEQ, D), 1.0)
    c = nrm(ks[1], (BATCH, D), 1.0)
    positions = (jnp.arange(SEQ, dtype=jnp.int32)[None, :]
                 + jax.random.randint(ks[2], (BATCH, 1), 0, 4096, dtype=jnp.int32))
    return {
        'x': x,
        'c': c,
        'positions': positions,
        'w_ada': nrm(ks[3], (DEPTH, D, 3 * D), 0.5 * D ** -0.5),
        'b_ada': nrm(ks[4], (DEPTH, 3 * D), 0.01),
        'g_pre': 1.0 + nrm(ks[5], (DEPTH, D), 0.02),
        'g_post': 1.0 + nrm(ks[6], (DEPTH, D), 0.02),
        'w_in': nrm(ks[7], (DEPTH, D, N_IN), D ** -0.5),
        'pe_ck': nrm(ks[8], (DEPTH, CMP_BLOCK, A_HEAD_DIM), 0.02),
        'pe_cv': nrm(ks[9], (DEPTH, CMP_BLOCK, A_HEAD_DIM), 0.02),
        'w_ck1': nrm(ks[10], (DEPTH, CMP_BLOCK * A_HEAD_DIM, CMP_HIDDEN), (CMP_BLOCK * A_HEAD_DIM) ** -0.5),
        'w_ck2': nrm(ks[11], (DEPTH, CMP_HIDDEN, A_HEAD_DIM), CMP_HIDDEN ** -0.5),
        'w_cv1': nrm(ks[12], (DEPTH, CMP_BLOCK * A_HEAD_DIM, CMP_HIDDEN), (CMP_BLOCK * A_HEAD_DIM) ** -0.5),
        'w_cv2': nrm(ks[13], (DEPTH, CMP_HIDDEN, A_HEAD_DIM), CMP_HIDDEN ** -0.5),
        'w_br_a': nrm(ks[14], (DEPTH, A_WIDTH, D), A_WIDTH ** -0.5),
        'w_br_b': nrm(ks[15], (DEPTH, B_WIDTH, D), B_WIDTH ** -0.5),
        'w_out': nrm(ks[16], (DEPTH, D, D), D ** -0.5),
    }


def reference(x, c, positions, w_ada, b_ada, g_pre, g_post, w_in, pe_ck, pe_cv, w_ck1, w_ck2,
              w_cv1, w_cv2, w_br_a, w_br_b, w_out):
    h = x
    for layer in range(DEPTH):
        h = hybrid_layer(h, c, positions, w_ada[layer], b_ada[layer], g_pre[layer], g_post[layer],
                         w_in[layer], pe_ck[layer], pe_cv[layer], w_ck1[layer], w_ck2[layer],
                         w_cv1[layer], w_cv2[layer], w_br_a[layer], w_br_b[layer], w_out[layer])
    return h
```

```python
import functools
import math

import numpy as np
import jax
import jax.numpy as jnp
from jax import lax
from jax.experimental import pallas as pl
from jax.experimental.pallas import tpu as pltpu

F32 = jnp.float32
BF16 = jnp.bfloat16

D_MODEL = 2048
A_HEADS = 16
A_HEAD_DIM = 64
A_KV_GROUPS = 4
A_REP = A_HEADS // A_KV_GROUPS
A_WIDTH = A_HEADS * A_HEAD_DIM
A_KV_WIDTH = A_KV_GROUPS * A_HEAD_DIM
CMP_BLOCK = 32
CMP_STRIDE = 16
CMP_HIDDEN = 4 * A_HEAD_DIM
SEL_BLOCK = 64
SEL_TOPK = 16
WIN_SIZE = 512
FORCE_BONUS = 1.0e4
B_HEADS = 8
B_HEAD_DIM = 128
B_WIDTH = B_HEADS * B_HEAD_DIM
ROPE_THETA = 500000.0
EPS = 1e-6
NEG = -1e30
M_INIT = -1.0e38

LANES = 128
VMEM_LIMIT = 56 * 1024 * 1024

TN = 512
COL_QA = 0
COL_KS = 1024
COL_KW = 1536
COL_QB = 2048
COL_KB = 3072
COL_VB = 4096
COL_ZA = 5120
COL_ZB = 6144
COL_VS = 7168
COL_VW = 7680
COL_MA = 8192
COL_MB = 10240
COL_KCV = 12288
N_SLAB = 12800
GATE_LANE = 96
AUG_LANE = 64

T_ATT = 256


def _cparams(sem):
    return pltpu.CompilerParams(dimension_semantics=sem, vmem_limit_bytes=VMEM_LIMIT)


def _ada_kernel(c_ref, w_ref, b_ref, o_ref):
    o_ref[...] = jnp.dot(c_ref[...], w_ref[...], preferred_element_type=F32,
                         precision=lax.Precision.HIGHEST) + b_ref[...]


def _ada(c, w_ada, b_ada):
    B, D = c.shape
    N = w_ada.shape[1]
    tn = 768
    return pl.pallas_call(
        _ada_kernel,
        grid=(N // tn,),
        in_specs=[pl.BlockSpec((B, D), lambda j: (0, 0)),
                  pl.BlockSpec((D, tn), lambda j: (0, j)),
                  pl.BlockSpec((1, tn), lambda j: (0, j))],
        out_specs=pl.BlockSpec((B, tn), lambda j: (0, j)),
        out_shape=jax.ShapeDtypeStruct((B, N), F32),
        compiler_params=_cparams(("arbitrary",)),
        name="ada",
    )(c, w_ada, b_ada.reshape(1, N))


def _rope_inv_tables():
    lane = np.arange(LANES)
    inv_a8 = (ROPE_THETA ** (-np.arange(0, 16, 2) / 16)).astype(np.float32)
    inv_b16 = (ROPE_THETA ** (-np.arange(0, 32, 2) / 32)).astype(np.float32)
    la = lane % A_HEAD_DIM
    inv_a = np.where(la < 16, inv_a8[la % 8], 0.0).astype(np.float32)
    inv_b = np.where(lane < 32, inv_b16[lane % 16], 0.0).astype(np.float32)
    return inv_a.reshape(1, LANES), inv_b.reshape(1, LANES)


def _rope_tables(pos_f, inv, period, half):
    ang = pos_f * inv
    c, s = jnp.cos(ang), jnp.sin(ang)
    lane = lax.broadcasted_iota(jnp.int32, ang.shape, 1) % period
    cos_t = jnp.where(lane < 2 * half, c, 1.0)
    sin_m = jnp.where(lane < half, -s, 0.0)
    sin_p = jnp.where((lane >= half) & (lane < 2 * half), s, 0.0)
    return cos_t, sin_m, sin_p


def _rope_apply(x, cos_t, sin_m, sin_p, half):
    return (x * cos_t + pltpu.roll(x, LANES - half, 1) * sin_m
            + pltpu.roll(x, half, 1) * sin_p)


def _proj_kernel(x_ref, shift_ref, scale_ref, gpre_ref, pos_ref, inva_ref, invb_ref, w_ref,
                 o_ref, h_scr, ca, sma, spa, cb, smb, spb, *, tm, seq):
    i = pl.program_id(0)
    j = pl.program_id(1)

    @pl.when(j == 0)
    def _():
        x = x_ref[...]
        ms = jnp.mean(x * x, axis=-1, keepdims=True)
        y = x * lax.rsqrt(ms + EPS) * gpre_ref[...]
        h = y * (1.0 + scale_ref[0]) + shift_ref[0]
        h_scr[...] = h.astype(BF16)
        pos_f = pos_ref[...].astype(F32)
        c, sm, sp = _rope_tables(pos_f, inva_ref[...], A_HEAD_DIM, 8)
        ca[...] = c
        sma[...] = sm
        spa[...] = sp
        c, sm, sp = _rope_tables(pos_f, invb_ref[...], LANES, 16)
        cb[...] = c
        smb[...] = sm
        spb[...] = sp

    acc = jnp.dot(h_scr[...], w_ref[...], preferred_element_type=F32)
    nchunk = TN // LANES

    def store_rope(tabs, half, mul, first_head_only, onehot):
        cos_t, sin_m, sin_p = tabs[0][...], tabs[1][...], tabs[2][...]
        lane = lax.broadcasted_iota(jnp.int32, (tm, LANES), 1)
        if first_head_only:
            keep = lane < A_HEAD_DIM
            cos_t = jnp.where(keep, cos_t, 1.0)
            sin_m = jnp.where(keep, sin_m, 0.0)
            sin_p = jnp.where(keep, sin_p, 0.0)
        if onehot:
            t = (i * tm + lax.broadcasted_iota(jnp.int32, (tm, LANES), 0)) % seq
            hot = (lane >= AUG_LANE) & (lane < AUG_LANE + seq // SEL_BLOCK) & \
                  ((t // SEL_BLOCK) == (lane - AUG_LANE))
        for cidx in range(nchunk):
            xc = acc[:, cidx * LANES:(cidx + 1) * LANES]
            r = _rope_apply(xc, cos_t, sin_m, sin_p, half)
            if mul != 1.0:
                r = r * mul
            if onehot:
                r = jnp.where(hot, 1.0, r)
            o_ref[:, cidx * LANES:(cidx + 1) * LANES] = r.astype(o_ref.dtype)

    ta = (ca, sma, spa)
    tb = (cb, smb, spb)

    @pl.when(j < COL_KS // TN)
    def _():
        store_rope(ta, 8, A_HEAD_DIM ** -0.5, False, False)

    @pl.when(j == COL_KS // TN)
    def _():
        store_rope(ta, 8, 1.0, True, True)

    @pl.when(j == COL_KW // TN)
    def _():
        store_rope(ta, 8, 1.0, True, False)

    @pl.when((j >= COL_QB // TN) & (j < COL_KB // TN))
    def _():
        store_rope(tb, 16, B_HEAD_DIM ** -0.5, False, False)

    @pl.when((j >= COL_KB // TN) & (j < COL_VB // TN))
    def _():
        store_rope(tb, 16, 1.0, False, False)

    @pl.when(j >= COL_VB // TN)
    def _():
        o_ref[...] = acc.astype(o_ref.dtype)


def _proj(x2d, ada3, g_pre, pos2d, w_slab, seq):
    M, D = x2d.shape
    tm = 1024
    per_b = seq // tm
    inv_a, inv_b = _rope_inv_tables()
    tab = pltpu.VMEM((tm, LANES), F32)
    return pl.pallas_call(
        functools.partial(_proj_kernel, tm=tm, seq=seq),
        grid=(M // tm, N_SLAB // TN),
        in_specs=[pl.BlockSpec((tm, D), lambda i, j: (i, 0)),
                  pl.BlockSpec((1, 1, D), lambda i, j: (i // per_b, 0, 0)),
                  pl.BlockSpec((1, 1, D), lambda i, j: (i // per_b, 0, 1)),
                  pl.BlockSpec((1, D), lambda i, j: (0, 0)),
                  pl.BlockSpec((tm, 1), lambda i, j: (i, 0)),
                  pl.BlockSpec((1, LANES), lambda i, j: (0, 0)),
                  pl.BlockSpec((1, LANES), lambda i, j: (0, 0)),
                  pl.BlockSpec((D, TN), lambda i, j: (0, j))],
        out_specs=pl.BlockSpec((tm, TN), lambda i, j: (i, j)),
        out_shape=jax.ShapeDtypeStruct((M, N_SLAB), BF16),
        scratch_shapes=[pltpu.VMEM((tm, D), BF16), tab, tab, tab, tab, tab, tab],
        compiler_params=_cparams(("arbitrary", "arbitrary")),
        name="proj",
    )(x2d, ada3, ada3, g_pre.reshape(1, D), pos2d, jnp.asarray(inv_a), jnp.asarray(inv_b), w_slab)


def _build_w_slab(w_in):
    D = w_in.shape[0]
    sizes = (A_WIDTH,) + (A_KV_WIDTH,) * 6 + (A_HEADS * 3, A_WIDTH) + (B_WIDTH,) * 4 + (D_MODEL, D_MODEL)
    offs = np.concatenate([[0], np.cumsum(sizes)])
    seg = lambda k: w_in[:, int(offs[k]):int(offs[k + 1])]
    qa, kc, vc, ks, vs, kw, vw, ga, za, qb, kb, vb, zb, ma, mb = [seg(k) for k in range(15)]
    z = lambda n: jnp.zeros((D, n), w_in.dtype)
    grp = lambda t, g: t[:, g * A_HEAD_DIM:(g + 1) * A_HEAD_DIM]
    ngate = A_REP * 3
    ks_aug, kw_pad, vs_pad, vw_pad, kcv = [], [], [], [], []
    for g in range(A_KV_GROUPS):
        ks_aug += [grp(ks, g), z(GATE_LANE - A_HEAD_DIM), ga[:, g * ngate:(g + 1) * ngate],
                   z(LANES - GATE_LANE - ngate)]
        kw_pad += [grp(kw, g), z(LANES - A_HEAD_DIM)]
        vs_pad += [grp(vs, g), z(LANES - A_HEAD_DIM)]
        vw_pad += [grp(vw, g), z(LANES - A_HEAD_DIM)]
        kcv += [grp(kc, g), grp(vc, g)]
    slab = jnp.concatenate([qa] + ks_aug + kw_pad + [qb, kb, vb, za, zb] + vs_pad + vw_pad
                           + [ma, mb] + kcv, axis=1)
    assert slab.shape[1] == N_SLAB
    return slab.astype(BF16)


def _cmp_kv_kernel(t_ref, posc_ref, inva_ref, wk1_ref, wk2_ref, wv1_ref, wv2_ref, pek_ref, pev_ref,
                   kc2_ref, vc2_ref):
    half_in = t_ref.shape[-1]

    def compress(idx, w1_ref, w2_ref, pe_ref):
        tg = t_ref[idx]
        p1 = jnp.dot(tg, w1_ref[0:half_in, :], preferred_element_type=F32)
        p2 = jnp.dot(tg, w1_ref[half_in:2 * half_in, :], preferred_element_type=F32)
        bias = jnp.dot(pe_ref[...], w1_ref[...], preferred_element_type=F32)[0:1, :]
        hid = p1 + pltpu.roll(p2, p2.shape[0] - 1, 0) + bias
        act = jax.nn.gelu(hid)
        return jnp.dot(act.astype(BF16), w2_ref[...], preferred_element_type=F32)

    kk = compress(0, wk1_ref, wk2_ref, pek_ref)
    vv = compress(1, wv1_ref, wv2_ref, pev_ref)
    pos_f = posc_ref[...].astype(F32)
    cos_t, sin_m, sin_p = _rope_tables(pos_f, inva_ref[...], A_HEAD_DIM, 8)
    kk = _rope_apply(kk, cos_t, sin_m, sin_p, 8)
    lane = lax.broadcasted_iota(jnp.int32, kk.shape, 1)
    lo = lane < A_HEAD_DIM
    n = kk.shape[0]
    kc2_ref[0:n, :] = jnp.where(lo, kk, 0.0).astype(kc2_ref.dtype)
    kc2_ref[n:2 * n, :] = jnp.where(lo, 0.0, kk).astype(kc2_ref.dtype)
    vc2_ref[0:n, :] = jnp.where(lo, vv, 0.0).astype(vc2_ref.dtype)
    vc2_ref[n:2 * n, :] = jnp.where(lo, 0.0, vv).astype(vc2_ref.dtype)


def _cmp_kv(t5, posc, wk1, wk2d, wv1, wv2d, pek, pev):
    B, G, _, NR, W = t5.shape
    inv_a, _ = _rope_inv_tables()
    full = lambda a: pl.BlockSpec(a.shape, lambda b, g: (0,) * a.ndim)
    out = jax.ShapeDtypeStruct((B, G, 2 * NR, LANES), BF16)
    ospec = pl.BlockSpec((None, None, 2 * NR, LANES), lambda b, g: (b, g, 0, 0))
    return pl.pallas_call(
        _cmp_kv_kernel,
        grid=(B, G),
        in_specs=[pl.BlockSpec((None, None, 2, NR, W), lambda b, g: (b, g, 0, 0, 0)),
                  pl.BlockSpec((None, NR, 1), lambda b, g: (b, 0, 0)),
                  pl.BlockSpec((1, LANES), lambda b, g: (0, 0)),
                  full(wk1), full(wk2d), full(wv1), full(wv2d), full(pek), full(pev)],
        out_specs=[ospec, ospec],
        out_shape=[out, out],
        compiler_params=_cparams(("arbitrary", "arbitrary")),
        name="cmp_kv",
    )(t5, posc, jnp.asarray(inv_a), wk1, wk2d, wv1, wv2d, pek, pev)


def _overlap_table(seq):
    n_c = (seq - CMP_BLOCK) // CMP_STRIDE + 1
    n_s = seq // SEL_BLOCK
    cs = np.arange(n_c) * CMP_STRIDE
    ss = np.arange(n_s) * SEL_BLOCK
    ov = np.clip(np.minimum(cs[:, None] + CMP_BLOCK, ss[None, :] + SEL_BLOCK)
                 - np.maximum(cs[:, None], ss[None, :]), 0, None).astype(np.float32) / CMP_BLOCK
    full = np.zeros((LANES, LANES), np.float32)
    for rep in range(LANES // n_s):
        full[:n_c, rep * n_s:(rep + 1) * n_s] = ov
    return full


def _split3(x):
    hi = x.astype(BF16)
    r1 = x - hi.astype(F32)
    mid = r1.astype(BF16)
    lo = (r1 - mid.astype(F32)).astype(BF16)
    return hi, mid, lo


def _cmp_attn_kernel(q_ref, kc2_ref, vc2_ref, ov_ref, ocmp_ref, qaug_ref, *, tq, n_sel_blocks):
    qi = pl.program_id(2)
    t = qi * tq + lax.broadcasted_iota(jnp.int32, (tq, LANES), 0)
    lane = lax.broadcasted_iota(jnp.int32, (tq, LANES), 1)
    cmask = (CMP_STRIDE * lane + (CMP_BLOCK - 1)) <= t
    cmask_f = cmask.astype(F32)
    nt = (((1,), (1,)), ((), ()))
    psum = jnp.zeros((tq, LANES), F32)
    for pair in range(A_REP // 2):
        qp = q_ref[:, pair * LANES:(pair + 1) * LANES]
        s2 = lax.dot_general(qp, kc2_ref[...], nt, preferred_element_type=F32)
        probs = []
        for hh in range(2):
            s = jnp.where(cmask, s2[:, hh * LANES:(hh + 1) * LANES], NEG)
            mx = jnp.max(s, axis=-1, keepdims=True)
            e = jnp.exp(s - mx) * cmask_f
            pr = e / jnp.maximum(jnp.sum(e, axis=-1, keepdims=True), 1.0)
            probs.append(pr)
            psum = psum + pr
        p2 = jnp.concatenate(probs, axis=1).astype(BF16)
        o_pair = jnp.dot(p2, vc2_ref[...], preferred_element_type=F32)
        ocmp_ref[:, pair * LANES:(pair + 1) * LANES] = o_pair.astype(ocmp_ref.dtype)

    ov = ov_ref[...]
    imp = sum(jnp.dot(part, ov, preferred_element_type=F32) for part in _split3(psum))
    jb = lane % n_sel_blocks
    cur = t // SEL_BLOCK
    valid = (SEL_BLOCK * jb) <= t
    forced = valid & ((jb == 0) | (jb == cur) | (jb == cur - 1))
    score = jnp.where(valid, imp + FORCE_BONUS * forced.astype(F32), NEG)
    rank = jnp.zeros((tq, LANES), F32)
    for d in range(1, n_sel_blocks):
        other = pltpu.roll(score, d, 1)
        beats = (other > score) | ((other == score) & (jb >= d))
        rank = rank + beats.astype(F32)
    selb = jnp.where(rank < float(min(SEL_TOPK, n_sel_blocks)), 0.0, NEG)
    aug = jnp.where((lane >= AUG_LANE) & (lane < AUG_LANE + n_sel_blocks), selb, 0.0).astype(BF16)
    lo = lane < A_HEAD_DIM
    for r in range(A_REP):
        qp = q_ref[:, (r // 2) * LANES:(r // 2 + 1) * LANES]
        if r % 2:
            qp = jnp.concatenate([qp[:, A_HEAD_DIM:], qp[:, :A_HEAD_DIM]], axis=1)
        qaug_ref[r] = jnp.where(lo, qp, aug)


def _cmp_attn(slab, kc2, vc2, B, seq):
    tq = T_ATT
    nq = seq // tq
    G = A_KV_GROUPS
    n_s = seq // SEL_BLOCK
    ov = jnp.asarray(_overlap_table(seq), BF16)
    kv_spec = pl.BlockSpec((None, None) + kc2.shape[2:], lambda b, g, q: (b, g, 0, 0))
    return pl.pallas_call(
        functools.partial(_cmp_attn_kernel, tq=tq, n_sel_blocks=n_s),
        grid=(B, G, nq),
        in_specs=[pl.BlockSpec((tq, A_REP * A_HEAD_DIM), lambda b, g, q: (b * nq + q, g)),
                  kv_spec, kv_spec,
                  pl.BlockSpec((LANES, LANES), lambda b, g, q: (0, 0))],
        out_specs=[pl.BlockSpec((tq, A_REP * A_HEAD_DIM), lambda b, g, q: (b * nq + q, g)),
                   pl.BlockSpec((None, None, A_REP, tq, LANES), lambda b, g, q: (b, g, 0, q, 0))],
        out_shape=[jax.ShapeDtypeStruct((B * seq, A_WIDTH), BF16),
                   jax.ShapeDtypeStruct((B, G, A_REP, seq, LANES), BF16)],
        compiler_params=_cparams(("arbitrary", "arbitrary", "arbitrary")),
        name="cmp_attn",
    )(slab, kc2, vc2, ov)


def _flash_init(m_scr, l_scr, acc_scr):
    m_scr[...] = jnp.full(m_scr.shape, M_INIT, F32)
    l_scr[...] = jnp.zeros(l_scr.shape, F32)
    acc_scr[...] = jnp.zeros(acc_scr.shape, F32)


def _flash_step(q, k, v, bias, reps, m_scr, l_scr, acc_scr):
    s = lax.dot_general(q, k, (((1,), (1,)), ((), ())), preferred_element_type=F32)
    if bias is not None:
        tq, tk = bias.shape
        s = (s.reshape(reps, tq, tk) + bias[None]).reshape(reps * tq, tk)
    m_prev = m_scr[...]
    m_new = jnp.maximum(m_prev, jnp.max(s, axis=-1, keepdims=True))
    alpha = jnp.exp(m_prev - m_new)
    p = jnp.exp(s - m_new)
    l_scr[...] = alpha * l_scr[...] + jnp.sum(p, axis=-1, keepdims=True)
    acc_scr[...] = alpha * acc_scr[...] + jnp.dot(p.astype(v.dtype), v, preferred_element_type=F32)
    m_scr[...] = m_new


def _tile_delta(t):
    return (lax.broadcasted_iota(jnp.int32, (t, t), 0) - lax.broadcasted_iota(jnp.int32, (t, t), 1))


def _nsa_flash_kernel(q_ref, k_ref, v_ref, o_ref, m_scr, l_scr, acc_scr, *, t, window):
    qi = pl.program_id(2)
    q = q_ref[...].reshape(A_REP * t, LANES)
    _flash_init(m_scr, l_scr, acc_scr)
    step = functools.partial(_flash_step, reps=A_REP, m_scr=m_scr, l_scr=l_scr, acc_scr=acc_scr)
    tile = lambda kj: (k_ref[pl.ds(pl.multiple_of(kj * t, t), t), :],
                       v_ref[pl.ds(pl.multiple_of(kj * t, t), t), :])
    delta = _tile_delta(t)
    if window is None:
        def body(kj, carry):
            k, v = tile(kj)
            step(q, k, v, None)
            return carry
        lax.fori_loop(0, qi, body, 0)
    else:
        back = (window - 1) // t + 1
        for off in range(back, 0, -1):
            lo_dist = (off - 1) * t + 1
            hi_dist = (off + 1) * t - 1

            @pl.when(qi >= off)
            def _(off=off, hi_dist=hi_dist):
                k, v = tile(qi - off)
                if hi_dist > window - 1:
                    bias = jnp.where(off * t + delta <= window - 1, 0.0, NEG)
                else:
                    bias = None
                step(q, k, v, bias)
    k, v = tile(qi)
    step(q, k, v, jnp.where(delta >= 0, 0.0, NEG))
    o = acc_scr[...] / l_scr[...]
    for pair in range(A_REP // 2):
        a = o[(2 * pair) * t:(2 * pair + 1) * t]
        b = o[(2 * pair + 1) * t:(2 * pair + 2) * t]
        o_ref[:, pair * LANES:(pair + 1) * LANES] = (a + pltpu.roll(b, A_HEAD_DIM, 1)).astype(o_ref.dtype)


def _nsa_flash(qaug, slab, col_k, col_v, window, B, seq):
    t = T_ATT
    nq = seq // t
    G = A_KV_GROUPS
    rows = A_REP * t
    return pl.pallas_call(
        functools.partial(_nsa_flash_kernel, t=t, window=window),
        grid=(B, G, nq),
        in_specs=[pl.BlockSpec((None, None, A_REP, t, LANES), lambda b, g, q: (b, g, 0, q, 0)),
                  pl.BlockSpec((seq, LANES), lambda b, g, q: (b, col_k // LANES + g)),
                  pl.BlockSpec((seq, LANES), lambda b, g, q: (b, col_v // LANES + g))],
        out_specs=pl.BlockSpec((t, A_REP * A_HEAD_DIM), lambda b, g, q: (b * nq + q, g)),
        out_shape=jax.ShapeDtypeStruct((B * seq, A_WIDTH), BF16),
        scratch_shapes=[pltpu.VMEM((rows, 1), F32), pltpu.VMEM((rows, 1), F32),
                        pltpu.VMEM((rows, LANES), F32)],
        compiler_params=_cparams(("arbitrary", "arbitrary", "arbitrary")),
        name="nsa_sel" if window is None else "nsa_win",
    )(qaug, slab, slab)


DILATIONS = ((128, 1), (512, 4), (2048, 16))


def _dil_log_mult(dist):
    mult = jnp.zeros(dist.shape, F32)
    for w, r in DILATIONS:
        mult = mult + ((dist >= 0) & (dist <= w) & ((dist & (r - 1)) == 0)).astype(F32)
    return jnp.where(mult > 2.5, math.log(3.0),
                     jnp.where(mult > 1.5, math.log(2.0), jnp.where(mult > 0.5, 0.0, NEG)))


def _dil_kernel(q_ref, k_ref, v_ref, o_ref, m_scr, l_scr, acc_scr, *, t, seq):
    qi = pl.program_id(2)
    q = q_ref[...]
    _flash_init(m_scr, l_scr, acc_scr)
    step = functools.partial(_flash_step, reps=1, m_scr=m_scr, l_scr=l_scr, acc_scr=acc_scr)
    tile = lambda kj: (k_ref[pl.ds(pl.multiple_of(kj * t, t), t), :],
                       v_ref[pl.ds(pl.multiple_of(kj * t, t), t), :])
    delta = _tile_delta(t)
    near = max((w + t - 1) // t for w, r in DILATIONS if w < seq)
    w_far, r_far = DILATIONS[-1]
    assert w_far >= seq and t % r_far == 0
    far_bias = jnp.where((delta & (r_far - 1)) == 0, 0.0, NEG)

    def body(kj, carry):
        k, v = tile(kj)
        step(q, k, v, far_bias)
        return carry
    lax.fori_loop(0, jnp.maximum(qi - near, 0), body, 0)
    for off in range(near, 0, -1):
        @pl.when(qi >= off)
        def _(off=off):
            k, v = tile(qi - off)
            step(q, k, v, _dil_log_mult(off * t + delta))
    k, v = tile(qi)
    step(q, k, v, _dil_log_mult(delta))
    o_ref[...] = (acc_scr[...] / l_scr[...]).astype(o_ref.dtype)


def _dil(slab, B, seq):
    t = T_ATT
    nq = seq // t
    H = B_HEADS
    return pl.pallas_call(
        functools.partial(_dil_kernel, t=t, seq=seq),
        grid=(B, H, nq),
        in_specs=[pl.BlockSpec((t, LANES), lambda b, h, q: (b * nq + q, COL_QB // LANES + h)),
                  pl.BlockSpec((seq, LANES), lambda b, h, q: (b, COL_KB // LANES + h)),
                  pl.BlockSpec((seq, LANES), lambda b, h, q: (b, COL_VB // LANES + h))],
        out_specs=pl.BlockSpec((t, LANES), lambda b, h, q: (b * nq + q, h)),
        out_shape=jax.ShapeDtypeStruct((B * seq, B_WIDTH), BF16),
        scratch_shapes=[pltpu.VMEM((t, 1), F32), pltpu.VMEM((t, 1), F32), pltpu.VMEM((t, LANES), F32)],
        compiler_params=_cparams(("arbitrary", "arbitrary", "arbitrary")),
        name="dilated",
    )(slab, slab, slab)


def _gate_expand_table():
    e = np.zeros((3, A_KV_GROUPS * LANES, A_WIDTH), np.float32)
    for g in range(A_KV_GROUPS):
        for r in range(A_REP):
            h = g * A_REP + r
            for br in range(3):
                e[br, g * LANES + GATE_LANE + r * 3 + br, h * A_HEAD_DIM:(h + 1) * A_HEAD_DIM] = 1.0
    return e


def _silu(z):
    return z * jax.nn.sigmoid(z)


def _out_kernel(ocmp_ref, osel_ref, owin_ref, ks_ref, za_ref, ob_ref, zb_ref, ma_ref, mb_ref,
                x_ref, gate_ref, gpost_ref, e_ref, wa_ref, wb_ref, wo_ref, o_ref):
    sg = jax.nn.sigmoid(ks_ref[...].astype(F32)).astype(BF16)
    oa = None
    for br, ref in enumerate((ocmp_ref, osel_ref, owin_ref)):
        gexp = jnp.dot(sg, e_ref[br], preferred_element_type=F32)
        term = gexp * ref[...].astype(F32)
        oa = term if oa is None else oa + term
    a_in = (oa * _silu(za_ref[...].astype(F32))).astype(BF16)
    ya = jnp.dot(a_in, wa_ref[...], preferred_element_type=F32)
    b_in = (ob_ref[...].astype(F32) * _silu(zb_ref[...].astype(F32))).astype(BF16)
    yb = jnp.dot(b_in, wb_ref[...], preferred_element_type=F32)
    merged = (jax.nn.sigmoid(ma_ref[...].astype(F32)) * ya
              + jax.nn.sigmoid(mb_ref[...].astype(F32)) * yb)
    out = jnp.dot(merged.astype(BF16), wo_ref[...], preferred_element_type=F32)
    y = out * lax.rsqrt(jnp.mean(out * out, axis=-1, keepdims=True) + EPS) * gpost_ref[...]
    o_ref[...] = x_ref[...] + gate_ref[0] * y


def _out(ocmp, osel, owin, ob, slab, x2d, ada3, g_post, wa, wb, wo, seq):
    M, D = x2d.shape
    tm = 256
    per_b = seq // tm
    e = jnp.asarray(_gate_expand_table(), BF16)
    const = lambda a: pl.BlockSpec(a.shape, lambda i: (0,) * a.ndim, pipeline_mode=pl.Buffered(1))
    row = lambda w, cb: pl.BlockSpec((tm, w), lambda i: (i, cb))
    return pl.pallas_call(
        _out_kernel,
        grid=(M // tm,),
        in_specs=[row(A_WIDTH, 0), row(A_WIDTH, 0), row(A_WIDTH, 0),
                  row(A_KV_GROUPS * LANES, COL_KS // (A_KV_GROUPS * LANES)),
                  row(A_WIDTH, COL_ZA // A_WIDTH), row(B_WIDTH, 0), row(B_WIDTH, COL_ZB // B_WIDTH),
                  row(D_MODEL, COL_MA // D_MODEL), row(D_MODEL, COL_MB // D_MODEL),
                  row(D, 0),
                  pl.BlockSpec((1, 1, D), lambda i: (i // per_b, 0, 2)),
                  pl.BlockSpec((1, D), lambda i: (0, 0)),
                  const(e), const(wa), const(wb), const(wo)],
        out_specs=pl.BlockSpec((tm, D), lambda i: (i, 0)),
        out_shape=jax.ShapeDtypeStruct((M, D), F32),
        compiler_params=_cparams(("arbitrary",)),
        name="out",
    )(ocmp, osel, owin, slab, slab, ob, slab, slab, slab, x2d, ada3, g_post.reshape(1, D), e, wa, wb, wo)


def _layer(x, c, positions, w_ada, b_ada, g_pre, g_post, w_in, pe_ck, pe_cv, w_ck1, w_ck2,
           w_cv1, w_cv2, w_br_a, w_br_b, w_out):
    B, S, D = x.shape
    G = A_KV_GROUPS
    x2d = x.reshape(B * S, D)
    ada3 = _ada(c, w_ada, b_ada).reshape(B, 1, 3 * D)
    slab = _proj(x2d, ada3, g_pre, positions.reshape(B * S, 1), _build_w_slab(w_in), S)

    kcv = slab[:, COL_KCV:COL_KCV + 2 * A_KV_WIDTH]
    t5 = kcv.reshape(B, S // CMP_STRIDE, CMP_STRIDE, G, 2, A_HEAD_DIM).transpose(0, 3, 4, 1, 2, 5)
    t5 = t5.reshape(B, G, 2, S // CMP_STRIDE, CMP_STRIDE * A_HEAD_DIM)
    n_c = (S - CMP_BLOCK) // CMP_STRIDE + 1
    cmp_end = np.minimum(np.arange(S // CMP_STRIDE) * CMP_STRIDE + CMP_BLOCK - 1, S - 1)
    posc = positions[:, cmp_end].reshape(B, S // CMP_STRIDE, 1)
    assert n_c == S // CMP_STRIDE - 1
    dup = lambda w: jnp.concatenate([w, w], axis=1).astype(BF16)
    pe8 = lambda pe: jnp.broadcast_to(pe.reshape(1, -1), (8, pe.size)).astype(BF16)
    kc2, vc2 = _cmp_kv(t5, posc, w_ck1.astype(BF16), dup(w_ck2), w_cv1.astype(BF16), dup(w_cv2),
                       pe8(pe_ck), pe8(pe_cv))

    ocmp, qaug = _cmp_attn(slab, kc2, vc2, B, S)
    osel = _nsa_flash(qaug, slab, COL_KS, COL_VS, None, B, S)
    owin = _nsa_flash(qaug, slab, COL_KW, COL_VW, WIN_SIZE, B, S)
    ob = _dil(slab, B, S)
    out = _out(ocmp, osel, owin, ob, slab, x2d, ada3, g_post,
               w_br_a.astype(BF16), w_br_b.astype(BF16), w_out.astype(BF16), S)
    return out.reshape(B, S, D)


def kernel(x, c, positions, w_ada, b_ada, g_pre, g_post, w_in, pe_ck, pe_cv, w_ck1, w_ck2, w_cv1, w_cv2, w_br_a, w_br_b, w_out):
    h = x
    for layer in range(w_ada.shape[0]):
        h = _layer(h, c, positions, w_ada[layer], b_ada[layer], g_pre[layer], g_post[layer],
                   w_in[layer], pe_ck[layer], pe_cv[layer], w_ck1[layer], w_ck2[layer],
                   w_cv1[layer], w_cv2[layer], w_br_a[layer], w_br_b[layer], w_out[layer])
    return h
```

```python
import functools
import math

import numpy as np
import jax
import jax.numpy as jnp
from jax import lax
from jax.experimental import pallas as pl
from jax.experimental.pallas import tpu as pltpu

F32 = jnp.float32
BF16 = jnp.bfloat16

D_MODEL = 2048
A_HEADS = 16
A_HEAD_DIM = 64
A_KV_GROUPS = 4
A_REP = A_HEADS // A_KV_GROUPS
A_WIDTH = A_HEADS * A_HEAD_DIM
A_KV_WIDTH = A_KV_GROUPS * A_HEAD_DIM
CMP_BLOCK = 32
CMP_STRIDE = 16
CMP_HIDDEN = 4 * A_HEAD_DIM
SEL_BLOCK = 64
SEL_TOPK = 16
WIN_SIZE = 512
FORCE_BONUS = 1.0e4
B_HEADS = 8
B_HEAD_DIM = 128
B_WIDTH = B_HEADS * B_HEAD_DIM
ROPE_THETA = 500000.0
EPS = 1e-6
NEG = -1e30
M_INIT = -1.0e38

LANES = 128
VMEM_LIMIT = 56 * 1024 * 1024

TN = 512
COL_QA = 0
COL_KS = 1024
COL_KW = 1536
COL_QB = 2048
COL_KB = 3072
COL_VB = 4096
COL_ZA = 5120
COL_ZB = 6144
COL_VS = 7168
COL_VW = 7680
COL_MA = 8192
COL_MB = 10240
COL_KCV = 12288
N_SLAB = 12800
GATE_LANE = 96
AUG_LANE = 64
ONES_LANE = 64

T_ATT = 256


def _cparams(sem):
    return pltpu.CompilerParams(dimension_semantics=sem, vmem_limit_bytes=VMEM_LIMIT)


def _ada_kernel(c_ref, w_ref, b_ref, o_ref):
    o_ref[...] = jnp.dot(c_ref[...], w_ref[...], preferred_element_type=F32,
                         precision=lax.Precision.HIGHEST) + b_ref[...]


def _ada(c, w_ada, b_ada):
    B, D = c.shape
    N = w_ada.shape[1]
    tn = 768
    return pl.pallas_call(
        _ada_kernel,
        grid=(N // tn,),
        in_specs=[pl.BlockSpec((B, D), lambda j: (0, 0)),
                  pl.BlockSpec((D, tn), lambda j: (0, j)),
                  pl.BlockSpec((1, tn), lambda j: (0, j))],
        out_specs=pl.BlockSpec((B, tn), lambda j: (0, j)),
        out_shape=jax.ShapeDtypeStruct((B, N), F32),
        compiler_params=_cparams(("arbitrary",)),
        name="ada",
    )(c, w_ada, b_ada.reshape(1, N))


def _rope_inv_tables():
    lane = np.arange(LANES)
    inv_a8 = (ROPE_THETA ** (-np.arange(0, 16, 2) / 16)).astype(np.float32)
    inv_b16 = (ROPE_THETA ** (-np.arange(0, 32, 2) / 32)).astype(np.float32)
    la = lane % A_HEAD_DIM
    inv_a = np.where(la < 16, inv_a8[la % 8], 0.0).astype(np.float32)
    inv_b = np.where(lane < 32, inv_b16[lane % 16], 0.0).astype(np.float32)
    return inv_a.reshape(1, LANES), inv_b.reshape(1, LANES)


def _rope_tables(pos_f, inv, period, half):
    ang = pos_f * inv
    c, s = jnp.cos(ang), jnp.sin(ang)
    lane = lax.broadcasted_iota(jnp.int32, ang.shape, 1) % period
    cos_t = jnp.where(lane < 2 * half, c, 1.0)
    sin_m = jnp.where(lane < half, -s, 0.0)
    sin_p = jnp.where((lane >= half) & (lane < 2 * half), s, 0.0)
    return cos_t, sin_m, sin_p


def _rope_apply(x, cos_t, sin_m, sin_p, half):
    return (x * cos_t + pltpu.roll(x, LANES - half, 1) * sin_m
            + pltpu.roll(x, half, 1) * sin_p)


def _proj_kernel(x_ref, shift_ref, scale_ref, gpre_ref, pos_ref, inva_ref, invb_ref, w_ref,
                 o_ref, h_scr, ca, sma, spa, cb, smb, spb, *, tm, seq):
    i = pl.program_id(0)
    j = pl.program_id(1)

    @pl.when(j == 0)
    def _():
        x = x_ref[...]
        ms = jnp.mean(x * x, axis=-1, keepdims=True)
        y = x * lax.rsqrt(ms + EPS) * gpre_ref[...]
        h = y * (1.0 + scale_ref[0]) + shift_ref[0]
        h_scr[...] = h.astype(BF16)
        pos_f = pos_ref[...].astype(F32)
        c, sm, sp = _rope_tables(pos_f, inva_ref[...], A_HEAD_DIM, 8)
        ca[...] = c
        sma[...] = sm
        spa[...] = sp
        c, sm, sp = _rope_tables(pos_f, invb_ref[...], LANES, 16)
        cb[...] = c
        smb[...] = sm
        spb[...] = sp

    acc = jnp.dot(h_scr[...], w_ref[...], preferred_element_type=F32)
    nchunk = TN // LANES

    def store_rope(tabs, half, mul, first_head_only, onehot):
        cos_t, sin_m, sin_p = tabs[0][...], tabs[1][...], tabs[2][...]
        lane = lax.broadcasted_iota(jnp.int32, (tm, LANES), 1)
        if first_head_only:
            keep = lane < A_HEAD_DIM
            cos_t = jnp.where(keep, cos_t, 1.0)
            sin_m = jnp.where(keep, sin_m, 0.0)
            sin_p = jnp.where(keep, sin_p, 0.0)
        if onehot:
            t = (i * tm + lax.broadcasted_iota(jnp.int32, (tm, LANES), 0)) % seq
            hot = (lane >= AUG_LANE) & (lane < AUG_LANE + seq // SEL_BLOCK) & \
                  ((t // SEL_BLOCK) == (lane - AUG_LANE))
        for cidx in range(nchunk):
            xc = acc[:, cidx * LANES:(cidx + 1) * LANES]
            r = _rope_apply(xc, cos_t, sin_m, sin_p, half)
            if mul != 1.0:
                r = r * mul
            if onehot:
                r = jnp.where(hot, 1.0, r)
            o_ref[:, cidx * LANES:(cidx + 1) * LANES] = r.astype(o_ref.dtype)

    ta = (ca, sma, spa)
    tb = (cb, smb, spb)

    @pl.when(j < COL_KS // TN)
    def _():
        store_rope(ta, 8, A_HEAD_DIM ** -0.5, False, False)

    @pl.when(j == COL_KS // TN)
    def _():
        store_rope(ta, 8, 1.0, True, True)

    @pl.when(j == COL_KW // TN)
    def _():
        store_rope(ta, 8, 1.0, True, False)

    @pl.when((j >= COL_QB // TN) & (j < COL_KB // TN))
    def _():
        store_rope(tb, 16, B_HEAD_DIM ** -0.5, False, False)

    @pl.when((j >= COL_KB // TN) & (j < COL_VB // TN))
    def _():
        store_rope(tb, 16, 1.0, False, False)

    is_vpad = (j == COL_VS // TN) | (j == COL_VW // TN)

    @pl.when((j >= COL_VB // TN) & jnp.logical_not(is_vpad))
    def _():
        o_ref[...] = acc.astype(o_ref.dtype)

    @pl.when(is_vpad)
    def _():
        lane = lax.broadcasted_iota(jnp.int32, acc.shape, 1) % LANES
        o_ref[...] = jnp.where(lane == ONES_LANE, 1.0, acc).astype(o_ref.dtype)


def _proj(x2d, ada3, g_pre, pos2d, w_slab, seq):
    M, D = x2d.shape
    tm = 1024
    per_b = seq // tm
    inv_a, inv_b = _rope_inv_tables()
    tab = pltpu.VMEM((tm, LANES), F32)
    return pl.pallas_call(
        functools.partial(_proj_kernel, tm=tm, seq=seq),
        grid=(M // tm, N_SLAB // TN),
        in_specs=[pl.BlockSpec((tm, D), lambda i, j: (i, 0)),
                  pl.BlockSpec((1, 1, D), lambda i, j: (i // per_b, 0, 0)),
                  pl.BlockSpec((1, 1, D), lambda i, j: (i // per_b, 0, 1)),
                  pl.BlockSpec((1, D), lambda i, j: (0, 0)),
                  pl.BlockSpec((tm, 1), lambda i, j: (i, 0)),
                  pl.BlockSpec((1, LANES), lambda i, j: (0, 0)),
                  pl.BlockSpec((1, LANES), lambda i, j: (0, 0)),
                  pl.BlockSpec((D, TN), lambda i, j: (0, j))],
        out_specs=pl.BlockSpec((tm, TN), lambda i, j: (i, j)),
        out_shape=jax.ShapeDtypeStruct((M, N_SLAB), BF16),
        scratch_shapes=[pltpu.VMEM((tm, D), BF16), tab, tab, tab, tab, tab, tab],
        compiler_params=_cparams(("arbitrary", "arbitrary")),
        name="proj",
    )(x2d, ada3, ada3, g_pre.reshape(1, D), pos2d, jnp.asarray(inv_a), jnp.asarray(inv_b), w_slab)


def _build_w_slab(w_in):
    D = w_in.shape[0]
    sizes = (A_WIDTH,) + (A_KV_WIDTH,) * 6 + (A_HEADS * 3, A_WIDTH) + (B_WIDTH,) * 4 + (D_MODEL, D_MODEL)
    offs = np.concatenate([[0], np.cumsum(sizes)])
    seg = lambda k: w_in[:, int(offs[k]):int(offs[k + 1])]
    qa, kc, vc, ks, vs, kw, vw, ga, za, qb, kb, vb, zb, ma, mb = [seg(k) for k in range(15)]
    z = lambda n: jnp.zeros((D, n), w_in.dtype)
    grp = lambda t, g: t[:, g * A_HEAD_DIM:(g + 1) * A_HEAD_DIM]
    ngate = A_REP * 3
    ks_aug, kw_pad, vs_pad, vw_pad, kcv = [], [], [], [], []
    for g in range(A_KV_GROUPS):
        ks_aug += [grp(ks, g), z(GATE_LANE - A_HEAD_DIM), ga[:, g * ngate:(g + 1) * ngate],
                   z(LANES - GATE_LANE - ngate)]
        kw_pad += [grp(kw, g), z(LANES - A_HEAD_DIM)]
        vs_pad += [grp(vs, g), z(LANES - A_HEAD_DIM)]
        vw_pad += [grp(vw, g), z(LANES - A_HEAD_DIM)]
        kcv += [grp(kc, g), grp(vc, g)]
    slab = jnp.concatenate([qa] + ks_aug + kw_pad + [qb, kb, vb, za, zb] + vs_pad + vw_pad
                           + [ma, mb] + kcv, axis=1)
    assert slab.shape[1] == N_SLAB
    return slab.astype(BF16)


def _cmp_kv_kernel(t_ref, posc_ref, inva_ref, wk1_ref, wk2_ref, wv1_ref, wv2_ref, pek_ref, pev_ref,
                   kc2_ref, vc2_ref):
    half_in = t_ref.shape[-1]

    def compress(idx, w1_ref, w2_ref, pe_ref):
        tg = t_ref[idx]
        p1 = jnp.dot(tg, w1_ref[0:half_in, :], preferred_element_type=F32)
        p2 = jnp.dot(tg, w1_ref[half_in:2 * half_in, :], preferred_element_type=F32)
        bias = jnp.dot(pe_ref[...], w1_ref[...], preferred_element_type=F32)[0:1, :]
        hid = p1 + pltpu.roll(p2, p2.shape[0] - 1, 0) + bias
        act = jax.nn.gelu(hid)
        return jnp.dot(act.astype(BF16), w2_ref[...], preferred_element_type=F32)

    kk = compress(0, wk1_ref, wk2_ref, pek_ref)
    vv = compress(1, wv1_ref, wv2_ref, pev_ref)
    pos_f = posc_ref[...].astype(F32)
    cos_t, sin_m, sin_p = _rope_tables(pos_f, inva_ref[...], A_HEAD_DIM, 8)
    kk = _rope_apply(kk, cos_t, sin_m, sin_p, 8)
    lane = lax.broadcasted_iota(jnp.int32, kk.shape, 1)
    lo = lane < A_HEAD_DIM
    n = kk.shape[0]
    kc2_ref[0:n, :] = jnp.where(lo, kk, 0.0).astype(kc2_ref.dtype)
    kc2_ref[n:2 * n, :] = jnp.where(lo, 0.0, kk).astype(kc2_ref.dtype)
    vc2_ref[0:n, :] = jnp.where(lo, vv, 0.0).astype(vc2_ref.dtype)
    vc2_ref[n:2 * n, :] = jnp.where(lo, 0.0, vv).astype(vc2_ref.dtype)


def _cmp_kv(t5, posc, wk1, wk2d, wv1, wv2d, pek, pev):
    B, G, _, NR, W = t5.shape
    inv_a, _ = _rope_inv_tables()
    full = lambda a: pl.BlockSpec(a.shape, lambda b, g: (0,) * a.ndim)
    out = jax.ShapeDtypeStruct((B, G, 2 * NR, LANES), BF16)
    ospec = pl.BlockSpec((None, None, 2 * NR, LANES), lambda b, g: (b, g, 0, 0))
    return pl.pallas_call(
        _cmp_kv_kernel,
        grid=(B, G),
        in_specs=[pl.BlockSpec((None, None, 2, NR, W), lambda b, g: (b, g, 0, 0, 0)),
                  pl.BlockSpec((None, NR, 1), lambda b, g: (b, 0, 0)),
                  pl.BlockSpec((1, LANES), lambda b, g: (0, 0)),
                  full(wk1), full(wk2d), full(wv1), full(wv2d), full(pek), full(pev)],
        out_specs=[ospec, ospec],
        out_shape=[out, out],
        compiler_params=_cparams(("arbitrary", "arbitrary")),
        name="cmp_kv",
    )(t5, posc, jnp.asarray(inv_a), wk1, wk2d, wv1, wv2d, pek, pev)


def _overlap_table(seq):
    n_c = (seq - CMP_BLOCK) // CMP_STRIDE + 1
    n_s = seq // SEL_BLOCK
    cs = np.arange(n_c) * CMP_STRIDE
    ss = np.arange(n_s) * SEL_BLOCK
    ov = np.clip(np.minimum(cs[:, None] + CMP_BLOCK, ss[None, :] + SEL_BLOCK)
                 - np.maximum(cs[:, None], ss[None, :]), 0, None).astype(np.float32) / CMP_BLOCK
    full = np.zeros((LANES, LANES), np.float32)
    for rep in range(LANES // n_s):
        full[:n_c, rep * n_s:(rep + 1) * n_s] = ov
    return full


def _split3(x):
    hi = x.astype(BF16)
    r1 = x - hi.astype(F32)
    mid = r1.astype(BF16)
    lo = (r1 - mid.astype(F32)).astype(BF16)
    return hi, mid, lo


def _cmp_attn_kernel(q_ref, kc2_ref, vc2_ref, ov_ref, ocmp_ref, qaug_ref, *, tq, n_sel_blocks):
    qi = pl.program_id(2)
    t = qi * tq + lax.broadcasted_iota(jnp.int32, (tq, LANES), 0)
    lane = lax.broadcasted_iota(jnp.int32, (tq, LANES), 1)
    cmask = (CMP_STRIDE * lane + (CMP_BLOCK - 1)) <= t
    cmask_f = cmask.astype(F32)
    nt = (((1,), (1,)), ((), ()))
    psum = jnp.zeros((tq, LANES), F32)
    for pair in range(A_REP // 2):
        qp = q_ref[:, pair * LANES:(pair + 1) * LANES]
        s2 = lax.dot_general(qp, kc2_ref[...], nt, preferred_element_type=F32)
        probs = []
        for hh in range(2):
            s = jnp.where(cmask, s2[:, hh * LANES:(hh + 1) * LANES], NEG)
            mx = jnp.max(s, axis=-1, keepdims=True)
            e = jnp.exp(s - mx) * cmask_f
            pr = e / jnp.maximum(jnp.sum(e, axis=-1, keepdims=True), 1.0)
            probs.append(pr)
            psum = psum + pr
        p2 = jnp.concatenate(probs, axis=1).astype(BF16)
        o_pair = jnp.dot(p2, vc2_ref[...], preferred_element_type=F32)
        ocmp_ref[:, pair * LANES:(pair + 1) * LANES] = o_pair.astype(ocmp_ref.dtype)

    ov = ov_ref[...]
    imp = sum(jnp.dot(part, ov, preferred_element_type=F32) for part in _split3(psum))
    jb = lane % n_sel_blocks
    cur = t // SEL_BLOCK
    valid = (SEL_BLOCK * jb) <= t
    forced = valid & ((jb == 0) | (jb == cur) | (jb == cur - 1))
    score = jnp.where(valid, imp + FORCE_BONUS * forced.astype(F32), NEG)
    rank = jnp.zeros((tq, LANES), F32)
    for d in range(1, n_sel_blocks):
        other = pltpu.roll(score, d, 1)
        beats = (other > score) | ((other == score) & (jb >= d))
        rank = rank + beats.astype(F32)
    selb = jnp.where(rank < float(min(SEL_TOPK, n_sel_blocks)), 0.0, NEG)
    aug = jnp.where((lane >= AUG_LANE) & (lane < AUG_LANE + n_sel_blocks), selb, 0.0).astype(BF16)
    lo = lane < A_HEAD_DIM
    for r in range(A_REP):
        qp = q_ref[:, (r // 2) * LANES:(r // 2 + 1) * LANES]
        if r % 2:
            qp = jnp.concatenate([qp[:, A_HEAD_DIM:], qp[:, :A_HEAD_DIM]], axis=1)
        qaug_ref[r] = jnp.where(lo, qp, aug)


def _cmp_attn(slab, kc2, vc2, B, seq):
    tq = T_ATT
    nq = seq // tq
    G = A_KV_GROUPS
    n_s = seq // SEL_BLOCK
    ov = jnp.asarray(_overlap_table(seq), BF16)
    kv_spec = pl.BlockSpec((None, None) + kc2.shape[2:], lambda b, g, q: (b, g, 0, 0))
    return pl.pallas_call(
        functools.partial(_cmp_attn_kernel, tq=tq, n_sel_blocks=n_s),
        grid=(B, G, nq),
        in_specs=[pl.BlockSpec((tq, A_REP * A_HEAD_DIM), lambda b, g, q: (b * nq + q, g)),
                  kv_spec, kv_spec,
                  pl.BlockSpec((LANES, LANES), lambda b, g, q: (0, 0))],
        out_specs=[pl.BlockSpec((tq, A_REP * A_HEAD_DIM), lambda b, g, q: (b * nq + q, g)),
                   pl.BlockSpec((None, None, A_REP, tq, LANES), lambda b, g, q: (b, g, 0, q, 0))],
        out_shape=[jax.ShapeDtypeStruct((B * seq, A_WIDTH), BF16),
                   jax.ShapeDtypeStruct((B, G, A_REP, seq, LANES), BF16)],
        compiler_params=_cparams(("arbitrary", "arbitrary", "arbitrary")),
        name="cmp_attn",
    )(slab, kc2, vc2, ov)


def _flash_init(m_scr, acc_scr, l_scr=None):
    m_scr[...] = jnp.full(m_scr.shape, M_INIT, F32)
    acc_scr[...] = jnp.zeros(acc_scr.shape, F32)
    if l_scr is not None:
        l_scr[...] = jnp.zeros(l_scr.shape, F32)


def _flash_step(q, k, v, bias, reps, m_scr, acc_scr, l_scr=None):
    s = lax.dot_general(q, k, (((1,), (1,)), ((), ())), preferred_element_type=F32)
    if bias is not None:
        tq, tk = bias.shape
        s = (s.reshape(reps, tq, tk) + bias[None]).reshape(reps * tq, tk)
    chunks = [s[:, c * LANES:(c + 1) * LANES] for c in range(s.shape[1] // LANES)]
    mx = functools.reduce(jnp.maximum, chunks)
    m_prev = m_scr[...]
    m_new = jnp.maximum(m_prev, jnp.max(mx, axis=-1, keepdims=True))
    alpha = jnp.exp(m_prev - m_new)
    ps = [jnp.exp(ch - m_new) for ch in chunks]
    p = jnp.concatenate(ps, axis=1).astype(v.dtype)
    acc_scr[...] = alpha * acc_scr[...] + jnp.dot(p, v, preferred_element_type=F32)
    if l_scr is not None:
        l_scr[...] = alpha * l_scr[...] + functools.reduce(jnp.add, ps)
    m_scr[...] = m_new


def _tile_delta(t):
    return (lax.broadcasted_iota(jnp.int32, (t, t), 0) - lax.broadcasted_iota(jnp.int32, (t, t), 1))


def _nsa_flash_kernel(q_ref, k_ref, v_ref, o_ref, m_scr, acc_scr, *, t, window):
    qi = pl.program_id(2)
    q = q_ref[...].reshape(A_REP * t, LANES)
    _flash_init(m_scr, acc_scr)
    step = functools.partial(_flash_step, reps=A_REP, m_scr=m_scr, acc_scr=acc_scr)
    tile = lambda kj: (k_ref[pl.ds(pl.multiple_of(kj * t, t), t), :],
                       v_ref[pl.ds(pl.multiple_of(kj * t, t), t), :])
    delta = _tile_delta(t)
    if window is None:
        def body(kj, carry):
            k, v = tile(kj)
            step(q, k, v, None)
            return carry
        lax.fori_loop(0, qi, body, 0)
    else:
        back = (window - 1) // t + 1
        for off in range(back, 0, -1):
            lo_dist = (off - 1) * t + 1
            hi_dist = (off + 1) * t - 1

            @pl.when(qi >= off)
            def _(off=off, hi_dist=hi_dist):
                k, v = tile(qi - off)
                if hi_dist > window - 1:
                    bias = jnp.where(off * t + delta <= window - 1, 0.0, NEG)
                else:
                    bias = None
                step(q, k, v, bias)
    k, v = tile(qi)
    step(q, k, v, jnp.where(delta >= 0, 0.0, NEG))
    acc = acc_scr[...]
    lane = lax.broadcasted_iota(jnp.int32, acc.shape, 1)
    o = jnp.where(lane < A_HEAD_DIM, acc / acc[:, ONES_LANE:ONES_LANE + 1], 0.0)
    for pair in range(A_REP // 2):
        a = o[(2 * pair) * t:(2 * pair + 1) * t]
        b = o[(2 * pair + 1) * t:(2 * pair + 2) * t]
        o_ref[:, pair * LANES:(pair + 1) * LANES] = (a + pltpu.roll(b, A_HEAD_DIM, 1)).astype(o_ref.dtype)


def _nsa_flash(qaug, slab, col_k, col_v, window, B, seq):
    t = T_ATT
    nq = seq // t
    G = A_KV_GROUPS
    rows = A_REP * t
    return pl.pallas_call(
        functools.partial(_nsa_flash_kernel, t=t, window=window),
        grid=(B, G, nq),
        in_specs=[pl.BlockSpec((None, None, A_REP, t, LANES), lambda b, g, q: (b, g, 0, q, 0)),
                  pl.BlockSpec((seq, LANES), lambda b, g, q: (b, col_k // LANES + g)),
                  pl.BlockSpec((seq, LANES), lambda b, g, q: (b, col_v // LANES + g))],
        out_specs=pl.BlockSpec((t, A_REP * A_HEAD_DIM), lambda b, g, q: (b * nq + q, g)),
        out_shape=jax.ShapeDtypeStruct((B * seq, A_WIDTH), BF16),
        scratch_shapes=[pltpu.VMEM((rows, LANES), F32), pltpu.VMEM((rows, LANES), F32)],
        compiler_params=_cparams(("arbitrary", "arbitrary", "arbitrary")),
        name="nsa_sel" if window is None else "nsa_win",
    )(qaug, slab, slab)


DILATIONS = ((128, 1), (512, 4), (2048, 16))


TQ_DIL = 512
TK_DIL = 256


def _dil_near_tiles(tq, tk, seq):
    w_near = max(w for w, r in DILATIONS if w < seq)
    return tq // tk + (w_near + tk - 1) // tk


def _dil_bias_table(tq, tk, seq):
    n_near = _dil_near_tiles(tq, tk, seq)
    w_far, r_far = DILATIONS[-1]
    assert w_far >= seq and tq % r_far == 0 and tk % r_far == 0
    delta = np.arange(tq)[:, None] - np.arange(tk)[None, :]
    tabs = []
    for e in range(n_near):
        dist = (e - (tq // tk - 1)) * tk + delta
        mult = sum(((dist >= 0) & (dist <= w) & (dist % r == 0)).astype(np.float64) for w, r in DILATIONS)
        tabs.append(np.where(mult > 0, np.log(np.maximum(mult, 1.0)), NEG))
    tabs.append(np.where(delta % r_far == 0, 0.0, NEG))
    return np.stack(tabs).astype(np.float32)


def _dil_kernel(q_ref, k_ref, v_ref, bias_ref, o_ref, m_scr, acc_scr, l_scr, *, tq, tk, n_near):
    qi = pl.program_id(2)
    q = q_ref[...]
    _flash_init(m_scr, acc_scr, l_scr)
    step = functools.partial(_flash_step, reps=1, m_scr=m_scr, acc_scr=acc_scr, l_scr=l_scr)
    tile = lambda kj: (k_ref[pl.ds(pl.multiple_of(kj * tk, tk), tk), :],
                       v_ref[pl.ds(pl.multiple_of(kj * tk, tk), tk), :])
    kj_last = (qi + 1) * (tq // tk) - 1

    def body(kj, carry):
        k, v = tile(kj)
        step(q, k, v, bias_ref[n_near])
        return carry
    lax.fori_loop(0, jnp.maximum(kj_last - n_near + 1, 0), body, 0)
    for e in range(n_near - 1, -1, -1):
        @pl.when(kj_last - e >= 0)
        def _(e=e):
            k, v = tile(kj_last - e)
            step(q, k, v, bias_ref[e])
    l = jnp.sum(l_scr[...], axis=-1, keepdims=True)
    o_ref[...] = (acc_scr[...] / l).astype(o_ref.dtype)


def _dil(slab, B, seq):
    tq, tk = TQ_DIL, TK_DIL
    nq = seq // tq
    H = B_HEADS
    n_near = _dil_near_tiles(tq, tk, seq)
    bias = jnp.asarray(_dil_bias_table(tq, tk, seq))
    return pl.pallas_call(
        functools.partial(_dil_kernel, tq=tq, tk=tk, n_near=n_near),
        grid=(B, H, nq),
        in_specs=[pl.BlockSpec((tq, LANES), lambda b, h, q: (b * nq + q, COL_QB // LANES + h)),
                  pl.BlockSpec((seq, LANES), lambda b, h, q: (b, COL_KB // LANES + h)),
                  pl.BlockSpec((seq, LANES), lambda b, h, q: (b, COL_VB // LANES + h)),
                  pl.BlockSpec(bias.shape, lambda b, h, q: (0, 0, 0))],
        out_specs=pl.BlockSpec((tq, LANES), lambda b, h, q: (b * nq + q, h)),
        out_shape=jax.ShapeDtypeStruct((B * seq, B_WIDTH), BF16),
        scratch_shapes=[pltpu.VMEM((tq, LANES), F32)] * 3,
        compiler_params=_cparams(("arbitrary", "arbitrary", "arbitrary")),
        name="dilated",
    )(slab, slab, slab, bias)


def _gate_expand_table():
    e = np.zeros((3, A_KV_GROUPS * LANES, A_WIDTH), np.float32)
    for g in range(A_KV_GROUPS):
        for r in range(A_REP):
            h = g * A_REP + r
            for br in range(3):
                e[br, g * LANES + GATE_LANE + r * 3 + br, h * A_HEAD_DIM:(h + 1) * A_HEAD_DIM] = 1.0
    return e


def _silu(z):
    return z * jax.nn.sigmoid(z)


def _out_kernel(ocmp_ref, osel_ref, owin_ref, ks_ref, za_ref, ob_ref, zb_ref, ma_ref, mb_ref,
                x_ref, gate_ref, gpost_ref, e_ref, wa_ref, wb_ref, wo_ref, o_ref):
    sg = jax.nn.sigmoid(ks_ref[...].astype(F32)).astype(BF16)
    oa = None
    for br, ref in enumerate((ocmp_ref, osel_ref, owin_ref)):
        gexp = jnp.dot(sg, e_ref[br], preferred_element_type=F32)
        term = gexp * ref[...].astype(F32)
        oa = term if oa is None else oa + term
    a_in = (oa * _silu(za_ref[...].astype(F32))).astype(BF16)
    ya = jnp.dot(a_in, wa_ref[...], preferred_element_type=F32)
    b_in = (ob_ref[...].astype(F32) * _silu(zb_ref[...].astype(F32))).astype(BF16)
    yb = jnp.dot(b_in, wb_ref[...], preferred_element_type=F32)
    merged = (jax.nn.sigmoid(ma_ref[...].astype(F32)) * ya
              + jax.nn.sigmoid(mb_ref[...].astype(F32)) * yb)
    out = jnp.dot(merged.astype(BF16), wo_ref[...], preferred_element_type=F32)
    y = out * lax.rsqrt(jnp.mean(out * out, axis=-1, keepdims=True) + EPS) * gpost_ref[...]
    o_ref[...] = x_ref[...] + gate_ref[0] * y


def _out(ocmp, osel, owin, ob, slab, x2d, ada3, g_post, wa, wb, wo, seq):
    M, D = x2d.shape
    tm = 256
    per_b = seq // tm
    e = jnp.asarray(_gate_expand_table(), BF16)
    const = lambda a: pl.BlockSpec(a.shape, lambda i: (0,) * a.ndim, pipeline_mode=pl.Buffered(1))
    row = lambda w, cb: pl.BlockSpec((tm, w), lambda i: (i, cb))
    return pl.pallas_call(
        _out_kernel,
        grid=(M // tm,),
        in_specs=[row(A_WIDTH, 0), row(A_WIDTH, 0), row(A_WIDTH, 0),
                  row(A_KV_GROUPS * LANES, COL_KS // (A_KV_GROUPS * LANES)),
                  row(A_WIDTH, COL_ZA // A_WIDTH), row(B_WIDTH, 0), row(B_WIDTH, COL_ZB // B_WIDTH),
                  row(D_MODEL, COL_MA // D_MODEL), row(D_MODEL, COL_MB // D_MODEL),
                  row(D, 0),
                  pl.BlockSpec((1, 1, D), lambda i: (i // per_b, 0, 2)),
                  pl.BlockSpec((1, D), lambda i: (0, 0)),
                  const(e), const(wa), const(wb), const(wo)],
        out_specs=pl.BlockSpec((tm, D), lambda i: (i, 0)),
        out_shape=jax.ShapeDtypeStruct((M, D), F32),
        compiler_params=_cparams(("arbitrary",)),
        name="out",
    )(ocmp, osel, owin, slab, slab, ob, slab, slab, slab, x2d, ada3, g_post.reshape(1, D), e, wa, wb, wo)


def _layer(x, c, positions, w_ada, b_ada, g_pre, g_post, w_in, pe_ck, pe_cv, w_ck1, w_ck2,
           w_cv1, w_cv2, w_br_a, w_br_b, w_out):
    B, S, D = x.shape
    G = A_KV_GROUPS
    x2d = x.reshape(B * S, D)
    ada3 = _ada(c, w_ada, b_ada).reshape(B, 1, 3 * D)
    slab = _proj(x2d, ada3, g_pre, positions.reshape(B * S, 1), _build_w_slab(w_in), S)

    kcv = slab[:, COL_KCV:COL_KCV + 2 * A_KV_WIDTH]
    t5 = kcv.reshape(B, S // CMP_STRIDE, CMP_STRIDE, G, 2, A_HEAD_DIM).transpose(0, 3, 4, 1, 2, 5)
    t5 = t5.reshape(B, G, 2, S // CMP_STRIDE, CMP_STRIDE * A_HEAD_DIM)
    n_c = (S - CMP_BLOCK) // CMP_STRIDE + 1
    cmp_end = np.minimum(np.arange(S // CMP_STRIDE) * CMP_STRIDE + CMP_BLOCK - 1, S - 1)
    posc = positions[:, cmp_end].reshape(B, S // CMP_STRIDE, 1)
    assert n_c == S // CMP_STRIDE - 1
    dup = lambda w: jnp.concatenate([w, w], axis=1).astype(BF16)
    pe8 = lambda pe: jnp.broadcast_to(pe.reshape(1, -1), (8, pe.size)).astype(BF16)
    kc2, vc2 = _cmp_kv(t5, posc, w_ck1.astype(BF16), dup(w_ck2), w_cv1.astype(BF16), dup(w_cv2),
                       pe8(pe_ck), pe8(pe_cv))

    ocmp, qaug = _cmp_attn(slab, kc2, vc2, B, S)
    osel = _nsa_flash(qaug, slab, COL_KS, COL_VS, None, B, S)
    owin = _nsa_flash(qaug, slab, COL_KW, COL_VW, WIN_SIZE, B, S)
    ob = _dil(slab, B, S)
    out = _out(ocmp, osel, owin, ob, slab, x2d, ada3, g_post,
               w_br_a.astype(BF16), w_br_b.astype(BF16), w_out.astype(BF16), S)
    return out.reshape(B, S, D)


def kernel(x, c, positions, w_ada, b_ada, g_pre, g_post, w_in, pe_ck, pe_cv, w_ck1, w_ck2, w_cv1, w_cv2, w_br_a, w_br_b, w_out):
    h = x
    for layer in range(w_ada.shape[0]):
        h = _layer(h, c, positions, w_ada[layer], b_ada[layer], g_pre[layer], g_post[layer],
                   w_in[layer], pe_ck[layer], pe_cv[layer], w_ck1[layer], w_ck2[layer],
                   w_cv1[layer], w_cv2[layer], w_br_a[layer], w_br_b[layer], w_out[layer])
    return h
```

```python
import functools

import numpy as np
import jax
import jax.numpy as jnp
from jax import lax
from jax.experimental import pallas as pl
from jax.experimental.pallas import tpu as pltpu

F32 = jnp.float32
BF16 = jnp.bfloat16

D_MODEL = 2048
A_HEADS = 16
A_HEAD_DIM = 64
A_KV_GROUPS = 4
A_REP = A_HEADS // A_KV_GROUPS
A_WIDTH = A_HEADS * A_HEAD_DIM
A_KV_WIDTH = A_KV_GROUPS * A_HEAD_DIM
CMP_BLOCK = 32
CMP_STRIDE = 16
CMP_HIDDEN = 4 * A_HEAD_DIM
SEL_BLOCK = 64
SEL_TOPK = 16
WIN_SIZE = 512
FORCE_BONUS = 1.0e4
B_HEADS = 8
B_HEAD_DIM = 128
B_WIDTH = B_HEADS * B_HEAD_DIM
ROPE_THETA = 500000.0
EPS = 1e-6
NEG = -1e30
M_INIT = -1.0e38
LOG2E = 1.4426950408889634

LANES = 128
VMEM_LIMIT = 56 * 1024 * 1024

TN = 512
COL_QA = 0
COL_KS = 1024
COL_KW = 1536
COL_QB = 2048
COL_KB = 3072
COL_VB = 4096
COL_ZA = 5120
COL_ZB = 6144
COL_VS = 7168
COL_VW = 7680
COL_MA = 8192
COL_MB = 10240
N_SLAB = 12288
COL_KCV = N_SLAB
N_PROJ = N_SLAB + 2 * A_KV_WIDTH
GATE_LANE = 96
AUG_LANE = 64
ONES_LANE = 64

T_ATT = 256


def _cparams(sem):
    return pltpu.CompilerParams(dimension_semantics=sem, vmem_limit_bytes=VMEM_LIMIT)


def _ada_kernel(c_ref, w_ref, b_ref, o_ref):
    o_ref[...] = jnp.dot(c_ref[...], w_ref[...], preferred_element_type=F32,
                         precision=lax.Precision.HIGHEST) + b_ref[...]


def _ada(c, w_ada, b_ada):
    B, D = c.shape
    N = w_ada.shape[1]
    tn = 768
    return pl.pallas_call(
        _ada_kernel,
        grid=(N // tn,),
        in_specs=[pl.BlockSpec((B, D), lambda j: (0, 0)),
                  pl.BlockSpec((D, tn), lambda j: (0, j)),
                  pl.BlockSpec((1, tn), lambda j: (0, j))],
        out_specs=pl.BlockSpec((B, tn), lambda j: (0, j)),
        out_shape=jax.ShapeDtypeStruct((B, N), F32),
        compiler_params=_cparams(("arbitrary",)),
        name="ada",
    )(c, w_ada, b_ada.reshape(1, N))


def _rope_inv_tables():
    lane = np.arange(LANES)
    inv_a8 = (ROPE_THETA ** (-np.arange(0, 16, 2) / 16)).astype(np.float32)
    inv_b16 = (ROPE_THETA ** (-np.arange(0, 32, 2) / 32)).astype(np.float32)
    la = lane % A_HEAD_DIM
    inv_a = np.where(la < 16, inv_a8[la % 8], 0.0).astype(np.float32)
    inv_b = np.where(lane < 32, inv_b16[lane % 16], 0.0).astype(np.float32)
    return inv_a.reshape(1, LANES), inv_b.reshape(1, LANES)


def _rope_tables(pos_f, inv, period, half):
    ang = pos_f * inv
    c, s = jnp.cos(ang), jnp.sin(ang)
    lane = lax.broadcasted_iota(jnp.int32, ang.shape, 1) % period
    cos_t = jnp.where(lane < 2 * half, c, 1.0)
    sin_m = jnp.where(lane < half, -s, 0.0)
    sin_p = jnp.where((lane >= half) & (lane < 2 * half), s, 0.0)
    return cos_t, sin_m, sin_p


def _rope_apply(x, cos_t, sin_m, sin_p, half):
    return (x * cos_t + pltpu.roll(x, LANES - half, 1) * sin_m
            + pltpu.roll(x, half, 1) * sin_p)


def _proj_kernel(x_ref, shift_ref, scale_ref, gpre_ref, pos_ref, inva_ref, invb_ref, w_ref,
                 o_ref, kcv_ref, h_scr, ca, sma, spa, cb, smb, spb, *, tm, seq):
    i = pl.program_id(0)
    j = pl.program_id(1)

    @pl.when(j == 0)
    def _():
        x = x_ref[...]
        ms = jnp.mean(x * x, axis=-1, keepdims=True)
        y = x * lax.rsqrt(ms + EPS) * gpre_ref[...]
        h = y * (1.0 + scale_ref[0]) + shift_ref[0]
        h_scr[...] = h.astype(BF16)
        pos_f = pos_ref[...].astype(F32)
        c, sm, sp = _rope_tables(pos_f, inva_ref[...], A_HEAD_DIM, 8)
        ca[...] = c
        sma[...] = sm
        spa[...] = sp
        c, sm, sp = _rope_tables(pos_f, invb_ref[...], LANES, 16)
        cb[...] = c
        smb[...] = sm
        spb[...] = sp

    acc = jnp.dot(h_scr[...], w_ref[...], preferred_element_type=F32)
    nchunk = TN // LANES

    def store_rope(tabs, half, mul, first_head_only, onehot):
        cos_t, sin_m, sin_p = tabs[0][...], tabs[1][...], tabs[2][...]
        lane = lax.broadcasted_iota(jnp.int32, (tm, LANES), 1)
        if first_head_only:
            keep = lane < A_HEAD_DIM
            cos_t = jnp.where(keep, cos_t, 1.0)
            sin_m = jnp.where(keep, sin_m, 0.0)
            sin_p = jnp.where(keep, sin_p, 0.0)
        if onehot:
            t = (i * tm + lax.broadcasted_iota(jnp.int32, (tm, LANES), 0)) % seq
            hot = (lane >= AUG_LANE) & (lane < AUG_LANE + seq // SEL_BLOCK) & \
                  ((t // SEL_BLOCK) == (lane - AUG_LANE))
        for cidx in range(nchunk):
            xc = acc[:, cidx * LANES:(cidx + 1) * LANES]
            r = _rope_apply(xc, cos_t, sin_m, sin_p, half)
            if mul != 1.0:
                r = r * mul
            if onehot:
                r = jnp.where(hot, 1.0, r)
            o_ref[:, cidx * LANES:(cidx + 1) * LANES] = r.astype(o_ref.dtype)

    ta = (ca, sma, spa)
    tb = (cb, smb, spb)

    @pl.when(j < COL_KS // TN)
    def _():
        store_rope(ta, 8, A_HEAD_DIM ** -0.5 * LOG2E, False, False)

    @pl.when(j == COL_KS // TN)
    def _():
        store_rope(ta, 8, 1.0, True, True)

    @pl.when(j == COL_KW // TN)
    def _():
        store_rope(ta, 8, 1.0, True, False)

    @pl.when((j >= COL_QB // TN) & (j < COL_KB // TN))
    def _():
        store_rope(tb, 16, B_HEAD_DIM ** -0.5 * LOG2E, False, False)

    @pl.when((j >= COL_KB // TN) & (j < COL_VB // TN))
    def _():
        store_rope(tb, 16, 1.0, False, False)

    is_vpad = (j == COL_VS // TN) | (j == COL_VW // TN)

    @pl.when((j >= COL_VB // TN) & (j < COL_KCV // TN) & jnp.logical_not(is_vpad))
    def _():
        o_ref[...] = acc.astype(o_ref.dtype)

    @pl.when(is_vpad)
    def _():
        lane = lax.broadcasted_iota(jnp.int32, acc.shape, 1) % LANES
        o_ref[...] = jnp.where(lane == ONES_LANE, 1.0, acc).astype(o_ref.dtype)

    @pl.when(j == COL_KCV // TN)
    def _():
        kcv_ref[...] = acc


def _proj(x2d, ada3, g_pre, pos2d, w_slab, seq):
    M, D = x2d.shape
    tm = 1024
    per_b = seq // tm
    inv_a, inv_b = _rope_inv_tables()
    tab = pltpu.VMEM((tm, LANES), F32)
    last_slab_tile = N_SLAB // TN - 1
    return pl.pallas_call(
        functools.partial(_proj_kernel, tm=tm, seq=seq),
        grid=(M // tm, N_PROJ // TN),
        in_specs=[pl.BlockSpec((tm, D), lambda i, j: (i, 0)),
                  pl.BlockSpec((1, 1, D), lambda i, j: (i // per_b, 0, 0)),
                  pl.BlockSpec((1, 1, D), lambda i, j: (i // per_b, 0, 1)),
                  pl.BlockSpec((1, D), lambda i, j: (0, 0)),
                  pl.BlockSpec((tm, 1), lambda i, j: (i, 0)),
                  pl.BlockSpec((1, LANES), lambda i, j: (0, 0)),
                  pl.BlockSpec((1, LANES), lambda i, j: (0, 0)),
                  pl.BlockSpec((D, TN), lambda i, j: (0, j))],
        out_specs=[pl.BlockSpec((tm, TN), lambda i, j: (i, jnp.minimum(j, last_slab_tile))),
                   pl.BlockSpec((tm, TN), lambda i, j: (i, 0))],
        out_shape=[jax.ShapeDtypeStruct((M, N_SLAB), BF16),
                   jax.ShapeDtypeStruct((M, N_PROJ - N_SLAB), F32)],
        scratch_shapes=[pltpu.VMEM((tm, D), BF16), tab, tab, tab, tab, tab, tab],
        compiler_params=_cparams(("arbitrary", "arbitrary")),
        name="proj",
    )(x2d, ada3, ada3, g_pre.reshape(1, D), pos2d, jnp.asarray(inv_a), jnp.asarray(inv_b), w_slab)


def _build_w_slab(w_in):
    D = w_in.shape[0]
    sizes = (A_WIDTH,) + (A_KV_WIDTH,) * 6 + (A_HEADS * 3, A_WIDTH) + (B_WIDTH,) * 4 + (D_MODEL, D_MODEL)
    offs = np.concatenate([[0], np.cumsum(sizes)])
    seg = lambda k: w_in[:, int(offs[k]):int(offs[k + 1])]
    qa, kc, vc, ks, vs, kw, vw, ga, za, qb, kb, vb, zb, ma, mb = [seg(k) for k in range(15)]
    z = lambda n: jnp.zeros((D, n), w_in.dtype)
    grp = lambda t, g: t[:, g * A_HEAD_DIM:(g + 1) * A_HEAD_DIM]
    ngate = A_REP * 3
    ks_aug, kw_pad, vs_pad, vw_pad, kcv = [], [], [], [], []
    for g in range(A_KV_GROUPS):
        ks_aug += [grp(ks, g), z(GATE_LANE - A_HEAD_DIM), ga[:, g * ngate:(g + 1) * ngate],
                   z(LANES - GATE_LANE - ngate)]
        kw_pad += [grp(kw, g), z(LANES - A_HEAD_DIM)]
        vs_pad += [grp(vs, g), z(LANES - A_HEAD_DIM)]
        vw_pad += [grp(vw, g), z(LANES - A_HEAD_DIM)]
        kcv += [grp(kc, g), grp(vc, g)]
    slab = jnp.concatenate([qa] + ks_aug + kw_pad + [qb, kb, vb, za, zb] + vs_pad + vw_pad
                           + [ma, mb] + kcv, axis=1)
    assert slab.shape[1] == N_PROJ
    return slab.astype(BF16)


def _cmp_kv_kernel(t_ref, pe_ref, posc_ref, inva_ref, w1_ref, wk2_ref, wv2_ref, kc2_ref, vc2_ref):
    nblk = t_ref.shape[0] // CMP_STRIDE
    hidden2 = w1_ref.shape[-1]
    p_lo = jnp.zeros((nblk, hidden2), F32)
    p_hi = jnp.zeros((nblk, hidden2), F32)
    for l in range(CMP_STRIDE):
        x = t_ref[pl.ds(l, nblk, stride=CMP_STRIDE), :]
        p_lo += jnp.dot((x + pe_ref[l:l + 1, :]).astype(BF16), w1_ref[l],
                        preferred_element_type=F32)
        p_hi += jnp.dot((x + pe_ref[CMP_STRIDE + l:CMP_STRIDE + l + 1, :]).astype(BF16),
                        w1_ref[CMP_STRIDE + l], preferred_element_type=F32)
    hid = p_lo + pltpu.roll(p_hi, nblk - 1, 0)
    act = jax.nn.gelu(hid).astype(BF16)
    kk = jnp.dot(act[:, :CMP_HIDDEN], wk2_ref[...], preferred_element_type=F32)
    vv = jnp.dot(act[:, CMP_HIDDEN:], wv2_ref[...], preferred_element_type=F32)
    pos_f = posc_ref[...].astype(F32)
    cos_t, sin_m, sin_p = _rope_tables(pos_f, inva_ref[...], A_HEAD_DIM, 8)
    kk = _rope_apply(kk, cos_t, sin_m, sin_p, 8)
    lane = lax.broadcasted_iota(jnp.int32, kk.shape, 1)
    lo = lane < A_HEAD_DIM
    kc2_ref[0:nblk, :] = jnp.where(lo, kk, 0.0).astype(kc2_ref.dtype)
    kc2_ref[nblk:2 * nblk, :] = jnp.where(lo, 0.0, kk).astype(kc2_ref.dtype)
    vc2_ref[0:nblk, :] = jnp.where(lo, vv, 0.0).astype(vc2_ref.dtype)
    vc2_ref[nblk:2 * nblk, :] = jnp.where(lo, 0.0, vv).astype(vc2_ref.dtype)


def _cmp_kv(kcv, pe2, posc, w1, wk2d, wv2d, B, seq):
    G = A_KV_GROUPS
    nblk = seq // CMP_STRIDE
    inv_a, _ = _rope_inv_tables()
    full = lambda a: pl.BlockSpec(a.shape, lambda b, g: (0,) * a.ndim)
    out = jax.ShapeDtypeStruct((B, G, 2 * nblk, LANES), BF16)
    ospec = pl.BlockSpec((None, None, 2 * nblk, LANES), lambda b, g: (b, g, 0, 0))
    return pl.pallas_call(
        _cmp_kv_kernel,
        grid=(B, G),
        in_specs=[pl.BlockSpec((seq, LANES), lambda b, g: (b, g)),
                  full(pe2),
                  pl.BlockSpec((None, nblk, 1), lambda b, g: (b, 0, 0)),
                  pl.BlockSpec((1, LANES), lambda b, g: (0, 0)),
                  full(w1), full(wk2d), full(wv2d)],
        out_specs=[ospec, ospec],
        out_shape=[out, out],
        compiler_params=_cparams(("arbitrary", "arbitrary")),
        name="cmp_kv",
    )(kcv, pe2, posc, jnp.asarray(inv_a), w1, wk2d, wv2d)


def _build_cmp_w1(w_ck1, w_cv1):
    wk = w_ck1.reshape(CMP_BLOCK, A_HEAD_DIM, CMP_HIDDEN)
    wv = w_cv1.reshape(CMP_BLOCK, A_HEAD_DIM, CMP_HIDDEN)
    z = jnp.zeros_like(wk)
    return jnp.concatenate([jnp.concatenate([wk, z], axis=2),
                            jnp.concatenate([z, wv], axis=2)], axis=1).astype(BF16)


def _overlap_table_t(seq):
    n_c = (seq - CMP_BLOCK) // CMP_STRIDE + 1
    n_s = seq // SEL_BLOCK
    cs = np.arange(n_c) * CMP_STRIDE
    ss = np.arange(n_s) * SEL_BLOCK
    ov = np.clip(np.minimum(cs[:, None] + CMP_BLOCK, ss[None, :] + SEL_BLOCK)
                 - np.maximum(cs[:, None], ss[None, :]), 0, None).astype(np.float32) / CMP_BLOCK
    full = np.zeros((LANES, LANES), np.float32)
    full[AUG_LANE:AUG_LANE + n_s, :n_c] = ov.T
    return full


def _split3(x):
    hi = x.astype(BF16)
    r1 = x - hi.astype(F32)
    mid = r1.astype(BF16)
    lo = (r1 - mid.astype(F32)).astype(BF16)
    return hi, mid, lo


def _cmp_attn_kernel(q_ref, kc2_ref, vc2_ref, ovt_ref, ocmp_ref, qaug_ref, *, tq, n_sel_blocks):
    qi = pl.program_id(2)
    t = qi * tq + lax.broadcasted_iota(jnp.int32, (tq, LANES), 0)
    lane = lax.broadcasted_iota(jnp.int32, (tq, LANES), 1)
    cmask = (CMP_STRIDE * lane + (CMP_BLOCK - 1)) <= t
    cmask_f = cmask.astype(F32)
    nt = (((1,), (1,)), ((), ()))
    psum = jnp.zeros((tq, LANES), F32)
    for pair in range(A_REP // 2):
        qp = q_ref[:, pair * LANES:(pair + 1) * LANES]
        s2 = lax.dot_general(qp, kc2_ref[...], nt, preferred_element_type=F32)
        probs = []
        for hh in range(2):
            s = jnp.where(cmask, s2[:, hh * LANES:(hh + 1) * LANES], NEG)
            mx = jnp.max(s, axis=-1, keepdims=True)
            e = jnp.exp2(s - mx) * cmask_f
            pr = e / jnp.maximum(jnp.sum(e, axis=-1, keepdims=True), 1.0)
            probs.append(pr)
            psum = psum + pr
        p2 = jnp.concatenate(probs, axis=1).astype(BF16)
        o_pair = jnp.dot(p2, vc2_ref[...], preferred_element_type=F32)
        ocmp_ref[:, pair * LANES:(pair + 1) * LANES] = o_pair.astype(ocmp_ref.dtype)

    ovt = ovt_ref[...]
    imp_t = sum(lax.dot_general(ovt, part, nt, preferred_element_type=F32) for part in _split3(psum))
    imp_t = imp_t[AUG_LANE:AUG_LANE + n_sel_blocks, :]
    jb = lax.broadcasted_iota(jnp.int32, imp_t.shape, 0)
    tcol = qi * tq + lax.broadcasted_iota(jnp.int32, imp_t.shape, 1)
    cur = tcol // SEL_BLOCK
    valid = (SEL_BLOCK * jb) <= tcol
    forced = valid & ((jb == 0) | (jb == cur) | (jb == cur - 1))
    score = jnp.where(valid, imp_t + FORCE_BONUS * forced.astype(F32), NEG)
    rank = jnp.zeros(imp_t.shape, F32)
    for i in range(n_sel_blocks):
        other = score[i:i + 1, :]
        beats = (other > score) | ((other == score) & (jb > i))
        rank = rank + beats.astype(F32)
    selb = jnp.where(rank < float(min(SEL_TOPK, n_sel_blocks)), 0.0, NEG)
    aug_t = jnp.concatenate([jnp.zeros((AUG_LANE, tq), F32), selb,
                             jnp.zeros((LANES - AUG_LANE - n_sel_blocks, tq), F32)], axis=0)
    aug = aug_t.T.astype(BF16)
    lo = lane < A_HEAD_DIM
    for r in range(A_REP):
        qp = q_ref[:, (r // 2) * LANES:(r // 2 + 1) * LANES]
        if r % 2:
            qp = jnp.concatenate([qp[:, A_HEAD_DIM:], qp[:, :A_HEAD_DIM]], axis=1)
        qaug_ref[r] = jnp.where(lo, qp, aug)


def _cmp_attn(slab, kc2, vc2, B, seq):
    tq = T_ATT
    nq = seq // tq
    G = A_KV_GROUPS
    n_s = seq // SEL_BLOCK
    ovt = jnp.asarray(_overlap_table_t(seq), BF16)
    kv_spec = pl.BlockSpec((None, None) + kc2.shape[2:], lambda b, g, q: (b, g, 0, 0))
    return pl.pallas_call(
        functools.partial(_cmp_attn_kernel, tq=tq, n_sel_blocks=n_s),
        grid=(B, G, nq),
        in_specs=[pl.BlockSpec((tq, A_REP * A_HEAD_DIM), lambda b, g, q: (b * nq + q, g)),
                  kv_spec, kv_spec,
                  pl.BlockSpec((LANES, LANES), lambda b, g, q: (0, 0))],
        out_specs=[pl.BlockSpec((tq, A_REP * A_HEAD_DIM), lambda b, g, q: (b * nq + q, g)),
                   pl.BlockSpec((None, None, A_REP, tq, LANES), lambda b, g, q: (b, g, 0, q, 0))],
        out_shape=[jax.ShapeDtypeStruct((B * seq, A_WIDTH), BF16),
                   jax.ShapeDtypeStruct((B, G, A_REP, seq, LANES), BF16)],
        compiler_params=_cparams(("arbitrary", "arbitrary", "arbitrary")),
        name="cmp_attn",
    )(slab, kc2, vc2, ovt)


def _flash_init(m_scr, acc_scr, l_scr=None):
    m_scr[...] = jnp.full(m_scr.shape, M_INIT, F32)
    acc_scr[...] = jnp.zeros(acc_scr.shape, F32)
    if l_scr is not None:
        l_scr[...] = jnp.zeros(l_scr.shape, F32)


def _scores(q, k):
    return lax.dot_general(q, k, (((1,), (1,)), ((), ())), preferred_element_type=F32)


def _add_bias(s, bias, reps):
    tq, tk = bias.shape
    return (s.reshape(reps, tq, tk) + bias[None]).reshape(reps * tq, tk)


def _flash_update(s, v, m_scr, acc_scr, l_scr=None):
    chunks = [s[:, c * LANES:(c + 1) * LANES] for c in range(s.shape[1] // LANES)]
    mx = functools.reduce(jnp.maximum, chunks)
    m_prev = m_scr[...]
    m_new = jnp.maximum(m_prev, jnp.max(mx, axis=-1, keepdims=True))
    alpha = jnp.exp2(m_prev - m_new)
    ps = [jnp.exp2(ch - m_new) for ch in chunks]
    p = jnp.concatenate(ps, axis=1).astype(v.dtype)
    acc_scr[...] = alpha * acc_scr[...] + jnp.dot(p, v, preferred_element_type=F32)
    if l_scr is not None:
        l_scr[...] = alpha * l_scr[...] + functools.reduce(jnp.add, ps)
    m_scr[...] = m_new


def _store_head_pairs(o_ref, acc, t):
    lane = lax.broadcasted_iota(jnp.int32, acc.shape, 1)
    o = jnp.where(lane < A_HEAD_DIM, acc / acc[:, ONES_LANE:ONES_LANE + 1], 0.0)
    for pair in range(A_REP // 2):
        a = o[(2 * pair) * t:(2 * pair + 1) * t]
        b = o[(2 * pair + 1) * t:(2 * pair + 2) * t]
        o_ref[:, pair * LANES:(pair + 1) * LANES] = (a + pltpu.roll(b, A_HEAD_DIM, 1)).astype(o_ref.dtype)


def _nsa_sel_kernel(q_ref, k_ref, v_ref, o_ref, m_scr, acc_scr, s0_scr, s1_scr, *, t):
    qi = pl.program_id(2)
    q = q_ref[...].reshape(A_REP * t, LANES)
    _flash_init(m_scr, acc_scr)
    rows_of = lambda ref, kj: ref[pl.ds(pl.multiple_of(kj * t, t), t), :]

    def score_into(buf, kj):
        buf[...] = _scores(q, rows_of(k_ref, kj))

    def fold(buf, kj, causal=False):
        s = buf[...]
        if causal:
            delta = (lax.broadcasted_iota(jnp.int32, (t, t), 0)
                     - lax.broadcasted_iota(jnp.int32, (t, t), 1))
            s = _add_bias(s, jnp.where(delta >= 0, 0.0, NEG), A_REP)
        _flash_update(s, rows_of(v_ref, kj), m_scr, acc_scr)

    score_into(s0_scr, 0)

    def body(i, carry):
        kj = 2 * i
        score_into(s1_scr, kj + 1)
        fold(s0_scr, kj)
        score_into(s0_scr, kj + 2)
        fold(s1_scr, kj + 1)
        return carry
    lax.fori_loop(0, qi // 2, body, 0)

    @pl.when(qi % 2 == 1)
    def _():
        score_into(s1_scr, qi)
        fold(s0_scr, qi - 1)
        fold(s1_scr, qi, causal=True)

    @pl.when(qi % 2 == 0)
    def _():
        fold(s0_scr, qi, causal=True)

    _store_head_pairs(o_ref, acc_scr[...], t)


def _nsa_win_kernel(q_ref, k_ref, v_ref, o_ref, *, t, span, window):
    qi = pl.program_id(2)
    start = pl.multiple_of(jnp.maximum((qi + 1) * t - span, 0), t)
    k = k_ref[pl.ds(start, span), :]
    v = v_ref[pl.ds(start, span), :]
    dist = (qi * t - start) + (lax.broadcasted_iota(jnp.int32, (t, span), 0)
                               - lax.broadcasted_iota(jnp.int32, (t, span), 1))
    bias = jnp.where((dist >= 0) & (dist < window), 0.0, NEG)
    lane = lax.broadcasted_iota(jnp.int32, (t, LANES), 1)
    for pair in range(A_REP // 2):
        halves = []
        for r in (2 * pair, 2 * pair + 1):
            s = _scores(q_ref[r], k) + bias
            chunks = [s[:, c * LANES:(c + 1) * LANES] for c in range(span // LANES)]
            m = jnp.max(functools.reduce(jnp.maximum, chunks), axis=-1, keepdims=True)
            p = jnp.concatenate([jnp.exp2(ch - m) for ch in chunks], axis=1).astype(BF16)
            acc = jnp.dot(p, v, preferred_element_type=F32)
            halves.append(jnp.where(lane < A_HEAD_DIM, acc / acc[:, ONES_LANE:ONES_LANE + 1], 0.0))
        o_ref[:, pair * LANES:(pair + 1) * LANES] = (
            halves[0] + pltpu.roll(halves[1], A_HEAD_DIM, 1)).astype(o_ref.dtype)


def _nsa_flash(qaug, slab, col_k, col_v, window, B, seq):
    t = T_ATT
    nq = seq // t
    G = A_KV_GROUPS
    rows = A_REP * t
    if window is None:
        body = functools.partial(_nsa_sel_kernel, t=t)
        scratch = [pltpu.VMEM((rows, LANES), F32), pltpu.VMEM((rows, LANES), F32),
                   pltpu.VMEM((rows, t), F32), pltpu.VMEM((rows, t), F32)]
    else:
        span = ((window - 1 + t - 1) // t + 1) * t
        body = functools.partial(_nsa_win_kernel, t=t, span=min(span, seq), window=window)
        scratch = []
    return pl.pallas_call(
        body,
        grid=(B, G, nq),
        in_specs=[pl.BlockSpec((None, None, A_REP, t, LANES), lambda b, g, q: (b, g, 0, q, 0)),
                  pl.BlockSpec((seq, LANES), lambda b, g, q: (b, col_k // LANES + g)),
                  pl.BlockSpec((seq, LANES), lambda b, g, q: (b, col_v // LANES + g))],
        out_specs=pl.BlockSpec((t, A_REP * A_HEAD_DIM), lambda b, g, q: (b * nq + q, g)),
        out_shape=jax.ShapeDtypeStruct((B * seq, A_WIDTH), BF16),
        scratch_shapes=scratch,
        compiler_params=_cparams(("arbitrary", "arbitrary", "arbitrary")),
        name="nsa_sel" if window is None else "nsa_win",
    )(qaug, slab, slab)


DILATIONS = ((128, 1), (512, 4), (2048, 16))
TQ_DIL = 512
TK_DIL = 256


def _dil_bias_tables(tq, tk, seq):
    w_near = max(w for w, r in DILATIONS if w < seq)
    near_keys = tq + -(-w_near // tk) * tk
    w_far, r_far = DILATIONS[-1]
    assert w_far >= seq and tq % r_far == 0 and tk % r_far == 0 and near_keys == 2 * tq
    assert tq % (2 * tk) == 0
    delta = np.arange(tq)[:, None] - np.arange(near_keys)[None, :]
    near = []
    for base in (near_keys - tq, 0):
        dist = base + delta
        mult = sum(((dist >= 0) & (dist <= w) & (dist % r == 0)).astype(np.float64) for w, r in DILATIONS)
        near.append(np.where(mult > 0, np.log2(np.maximum(mult, 1.0)), NEG))
    far = np.where(delta[:, :tk] % r_far == 0, 0.0, NEG)
    return np.stack(near).astype(np.float32), far.astype(np.float32)


def _dil_kernel(q_ref, k_ref, v_ref, near_ref, far_ref, o_ref, m_scr, acc_scr, l_scr, s0_scr, s1_scr,
                *, tq, tk):
    qi = pl.program_id(2)
    q = q_ref[...]
    near_keys = near_ref.shape[-1]
    _flash_init(m_scr, acc_scr, l_scr)
    update = functools.partial(_flash_update, m_scr=m_scr, acc_scr=acc_scr, l_scr=l_scr)
    rows_of = lambda ref, kj: ref[pl.ds(pl.multiple_of(kj * tk, tk), tk), :]
    clamped = (qi + 1) * tq < near_keys
    start = pl.multiple_of(jnp.maximum((qi + 1) * tq - near_keys, 0), tk)
    n_far = start // tk

    def score_into(buf, kj):
        buf[...] = _scores(q, rows_of(k_ref, kj))

    def fold(buf, kj):
        update(buf[...] + far_ref[...], rows_of(v_ref, kj))

    @pl.when(n_far > 0)
    def _():
        score_into(s0_scr, 0)

        def body(i, carry):
            kj = 2 * i
            score_into(s1_scr, kj + 1)
            fold(s0_scr, kj)
            score_into(s0_scr, kj + 2)
            fold(s1_scr, kj + 1)
            return carry
        lax.fori_loop(0, n_far // 2, body, 0)

    s = _scores(q, k_ref[pl.ds(start, near_keys), :]) + near_ref[clamped.astype(jnp.int32)]
    update(s, v_ref[pl.ds(start, near_keys), :])
    l = jnp.sum(l_scr[...], axis=-1, keepdims=True)
    o_ref[...] = (acc_scr[...] / l).astype(o_ref.dtype)


def _dil(slab, B, seq):
    tq, tk = TQ_DIL, TK_DIL
    nq = seq // tq
    H = B_HEADS
    near, far = (jnp.asarray(a) for a in _dil_bias_tables(tq, tk, seq))
    return pl.pallas_call(
        functools.partial(_dil_kernel, tq=tq, tk=tk),
        grid=(B, H, nq),
        in_specs=[pl.BlockSpec((tq, LANES), lambda b, h, q: (b * nq + q, COL_QB // LANES + h)),
                  pl.BlockSpec((seq, LANES), lambda b, h, q: (b, COL_KB // LANES + h)),
                  pl.BlockSpec((seq, LANES), lambda b, h, q: (b, COL_VB // LANES + h)),
                  pl.BlockSpec(near.shape, lambda b, h, q: (0, 0, 0)),
                  pl.BlockSpec(far.shape, lambda b, h, q: (0, 0))],
        out_specs=pl.BlockSpec((tq, LANES), lambda b, h, q: (b * nq + q, h)),
        out_shape=jax.ShapeDtypeStruct((B * seq, B_WIDTH), BF16),
        scratch_shapes=[pltpu.VMEM((tq, LANES), F32)] * 3 + [pltpu.VMEM((tq, tk), F32)] * 2,
        compiler_params=_cparams(("arbitrary", "arbitrary", "arbitrary")),
        name="dilated",
    )(slab, slab, slab, near, far)


def _gate_expand_table():
    e = np.zeros((3, A_KV_GROUPS * LANES, A_WIDTH), np.float32)
    for g in range(A_KV_GROUPS):
        for r in range(A_REP):
            h = g * A_REP + r
            for br in range(3):
                e[br, g * LANES + GATE_LANE + r * 3 + br, h * A_HEAD_DIM:(h + 1) * A_HEAD_DIM] = 1.0
    return e


def _silu(z):
    return z * jax.nn.sigmoid(z)


def _out_kernel(ocmp_ref, osel_ref, owin_ref, ks_ref, za_ref, ob_ref, zb_ref, ma_ref, mb_ref,
                x_ref, gate_ref, gpost_ref, e_ref, wa_ref, wb_ref, wo_ref, o_ref):
    sg = jax.nn.sigmoid(ks_ref[...].astype(F32)).astype(BF16)
    oa = None
    for br, ref in enumerate((ocmp_ref, osel_ref, owin_ref)):
        gexp = jnp.dot(sg, e_ref[br], preferred_element_type=F32)
        term = gexp * ref[...].astype(F32)
        oa = term if oa is None else oa + term
    a_in = (oa * _silu(za_ref[...].astype(F32))).astype(BF16)
    ya = jnp.dot(a_in, wa_ref[...], preferred_element_type=F32)
    b_in = (ob_ref[...].astype(F32) * _silu(zb_ref[...].astype(F32))).astype(BF16)
    yb = jnp.dot(b_in, wb_ref[...], preferred_element_type=F32)
    merged = (jax.nn.sigmoid(ma_ref[...].astype(F32)) * ya
              + jax.nn.sigmoid(mb_ref[...].astype(F32)) * yb)
    out = jnp.dot(merged.astype(BF16), wo_ref[...], preferred_element_type=F32)
    y = out * lax.rsqrt(jnp.mean(out * out, axis=-1, keepdims=True) + EPS) * gpost_ref[...]
    o_ref[...] = x_ref[...] + gate_ref[0] * y


def _out(ocmp, osel, owin, ob, slab, x2d, ada3, g_post, wa, wb, wo, seq):
    M, D = x2d.shape
    tm = 256
    per_b = seq // tm
    e = jnp.asarray(_gate_expand_table(), BF16)
    const = lambda a: pl.BlockSpec(a.shape, lambda i: (0,) * a.ndim, pipeline_mode=pl.Buffered(1))
    row = lambda w, cb: pl.BlockSpec((tm, w), lambda i: (i, cb))
    return pl.pallas_call(
        _out_kernel,
        grid=(M // tm,),
        in_specs=[row(A_WIDTH, 0), row(A_WIDTH, 0), row(A_WIDTH, 0),
                  row(A_KV_GROUPS * LANES, COL_KS // (A_KV_GROUPS * LANES)),
                  row(A_WIDTH, COL_ZA // A_WIDTH), row(B_WIDTH, 0), row(B_WIDTH, COL_ZB // B_WIDTH),
                  row(D_MODEL, COL_MA // D_MODEL), row(D_MODEL, COL_MB // D_MODEL),
                  row(D, 0),
                  pl.BlockSpec((1, 1, D), lambda i: (i // per_b, 0, 2)),
                  pl.BlockSpec((1, D), lambda i: (0, 0)),
                  const(e), const(wa), const(wb), const(wo)],
        out_specs=pl.BlockSpec((tm, D), lambda i: (i, 0)),
        out_shape=jax.ShapeDtypeStruct((M, D), F32),
        compiler_params=_cparams(("arbitrary",)),
        name="out",
    )(ocmp, osel, owin, slab, slab, ob, slab, slab, slab, x2d, ada3, g_post.reshape(1, D), e, wa, wb, wo)


def _layer(x, c, positions, w_ada, b_ada, g_pre, g_post, w_in, pe_ck, pe_cv, w_ck1, w_ck2,
           w_cv1, w_cv2, w_br_a, w_br_b, w_out):
    B, S, D = x.shape
    x2d = x.reshape(B * S, D)
    ada3 = _ada(c, w_ada, b_ada).reshape(B, 1, 3 * D)
    slab, kcv = _proj(x2d, ada3, g_pre, positions.reshape(B * S, 1), _build_w_slab(w_in), S)

    nblk = S // CMP_STRIDE
    assert (S - CMP_BLOCK) // CMP_STRIDE + 1 == nblk - 1
    cmp_end = np.minimum(np.arange(nblk) * CMP_STRIDE + CMP_BLOCK - 1, S - 1)
    posc = positions[:, cmp_end].reshape(B, nblk, 1)
    dup = lambda w: jnp.concatenate([w, w], axis=1).astype(BF16)
    pe2 = jnp.concatenate([pe_ck, pe_cv], axis=1)
    kc2, vc2 = _cmp_kv(kcv, pe2, posc, _build_cmp_w1(w_ck1, w_cv1), dup(w_ck2), dup(w_cv2), B, S)

    ocmp, qaug = _cmp_attn(slab, kc2, vc2, B, S)
    osel = _nsa_flash(qaug, slab, COL_KS, COL_VS, None, B, S)
    owin = _nsa_flash(qaug, slab, COL_KW, COL_VW, WIN_SIZE, B, S)
    ob = _dil(slab, B, S)
    out = _out(ocmp, osel, owin, ob, slab, x2d, ada3, g_post,
               w_br_a.astype(BF16), w_br_b.astype(BF16), w_out.astype(BF16), S)
    return out.reshape(B, S, D)


def kernel(x, c, positions, w_ada, b_ada, g_pre, g_post, w_in, pe_ck, pe_cv, w_ck1, w_ck2, w_cv1, w_cv2, w_br_a, w_br_b, w_out):
    h = x
    for layer in range(w_ada.shape[0]):
        h = _layer(h, c, positions, w_ada[layer], b_ada[layer], g_pre[layer], g_post[layer],
                   w_in[layer], pe_ck[layer], pe_cv[layer], w_ck1[layer], w_ck2[layer],
                   w_cv1[layer], w_cv2[layer], w_br_a[layer], w_br_b[layer], w_out[layer])
    return h
```

```python
import functools

import numpy as np
import jax
import jax.numpy as jnp
from jax import lax
from jax.experimental import pallas as pl
from jax.experimental.pallas import tpu as pltpu

F32 = jnp.float32
BF16 = jnp.bfloat16

D_MODEL = 2048
A_HEADS = 16
A_HEAD_DIM = 64
A_KV_GROUPS = 4
A_REP = A_HEADS // A_KV_GROUPS
A_WIDTH = A_HEADS * A_HEAD_DIM
A_KV_WIDTH = A_KV_GROUPS * A_HEAD_DIM
CMP_BLOCK = 32
CMP_STRIDE = 16
CMP_HIDDEN = 4 * A_HEAD_DIM
SEL_BLOCK = 64
SEL_TOPK = 16
WIN_SIZE = 512
FORCE_BONUS = 1.0e4
B_HEADS = 8
B_HEAD_DIM = 128
B_WIDTH = B_HEADS * B_HEAD_DIM
ROPE_THETA = 500000.0
EPS = 1e-6
NEG = -1e30
M_INIT = -1.0e38
LOG2E = 1.4426950408889634

LANES = 128
MXU_N = 256
VMEM_LIMIT = 56 * 1024 * 1024

TN = 512
COL_QA = 0
COL_KS = 1024
COL_KW = 1536
COL_QB = 2048
COL_KB = 3072
COL_VB = 4096
COL_ZA = 5120
COL_ZB = 6144
COL_VS = 7168
COL_VW = 7680
COL_MA = 8192
COL_MB = 10240
N_SLAB = 12288
COL_KCV = N_SLAB
N_PROJ = N_SLAB + 2 * A_KV_WIDTH
GATE_LANE = 96
AUG_LANE = 64
ONES_LANE = 64

T_ATT = 256


def _cparams(sem):
    return pltpu.CompilerParams(dimension_semantics=sem, vmem_limit_bytes=VMEM_LIMIT)


def _ada_kernel(c_ref, w_ref, b_ref, o_ref):
    o_ref[...] = jnp.dot(c_ref[...], w_ref[...], preferred_element_type=F32,
                         precision=lax.Precision.HIGHEST) + b_ref[...]


def _ada(c, w_ada, b_ada):
    B, D = c.shape
    N = w_ada.shape[1]
    tn = 768
    return pl.pallas_call(
        _ada_kernel,
        grid=(N // tn,),
        in_specs=[pl.BlockSpec((B, D), lambda j: (0, 0)),
                  pl.BlockSpec((D, tn), lambda j: (0, j)),
                  pl.BlockSpec((1, tn), lambda j: (0, j))],
        out_specs=pl.BlockSpec((B, tn), lambda j: (0, j)),
        out_shape=jax.ShapeDtypeStruct((B, N), F32),
        compiler_params=_cparams(("arbitrary",)),
        name="ada",
    )(c, w_ada, b_ada.reshape(1, N))


def _rope_inv_tables():
    lane = np.arange(LANES)
    inv_a8 = (ROPE_THETA ** (-np.arange(0, 16, 2) / 16)).astype(np.float32)
    inv_b16 = (ROPE_THETA ** (-np.arange(0, 32, 2) / 32)).astype(np.float32)
    la = lane % A_HEAD_DIM
    inv_a = np.where(la < 16, inv_a8[la % 8], 0.0).astype(np.float32)
    inv_b = np.where(lane < 32, inv_b16[lane % 16], 0.0).astype(np.float32)
    return inv_a.reshape(1, LANES), inv_b.reshape(1, LANES)


def _rope_tables(pos_f, inv, period, half):
    ang = pos_f * inv
    c, s = jnp.cos(ang), jnp.sin(ang)
    lane = lax.broadcasted_iota(jnp.int32, ang.shape, 1) % period
    cos_t = jnp.where(lane < 2 * half, c, 1.0)
    sin_m = jnp.where(lane < half, -s, 0.0)
    sin_p = jnp.where((lane >= half) & (lane < 2 * half), s, 0.0)
    return cos_t, sin_m, sin_p


def _rope_apply(x, cos_t, sin_m, sin_p, half):
    return (x * cos_t + pltpu.roll(x, LANES - half, 1) * sin_m
            + pltpu.roll(x, half, 1) * sin_p)


def _proj_kernel(x_ref, shift_ref, scale_ref, gpre_ref, pos_ref, inva_ref, invb_ref, w_ref,
                 o_ref, kcv_ref, h_scr, ca, sma, spa, cb, smb, spb, *, tm, seq):
    i = pl.program_id(0)
    j = pl.program_id(1)

    @pl.when(j == 0)
    def _():
        x = x_ref[...]
        ms = jnp.mean(x * x, axis=-1, keepdims=True)
        y = x * lax.rsqrt(ms + EPS) * gpre_ref[...]
        h = y * (1.0 + scale_ref[0]) + shift_ref[0]
        h_scr[...] = h.astype(BF16)
        pos_f = pos_ref[...].astype(F32)
        c, sm, sp = _rope_tables(pos_f, inva_ref[...], A_HEAD_DIM, 8)
        ca[...] = c
        sma[...] = sm
        spa[...] = sp
        c, sm, sp = _rope_tables(pos_f, invb_ref[...], LANES, 16)
        cb[...] = c
        smb[...] = sm
        spb[...] = sp

    def tile(epilogue):
        for c0 in range(0, TN, MXU_N):
            acc = jnp.dot(h_scr[...], w_ref[:, c0:c0 + MXU_N], preferred_element_type=F32)
            epilogue(acc, c0)

    def rope_epilogue(tabs, half, mul, first_head_only, onehot):
        def epilogue(acc, c0):
            cos_t, sin_m, sin_p = tabs[0][...], tabs[1][...], tabs[2][...]
            lane = lax.broadcasted_iota(jnp.int32, (tm, LANES), 1)
            if first_head_only:
                keep = lane < A_HEAD_DIM
                cos_t = jnp.where(keep, cos_t, 1.0)
                sin_m = jnp.where(keep, sin_m, 0.0)
                sin_p = jnp.where(keep, sin_p, 0.0)
            if onehot:
                t = (i * tm + lax.broadcasted_iota(jnp.int32, (tm, LANES), 0)) % seq
                hot = (lane >= AUG_LANE) & (lane < AUG_LANE + seq // SEL_BLOCK) & \
                      ((t // SEL_BLOCK) == (lane - AUG_LANE))
            for cidx in range(MXU_N // LANES):
                xc = acc[:, cidx * LANES:(cidx + 1) * LANES]
                r = _rope_apply(xc, cos_t, sin_m, sin_p, half)
                if mul != 1.0:
                    r = r * mul
                if onehot:
                    r = jnp.where(hot, 1.0, r)
                o_ref[:, c0 + cidx * LANES:c0 + (cidx + 1) * LANES] = r.astype(o_ref.dtype)
        return epilogue

    def plain_epilogue(acc, c0):
        o_ref[:, c0:c0 + MXU_N] = acc.astype(o_ref.dtype)

    def ones_lane_epilogue(acc, c0):
        lane = lax.broadcasted_iota(jnp.int32, acc.shape, 1) % LANES
        o_ref[:, c0:c0 + MXU_N] = jnp.where(lane == ONES_LANE, 1.0, acc).astype(o_ref.dtype)

    def kcv_epilogue(acc, c0):
        kcv_ref[:, c0:c0 + MXU_N] = acc

    ta = (ca, sma, spa)
    tb = (cb, smb, spb)
    is_vpad = (j == COL_VS // TN) | (j == COL_VW // TN)
    pl.when(j < COL_KS // TN)(
        lambda: tile(rope_epilogue(ta, 8, A_HEAD_DIM ** -0.5 * LOG2E, False, False)))
    pl.when(j == COL_KS // TN)(lambda: tile(rope_epilogue(ta, 8, 1.0, True, True)))
    pl.when(j == COL_KW // TN)(lambda: tile(rope_epilogue(ta, 8, 1.0, True, False)))
    pl.when((j >= COL_QB // TN) & (j < COL_KB // TN))(
        lambda: tile(rope_epilogue(tb, 16, B_HEAD_DIM ** -0.5 * LOG2E, False, False)))
    pl.when((j >= COL_KB // TN) & (j < COL_VB // TN))(
        lambda: tile(rope_epilogue(tb, 16, 1.0, False, False)))
    pl.when((j >= COL_VB // TN) & (j < COL_KCV // TN) & jnp.logical_not(is_vpad))(
        lambda: tile(plain_epilogue))
    pl.when(is_vpad)(lambda: tile(ones_lane_epilogue))
    pl.when(j == COL_KCV // TN)(lambda: tile(kcv_epilogue))


def _proj(x2d, ada3, g_pre, pos2d, w_slab, seq):
    M, D = x2d.shape
    tm = 1024
    per_b = seq // tm
    inv_a, inv_b = _rope_inv_tables()
    tab = pltpu.VMEM((tm, LANES), F32)
    last_slab_tile = N_SLAB // TN - 1
    return pl.pallas_call(
        functools.partial(_proj_kernel, tm=tm, seq=seq),
        grid=(M // tm, N_PROJ // TN),
        in_specs=[pl.BlockSpec((tm, D), lambda i, j: (i, 0)),
                  pl.BlockSpec((1, 1, D), lambda i, j: (i // per_b, 0, 0)),
                  pl.BlockSpec((1, 1, D), lambda i, j: (i // per_b, 0, 1)),
                  pl.BlockSpec((1, D), lambda i, j: (0, 0)),
                  pl.BlockSpec((tm, 1), lambda i, j: (i, 0)),
                  pl.BlockSpec((1, LANES), lambda i, j: (0, 0)),
                  pl.BlockSpec((1, LANES), lambda i, j: (0, 0)),
                  pl.BlockSpec((D, TN), lambda i, j: (0, j))],
        out_specs=[pl.BlockSpec((tm, TN), lambda i, j: (i, jnp.minimum(j, last_slab_tile))),
                   pl.BlockSpec((tm, TN), lambda i, j: (i, 0))],
        out_shape=[jax.ShapeDtypeStruct((M, N_SLAB), BF16),
                   jax.ShapeDtypeStruct((M, N_PROJ - N_SLAB), F32)],
        scratch_shapes=[pltpu.VMEM((tm, D), BF16), tab, tab, tab, tab, tab, tab],
        compiler_params=_cparams(("arbitrary", "arbitrary")),
        name="proj",
    )(x2d, ada3, ada3, g_pre.reshape(1, D), pos2d, jnp.asarray(inv_a), jnp.asarray(inv_b), w_slab)


_SRC_SIZES = (A_WIDTH,) + (A_KV_WIDTH,) * 6 + (A_HEADS * 3, A_WIDTH) + (B_WIDTH,) * 4 + (D_MODEL, D_MODEL)
_SRC = [int(v) for v in np.concatenate([[0], np.cumsum(_SRC_SIZES)])]
(SRC_QA, SRC_KC, SRC_VC, SRC_KS, SRC_VS, SRC_KW, SRC_VW, SRC_GA, SRC_ZA, SRC_QB, SRC_KB, SRC_VB,
 SRC_ZB, SRC_MA, SRC_MB, SRC_END) = _SRC


def _w_slab_kernel(w_ref, o_ref):
    rows = w_ref.shape[0]

    def put(dst, src, width):
        o_ref[:, dst:dst + width] = w_ref[:, src:src + width].astype(o_ref.dtype)

    put(COL_QA, SRC_QA, A_WIDTH)
    put(COL_QB, SRC_QB, B_WIDTH)
    put(COL_KB, SRC_KB, B_WIDTH)
    put(COL_VB, SRC_VB, B_WIDTH)
    put(COL_ZA, SRC_ZA, A_WIDTH)
    put(COL_ZB, SRC_ZB, B_WIDTH)
    put(COL_MA, SRC_MA, D_MODEL)
    put(COL_MB, SRC_MB, D_MODEL)
    ngate = A_REP * 3
    zeros = lambda n: jnp.zeros((rows, n), o_ref.dtype)
    for g in range(A_KV_GROUPS):
        d = g * LANES
        s = g * A_HEAD_DIM
        put(COL_KS + d, SRC_KS + s, A_HEAD_DIM)
        o_ref[:, COL_KS + d + A_HEAD_DIM:COL_KS + d + GATE_LANE] = zeros(GATE_LANE - A_HEAD_DIM)
        put(COL_KS + d + GATE_LANE, SRC_GA + g * ngate, ngate)
        o_ref[:, COL_KS + d + GATE_LANE + ngate:COL_KS + d + LANES] = zeros(LANES - GATE_LANE - ngate)
        for col, src in ((COL_KW, SRC_KW), (COL_VS, SRC_VS), (COL_VW, SRC_VW)):
            put(col + d, src + s, A_HEAD_DIM)
            o_ref[:, col + d + A_HEAD_DIM:col + d + LANES] = zeros(LANES - A_HEAD_DIM)
        put(COL_KCV + d, SRC_KC + s, A_HEAD_DIM)
        put(COL_KCV + d + A_HEAD_DIM, SRC_VC + s, A_HEAD_DIM)


def _build_w_slab(w_in):
    D, n_in = w_in.shape
    assert n_in == SRC_END
    rows = 128
    return pl.pallas_call(
        _w_slab_kernel,
        grid=(D // rows,),
        in_specs=[pl.BlockSpec((rows, n_in), lambda i: (i, 0))],
        out_specs=pl.BlockSpec((rows, N_PROJ), lambda i: (i, 0)),
        out_shape=jax.ShapeDtypeStruct((D, N_PROJ), BF16),
        compiler_params=_cparams(("arbitrary",)),
        name="w_slab",
    )(w_in)


def _cmp_kv_kernel(t_ref, pe_ref, posc_ref, inva_ref, w1_ref, wk2_ref, wv2_ref, kc2_ref, vc2_ref):
    nblk = t_ref.shape[0] // CMP_STRIDE
    hidden2 = w1_ref.shape[-1]
    p_lo = jnp.zeros((nblk, hidden2), F32)
    p_hi = jnp.zeros((nblk, hidden2), F32)
    for l in range(CMP_STRIDE):
        x = t_ref[pl.ds(l, nblk, stride=CMP_STRIDE), :]
        p_lo += jnp.dot((x + pe_ref[l:l + 1, :]).astype(BF16), w1_ref[l],
                        preferred_element_type=F32)
        p_hi += jnp.dot((x + pe_ref[CMP_STRIDE + l:CMP_STRIDE + l + 1, :]).astype(BF16),
                        w1_ref[CMP_STRIDE + l], preferred_element_type=F32)
    hid = p_lo + pltpu.roll(p_hi, nblk - 1, 0)
    act = jax.nn.gelu(hid).astype(BF16)
    kk = jnp.dot(act[:, :CMP_HIDDEN], wk2_ref[...], preferred_element_type=F32)
    vv = jnp.dot(act[:, CMP_HIDDEN:], wv2_ref[...], preferred_element_type=F32)
    pos_f = posc_ref[...].astype(F32)
    cos_t, sin_m, sin_p = _rope_tables(pos_f, inva_ref[...], A_HEAD_DIM, 8)
    kk = _rope_apply(kk, cos_t, sin_m, sin_p, 8)
    lane = lax.broadcasted_iota(jnp.int32, kk.shape, 1)
    lo = lane < A_HEAD_DIM
    kc2_ref[0:nblk, :] = jnp.where(lo, kk, 0.0).astype(kc2_ref.dtype)
    kc2_ref[nblk:2 * nblk, :] = jnp.where(lo, 0.0, kk).astype(kc2_ref.dtype)
    vc2_ref[0:nblk, :] = jnp.where(lo, vv, 0.0).astype(vc2_ref.dtype)
    vc2_ref[nblk:2 * nblk, :] = jnp.where(lo, 0.0, vv).astype(vc2_ref.dtype)


def _cmp_kv(kcv, pe2, posc, w1, wk2d, wv2d, B, seq):
    G = A_KV_GROUPS
    nblk = seq // CMP_STRIDE
    inv_a, _ = _rope_inv_tables()
    full = lambda a: pl.BlockSpec(a.shape, lambda b, g: (0,) * a.ndim)
    out = jax.ShapeDtypeStruct((B, G, 2 * nblk, LANES), BF16)
    ospec = pl.BlockSpec((None, None, 2 * nblk, LANES), lambda b, g: (b, g, 0, 0))
    return pl.pallas_call(
        _cmp_kv_kernel,
        grid=(B, G),
        in_specs=[pl.BlockSpec((seq, LANES), lambda b, g: (b, g)),
                  full(pe2),
                  pl.BlockSpec((None, nblk, 1), lambda b, g: (b, 0, 0)),
                  pl.BlockSpec((1, LANES), lambda b, g: (0, 0)),
                  full(w1), full(wk2d), full(wv2d)],
        out_specs=[ospec, ospec],
        out_shape=[out, out],
        compiler_params=_cparams(("arbitrary", "arbitrary")),
        name="cmp_kv",
    )(kcv, pe2, posc, jnp.asarray(inv_a), w1, wk2d, wv2d)


def _build_cmp_w1(w_ck1, w_cv1):
    wk = w_ck1.reshape(CMP_BLOCK, A_HEAD_DIM, CMP_HIDDEN)
    wv = w_cv1.reshape(CMP_BLOCK, A_HEAD_DIM, CMP_HIDDEN)
    z = jnp.zeros_like(wk)
    return jnp.concatenate([jnp.concatenate([wk, z], axis=2),
                            jnp.concatenate([z, wv], axis=2)], axis=1).astype(BF16)


def _gate_expand_table(branch):
    e = np.zeros((LANES, A_REP * LANES), np.float32)
    for r in range(A_REP):
        e[GATE_LANE + 3 * r + branch, r * LANES:(r + 1) * LANES] = 1.0
    return e


def _branch_gates(gate_blk, e_ref):
    sg = jax.nn.sigmoid(gate_blk.astype(F32)).astype(BF16)
    return jnp.dot(sg, e_ref[...], preferred_element_type=F32)


def _overlap_table_t(seq):
    n_c = (seq - CMP_BLOCK) // CMP_STRIDE + 1
    n_s = seq // SEL_BLOCK
    cs = np.arange(n_c) * CMP_STRIDE
    ss = np.arange(n_s) * SEL_BLOCK
    ov = np.clip(np.minimum(cs[:, None] + CMP_BLOCK, ss[None, :] + SEL_BLOCK)
                 - np.maximum(cs[:, None], ss[None, :]), 0, None).astype(np.float32) / CMP_BLOCK
    full = np.zeros((LANES, LANES), np.float32)
    full[AUG_LANE:AUG_LANE + n_s, :n_c] = ov.T
    return full


def _split3(x):
    hi = x.astype(BF16)
    r1 = x - hi.astype(F32)
    mid = r1.astype(BF16)
    lo = (r1 - mid.astype(F32)).astype(BF16)
    return hi, mid, lo


def _cmp_attn_kernel(q_ref, kc2_ref, vc2_ref, ovt_ref, gate_ref, e_ref, ocmp_ref, qaug_ref, *, tq, n_sel_blocks):
    qi = pl.program_id(2)
    t = qi * tq + lax.broadcasted_iota(jnp.int32, (tq, LANES), 0)
    lane = lax.broadcasted_iota(jnp.int32, (tq, LANES), 1)
    cmask = (CMP_STRIDE * lane + (CMP_BLOCK - 1)) <= t
    cmask_f = cmask.astype(F32)
    nt = (((1,), (1,)), ((), ()))
    psum = jnp.zeros((tq, LANES), F32)
    gates = _branch_gates(gate_ref[...], e_ref)
    for pair in range(A_REP // 2):
        qp = q_ref[:, pair * LANES:(pair + 1) * LANES]
        s2 = lax.dot_general(qp, kc2_ref[...], nt, preferred_element_type=F32)
        probs = []
        for hh in range(2):
            s = jnp.where(cmask, s2[:, hh * LANES:(hh + 1) * LANES], NEG)
            mx = jnp.max(s, axis=-1, keepdims=True)
            e = jnp.exp2(s - mx) * cmask_f
            pr = e / jnp.maximum(jnp.sum(e, axis=-1, keepdims=True), 1.0)
            probs.append(pr)
            psum = psum + pr
        p2 = jnp.concatenate(probs, axis=1).astype(BF16)
        o_pair = jnp.dot(p2, vc2_ref[...], preferred_element_type=F32)
        gate = jnp.where(lane < A_HEAD_DIM, gates[:, (2 * pair) * LANES:(2 * pair + 1) * LANES],
                         gates[:, (2 * pair + 1) * LANES:(2 * pair + 2) * LANES])
        ocmp_ref[:, pair * LANES:(pair + 1) * LANES] = (gate * o_pair).astype(ocmp_ref.dtype)

    ovt = ovt_ref[...]
    imp_t = sum(lax.dot_general(ovt, part, nt, preferred_element_type=F32) for part in _split3(psum))
    imp_t = imp_t[AUG_LANE:AUG_LANE + n_sel_blocks, :]
    jb = lax.broadcasted_iota(jnp.int32, imp_t.shape, 0)
    tcol = qi * tq + lax.broadcasted_iota(jnp.int32, imp_t.shape, 1)
    cur = tcol // SEL_BLOCK
    valid = (SEL_BLOCK * jb) <= tcol
    forced = valid & ((jb == 0) | (jb == cur) | (jb == cur - 1))
    score = jnp.where(valid, imp_t + FORCE_BONUS * forced.astype(F32), NEG)
    rank = jnp.zeros(imp_t.shape, F32)
    for i in range(n_sel_blocks):
        other = score[i:i + 1, :]
        beats = (other > score) | ((other == score) & (jb > i))
        rank = rank + beats.astype(F32)
    selb = jnp.where(rank < float(min(SEL_TOPK, n_sel_blocks)), 0.0, NEG)
    aug_t = jnp.concatenate([jnp.zeros((AUG_LANE, tq), F32), selb,
                             jnp.zeros((LANES - AUG_LANE - n_sel_blocks, tq), F32)], axis=0)
    aug = aug_t.T.astype(BF16)
    lo = lane < A_HEAD_DIM
    for r in range(A_REP):
        qp = q_ref[:, (r // 2) * LANES:(r // 2 + 1) * LANES]
        if r % 2:
            qp = jnp.concatenate([qp[:, A_HEAD_DIM:], qp[:, :A_HEAD_DIM]], axis=1)
        qaug_ref[r] = jnp.where(lo, qp, aug)


def _cmp_attn(slab, kc2, vc2, B, seq):
    tq = T_ATT
    nq = seq // tq
    G = A_KV_GROUPS
    n_s = seq // SEL_BLOCK
    ovt = jnp.asarray(_overlap_table_t(seq), BF16)
    e = jnp.asarray(_gate_expand_table(0), BF16)
    kv_spec = pl.BlockSpec((None, None) + kc2.shape[2:], lambda b, g, q: (b, g, 0, 0))
    return pl.pallas_call(
        functools.partial(_cmp_attn_kernel, tq=tq, n_sel_blocks=n_s),
        grid=(B, G, nq),
        in_specs=[pl.BlockSpec((tq, A_REP * A_HEAD_DIM), lambda b, g, q: (b * nq + q, g)),
                  kv_spec, kv_spec,
                  pl.BlockSpec((LANES, LANES), lambda b, g, q: (0, 0)),
                  pl.BlockSpec((tq, LANES), lambda b, g, q: (b * nq + q, COL_KS // LANES + g)),
                  pl.BlockSpec(e.shape, lambda b, g, q: (0, 0))],
        out_specs=[pl.BlockSpec((tq, A_REP * A_HEAD_DIM), lambda b, g, q: (b * nq + q, g)),
                   pl.BlockSpec((None, None, A_REP, tq, LANES), lambda b, g, q: (b, g, 0, q, 0))],
        out_shape=[jax.ShapeDtypeStruct((B * seq, A_WIDTH), BF16),
                   jax.ShapeDtypeStruct((B, G, A_REP, seq, LANES), BF16)],
        compiler_params=_cparams(("arbitrary", "arbitrary", "arbitrary")),
        name="cmp_attn",
    )(slab, kc2, vc2, ovt, slab, e)


def _flash_init(m_scr, acc_scr, l_scr=None):
    m_scr[...] = jnp.full(m_scr.shape, M_INIT, F32)
    acc_scr[...] = jnp.zeros(acc_scr.shape, F32)
    if l_scr is not None:
        l_scr[...] = jnp.zeros(l_scr.shape, F32)


def _scores(q, k):
    return lax.dot_general(q, k, (((1,), (1,)), ((), ())), preferred_element_type=F32)


def _add_bias(s, bias, reps):
    tq, tk = bias.shape
    return (s.reshape(reps, tq, tk) + bias[None]).reshape(reps * tq, tk)


def _flash_update(s, v, m_scr, acc_scr, l_scr=None):
    chunks = [s[:, c * LANES:(c + 1) * LANES] for c in range(s.shape[1] // LANES)]
    mx = functools.reduce(jnp.maximum, chunks)
    m_prev = m_scr[...]
    m_new = jnp.maximum(m_prev, jnp.max(mx, axis=-1, keepdims=True))
    alpha = jnp.exp2(m_prev - m_new)
    ps = [jnp.exp2(ch - m_new) for ch in chunks]
    p = jnp.concatenate(ps, axis=1).astype(v.dtype)
    acc_scr[...] = alpha * acc_scr[...] + jnp.dot(p, v, preferred_element_type=F32)
    if l_scr is not None:
        l_scr[...] = alpha * l_scr[...] + functools.reduce(jnp.add, ps)
    m_scr[...] = m_new


def _gated_head(acc, gates, r):
    lane = lax.broadcasted_iota(jnp.int32, acc.shape, 1)
    scale = gates[:, r * LANES:(r + 1) * LANES] / acc[:, ONES_LANE:ONES_LANE + 1]
    return jnp.where(lane < A_HEAD_DIM, acc * scale, 0.0)


def _store_head_pair(o_ref, pair, even, odd):
    o_ref[:, pair * LANES:(pair + 1) * LANES] = (even + pltpu.roll(odd, A_HEAD_DIM, 1)).astype(o_ref.dtype)


def _nsa_sel_kernel(q_ref, k_ref, v_ref, e_ref, o_ref, m_scr, acc_scr, s0_scr, s1_scr, *, t):
    qi = pl.program_id(2)
    q = q_ref[...].reshape(A_REP * t, LANES)
    _flash_init(m_scr, acc_scr)
    rows_of = lambda ref, kj: ref[pl.ds(pl.multiple_of(kj * t, t), t), :]

    def score_into(buf, kj):
        buf[...] = _scores(q, rows_of(k_ref, kj))

    def fold(buf, kj, causal=False):
        s = buf[...]
        if causal:
            delta = (lax.broadcasted_iota(jnp.int32, (t, t), 0)
                     - lax.broadcasted_iota(jnp.int32, (t, t), 1))
            s = _add_bias(s, jnp.where(delta >= 0, 0.0, NEG), A_REP)
        _flash_update(s, rows_of(v_ref, kj), m_scr, acc_scr)

    score_into(s0_scr, 0)

    def body(i, carry):
        kj = 2 * i
        score_into(s1_scr, kj + 1)
        fold(s0_scr, kj)
        score_into(s0_scr, kj + 2)
        fold(s1_scr, kj + 1)
        return carry
    lax.fori_loop(0, qi // 2, body, 0)

    @pl.when(qi % 2 == 1)
    def _():
        score_into(s1_scr, qi)
        fold(s0_scr, qi - 1)
        fold(s1_scr, qi, causal=True)

    @pl.when(qi % 2 == 0)
    def _():
        fold(s0_scr, qi, causal=True)

    gates = _branch_gates(rows_of(k_ref, qi), e_ref)
    heads = [_gated_head(acc_scr[r * t:(r + 1) * t, :], gates, r) for r in range(A_REP)]
    for pair in range(A_REP // 2):
        _store_head_pair(o_ref, pair, heads[2 * pair], heads[2 * pair + 1])


def _win_bias_table(t, span, window):
    delta = np.arange(t)[:, None] - np.arange(span)[None, :]
    dist = np.stack([n * t + delta for n in range(span // t)])
    return np.where((dist >= 0) & (dist < window), 0.0, NEG).astype(np.float32)


def _nsa_win_kernel(q_ref, k_ref, v_ref, e_ref, gate_ref, bias_ref, o_ref, *, t, span):
    qi = pl.program_id(2)
    start = pl.multiple_of(jnp.maximum((qi + 1) * t - span, 0), t)
    k = k_ref[pl.ds(start, span), :]
    v = v_ref[pl.ds(start, span), :]
    bias = bias_ref[jnp.minimum(qi, span // t - 1)]
    gates = _branch_gates(gate_ref[...], e_ref)
    for pair in range(A_REP // 2):
        halves = []
        for r in (2 * pair, 2 * pair + 1):
            s = _scores(q_ref[r], k) + bias
            chunks = [s[:, c * LANES:(c + 1) * LANES] for c in range(span // LANES)]
            m = jnp.max(functools.reduce(jnp.maximum, chunks), axis=-1, keepdims=True)
            p = jnp.concatenate([jnp.exp2(ch - m) for ch in chunks], axis=1).astype(BF16)
            acc = jnp.dot(p, v, preferred_element_type=F32)
            halves.append(_gated_head(acc, gates, r))
        _store_head_pair(o_ref, pair, halves[0], halves[1])


def _nsa_flash(qaug, slab, col_k, col_v, window, B, seq):
    t = T_ATT
    nq = seq // t
    G = A_KV_GROUPS
    rows = A_REP * t
    in_specs = [pl.BlockSpec((None, None, A_REP, t, LANES), lambda b, g, q: (b, g, 0, q, 0)),
                pl.BlockSpec((seq, LANES), lambda b, g, q: (b, col_k // LANES + g)),
                pl.BlockSpec((seq, LANES), lambda b, g, q: (b, col_v // LANES + g))]
    e = jnp.asarray(_gate_expand_table(1 if window is None else 2), BF16)
    in_specs.append(pl.BlockSpec(e.shape, lambda b, g, q: (0, 0)))
    operands = [qaug, slab, slab, e]
    if window is None:
        body = functools.partial(_nsa_sel_kernel, t=t)
        scratch = [pltpu.VMEM((rows, LANES), F32), pltpu.VMEM((rows, LANES), F32),
                   pltpu.VMEM((rows, t), F32), pltpu.VMEM((rows, t), F32)]
    else:
        span = ((window - 1 + t - 1) // t + 1) * t
        span = min(span, seq)
        body = functools.partial(_nsa_win_kernel, t=t, span=span)
        scratch = []
        bias = jnp.asarray(_win_bias_table(t, span, window))
        in_specs += [pl.BlockSpec((t, LANES), lambda b, g, q: (b * nq + q, COL_KS // LANES + g)),
                     pl.BlockSpec(bias.shape, lambda b, g, q: (0, 0, 0))]
        operands += [slab, bias]
    return pl.pallas_call(
        body,
        grid=(B, G, nq),
        in_specs=in_specs,
        out_specs=pl.BlockSpec((t, A_REP * A_HEAD_DIM), lambda b, g, q: (b * nq + q, g)),
        out_shape=jax.ShapeDtypeStruct((B * seq, A_WIDTH), BF16),
        scratch_shapes=scratch,
        compiler_params=_cparams(("arbitrary", "arbitrary", "arbitrary")),
        name="nsa_sel" if window is None else "nsa_win",
    )(*operands)


DILATIONS = ((128, 1), (512, 4), (2048, 16))
TQ_DIL = 512
TK_DIL = 256


def _dil_bias_tables(tq, tk, seq):
    w_near = max(w for w, r in DILATIONS if w < seq)
    near_keys = tq + -(-w_near // tk) * tk
    w_far, r_far = DILATIONS[-1]
    assert w_far >= seq and tq % r_far == 0 and tk % r_far == 0 and near_keys == 2 * tq
    assert tq % (2 * tk) == 0
    delta = np.arange(tq)[:, None] - np.arange(near_keys)[None, :]
    near = []
    for base in (near_keys - tq, 0):
        dist = base + delta
        mult = sum(((dist >= 0) & (dist <= w) & (dist % r == 0)).astype(np.float64) for w, r in DILATIONS)
        near.append(np.where(mult > 0, np.log2(np.maximum(mult, 1.0)), NEG))
    far = np.where(delta[:, :tk] % r_far == 0, 0.0, NEG)
    return np.stack(near).astype(np.float32), far.astype(np.float32)


def _dil_kernel(q_ref, k_ref, v_ref, near_ref, far_ref, o_ref, m_scr, acc_scr, l_scr, s0_scr, s1_scr,
                *, tq, tk):
    qi = pl.program_id(2)
    q = q_ref[...]
    near_keys = near_ref.shape[-1]
    _flash_init(m_scr, acc_scr, l_scr)
    update = functools.partial(_flash_update, m_scr=m_scr, acc_scr=acc_scr, l_scr=l_scr)
    rows_of = lambda ref, kj: ref[pl.ds(pl.multiple_of(kj * tk, tk), tk), :]
    clamped = (qi + 1) * tq < near_keys
    start = pl.multiple_of(jnp.maximum((qi + 1) * tq - near_keys, 0), tk)
    n_far = start // tk

    def score_into(buf, kj):
        buf[...] = _scores(q, rows_of(k_ref, kj))

    def fold(buf, kj):
        update(buf[...] + far_ref[...], rows_of(v_ref, kj))

    @pl.when(n_far > 0)
    def _():
        score_into(s0_scr, 0)

        def body(i, carry):
            kj = 2 * i
            score_into(s1_scr, kj + 1)
            fold(s0_scr, kj)
            score_into(s0_scr, kj + 2)
            fold(s1_scr, kj + 1)
            return carry
        lax.fori_loop(0, n_far // 2, body, 0)

    s = _scores(q, k_ref[pl.ds(start, near_keys), :]) + near_ref[clamped.astype(jnp.int32)]
    update(s, v_ref[pl.ds(start, near_keys), :])
    l = jnp.sum(l_scr[...], axis=-1, keepdims=True)
    o_ref[...] = (acc_scr[...] / l).astype(o_ref.dtype)


def _dil(slab, B, seq):
    tq, tk = TQ_DIL, TK_DIL
    nq = seq // tq
    H = B_HEADS
    near, far = (jnp.asarray(a) for a in _dil_bias_tables(tq, tk, seq))
    return pl.pallas_call(
        functools.partial(_dil_kernel, tq=tq, tk=tk),
        grid=(B, H, nq),
        in_specs=[pl.BlockSpec((tq, LANES), lambda b, h, q: (b * nq + q, COL_QB // LANES + h)),
                  pl.BlockSpec((seq, LANES), lambda b, h, q: (b, COL_KB // LANES + h)),
                  pl.BlockSpec((seq, LANES), lambda b, h, q: (b, COL_VB // LANES + h)),
                  pl.BlockSpec(near.shape, lambda b, h, q: (0, 0, 0)),
                  pl.BlockSpec(far.shape, lambda b, h, q: (0, 0))],
        out_specs=pl.BlockSpec((tq, LANES), lambda b, h, q: (b * nq + q, h)),
        out_shape=jax.ShapeDtypeStruct((B * seq, B_WIDTH), BF16),
        scratch_shapes=[pltpu.VMEM((tq, LANES), F32)] * 3 + [pltpu.VMEM((tq, tk), F32)] * 2,
        compiler_params=_cparams(("arbitrary", "arbitrary", "arbitrary")),
        name="dilated",
    )(slab, slab, slab, near, far)


def _silu(z):
    return z * jax.nn.sigmoid(z)


def _out_kernel(ocmp_ref, osel_ref, owin_ref, za_ref, ob_ref, zb_ref, ma_ref, mb_ref,
                x_ref, gate_ref, gpost_ref, wa_ref, wb_ref, wo_ref, o_ref):
    oa = ocmp_ref[...].astype(F32) + osel_ref[...].astype(F32) + owin_ref[...].astype(F32)
    a_in = (oa * _silu(za_ref[...].astype(F32))).astype(BF16)
    ya = jnp.dot(a_in, wa_ref[...], preferred_element_type=F32)
    b_in = (ob_ref[...].astype(F32) * _silu(zb_ref[...].astype(F32))).astype(BF16)
    yb = jnp.dot(b_in, wb_ref[...], preferred_element_type=F32)
    merged = (jax.nn.sigmoid(ma_ref[...].astype(F32)) * ya
              + jax.nn.sigmoid(mb_ref[...].astype(F32)) * yb)
    out = jnp.dot(merged.astype(BF16), wo_ref[...], preferred_element_type=F32)
    y = out * lax.rsqrt(jnp.mean(out * out, axis=-1, keepdims=True) + EPS) * gpost_ref[...]
    o_ref[...] = x_ref[...] + gate_ref[0] * y


def _out(ocmp, osel, owin, ob, slab, x2d, ada3, g_post, wa, wb, wo, seq):
    M, D = x2d.shape
    tm = 256
    per_b = seq // tm
    const = lambda a: pl.BlockSpec(a.shape, lambda i: (0,) * a.ndim, pipeline_mode=pl.Buffered(1))
    row = lambda w, cb: pl.BlockSpec((tm, w), lambda i: (i, cb))
    return pl.pallas_call(
        _out_kernel,
        grid=(M // tm,),
        in_specs=[row(A_WIDTH, 0), row(A_WIDTH, 0), row(A_WIDTH, 0),
                  row(A_WIDTH, COL_ZA // A_WIDTH), row(B_WIDTH, 0), row(B_WIDTH, COL_ZB // B_WIDTH),
                  row(D_MODEL, COL_MA // D_MODEL), row(D_MODEL, COL_MB // D_MODEL),
                  row(D, 0),
                  pl.BlockSpec((1, 1, D), lambda i: (i // per_b, 0, 2)),
                  pl.BlockSpec((1, D), lambda i: (0, 0)),
                  const(wa), const(wb), const(wo)],
        out_specs=pl.BlockSpec((tm, D), lambda i: (i, 0)),
        out_shape=jax.ShapeDtypeStruct((M, D), F32),
        compiler_params=_cparams(("arbitrary",)),
        name="out",
    )(ocmp, osel, owin, slab, ob, slab, slab, slab, x2d, ada3, g_post.reshape(1, D), wa, wb, wo)


def _layer(x, c, positions, w_ada, b_ada, g_pre, g_post, w_in, pe_ck, pe_cv, w_ck1, w_ck2,
           w_cv1, w_cv2, w_br_a, w_br_b, w_out):
    B, S, D = x.shape
    x2d = x.reshape(B * S, D)
    ada3 = _ada(c, w_ada, b_ada).reshape(B, 1, 3 * D)
    slab, kcv = _proj(x2d, ada3, g_pre, positions.reshape(B * S, 1), _build_w_slab(w_in), S)

    nblk = S // CMP_STRIDE
    assert (S - CMP_BLOCK) // CMP_STRIDE + 1 == nblk - 1
    cmp_end = np.minimum(np.arange(nblk) * CMP_STRIDE + CMP_BLOCK - 1, S - 1)
    posc = positions[:, cmp_end].reshape(B, nblk, 1)
    dup = lambda w: jnp.concatenate([w, w], axis=1).astype(BF16)
    pe2 = jnp.concatenate([pe_ck, pe_cv], axis=1)
    kc2, vc2 = _cmp_kv(kcv, pe2, posc, _build_cmp_w1(w_ck1, w_cv1), dup(w_ck2), dup(w_cv2), B, S)

    ocmp, qaug = _cmp_attn(slab, kc2, vc2, B, S)
    osel = _nsa_flash(qaug, slab, COL_KS, COL_VS, None, B, S)
    owin = _nsa_flash(qaug, slab, COL_KW, COL_VW, WIN_SIZE, B, S)
    ob = _dil(slab, B, S)
    out = _out(ocmp, osel, owin, ob, slab, x2d, ada3, g_post,
               w_br_a.astype(BF16), w_br_b.astype(BF16), w_out.astype(BF16), S)
    return out.reshape(B, S, D)


def kernel(x, c, positions, w_ada, b_ada, g_pre, g_post, w_in, pe_ck, pe_cv, w_ck1, w_ck2, w_cv1, w_cv2, w_br_a, w_br_b, w_out):
    h = x
    for layer in range(w_ada.shape[0]):
        h = _layer(h, c, positions, w_ada[layer], b_ada[layer], g_pre[layer], g_post[layer],
                   w_in[layer], pe_ck[layer], pe_cv[layer], w_ck1[layer], w_ck2[layer],
                   w_cv1[layer], w_cv2[layer], w_br_a[layer], w_br_b[layer], w_out[layer])
    return h
```

```python
import functools

import numpy as np
import jax
import jax.numpy as jnp
from jax import lax
from jax.experimental import pallas as pl
from jax.experimental.pallas import tpu as pltpu

F32 = jnp.float32
BF16 = jnp.bfloat16

D_MODEL = 2048
A_HEADS = 16
A_HEAD_DIM = 64
A_KV_GROUPS = 4
A_REP = A_HEADS // A_KV_GROUPS
A_WIDTH = A_HEADS * A_HEAD_DIM
A_KV_WIDTH = A_KV_GROUPS * A_HEAD_DIM
CMP_BLOCK = 32
CMP_STRIDE = 16
CMP_HIDDEN = 4 * A_HEAD_DIM
SEL_BLOCK = 64
SEL_TOPK = 16
WIN_SIZE = 512
FORCE_BONUS = 1.0e4
B_HEADS = 8
B_HEAD_DIM = 128
B_WIDTH = B_HEADS * B_HEAD_DIM
ROPE_THETA = 500000.0
EPS = 1e-6
NEG = -1e30
M_INIT = -1.0e38
LOG2E = 1.4426950408889634

LANES = 128
MXU_N = 256
VMEM_LIMIT = 56 * 1024 * 1024

TN = 512
COL_QA = 0
COL_KS = 1024
COL_KW = 1536
COL_QB = 2048
COL_KB = 3072
COL_VB = 4096
COL_ZA = 5120
COL_ZB = 6144
COL_VS = 7168
COL_VW = 7680
COL_MA = 8192
COL_MB = 10240
N_SLAB = 12288
COL_KCV = N_SLAB
N_PROJ = N_SLAB + 2 * A_KV_WIDTH
GATE_LANE = 96
AUG_LANE = 64
ONES_LANE = 64

T_ATT = 256


def _cparams(sem):
    return pltpu.CompilerParams(dimension_semantics=sem, vmem_limit_bytes=VMEM_LIMIT)


def _ada_kernel(c_ref, w_ref, b_ref, o_ref):
    o_ref[...] = jnp.dot(c_ref[...], w_ref[...], preferred_element_type=F32,
                         precision=lax.Precision.HIGHEST) + b_ref[...]


def _ada(c, w_ada, b_ada):
    B, D = c.shape
    N = w_ada.shape[1]
    tn = 768
    return pl.pallas_call(
        _ada_kernel,
        grid=(N // tn,),
        in_specs=[pl.BlockSpec((B, D), lambda j: (0, 0)),
                  pl.BlockSpec((D, tn), lambda j: (0, j)),
                  pl.BlockSpec((1, tn), lambda j: (0, j))],
        out_specs=pl.BlockSpec((B, tn), lambda j: (0, j)),
        out_shape=jax.ShapeDtypeStruct((B, N), F32),
        compiler_params=_cparams(("arbitrary",)),
        name="ada",
    )(c, w_ada, b_ada.reshape(1, N))


def _rope_inv_tables():
    lane = np.arange(LANES)
    inv_a8 = (ROPE_THETA ** (-np.arange(0, 16, 2) / 16)).astype(np.float32)
    inv_b16 = (ROPE_THETA ** (-np.arange(0, 32, 2) / 32)).astype(np.float32)
    la = lane % A_HEAD_DIM
    inv_a = np.where(la < 16, inv_a8[la % 8], 0.0).astype(np.float32)
    inv_b = np.where(lane < 32, inv_b16[lane % 16], 0.0).astype(np.float32)
    return inv_a.reshape(1, LANES), inv_b.reshape(1, LANES)


def _rope_tables(pos_f, inv, period, half):
    ang = pos_f * inv
    c, s = jnp.cos(ang), jnp.sin(ang)
    lane = lax.broadcasted_iota(jnp.int32, ang.shape, 1) % period
    cos_t = jnp.where(lane < 2 * half, c, 1.0)
    sin_m = jnp.where(lane < half, -s, 0.0)
    sin_p = jnp.where((lane >= half) & (lane < 2 * half), s, 0.0)
    return cos_t, sin_m, sin_p


def _rope_apply(x, cos_t, sin_m, sin_p, half):
    return (x * cos_t + pltpu.roll(x, LANES - half, 1) * sin_m
            + pltpu.roll(x, half, 1) * sin_p)


def _proj_kernel(x_ref, shift_ref, scale_ref, gpre_ref, pos_ref, inva_ref, invb_ref, w_ref,
                 o_ref, kcv_ref, h_scr, ca, sma, spa, cb, smb, spb, *, tm, seq):
    i = pl.program_id(0)
    j = pl.program_id(1)

    @pl.when(j == 0)
    def _():
        x = x_ref[...]
        ms = jnp.mean(x * x, axis=-1, keepdims=True)
        y = x * lax.rsqrt(ms + EPS) * gpre_ref[...]
        h = y * (1.0 + scale_ref[0]) + shift_ref[0]
        h_scr[...] = h.astype(BF16)
        pos_f = pos_ref[...].astype(F32)
        c, sm, sp = _rope_tables(pos_f, inva_ref[...], A_HEAD_DIM, 8)
        ca[...] = c
        sma[...] = sm
        spa[...] = sp
        c, sm, sp = _rope_tables(pos_f, invb_ref[...], LANES, 16)
        cb[...] = c
        smb[...] = sm
        spb[...] = sp

    def tile(epilogue):
        for c0 in range(0, TN, MXU_N):
            acc = lax.dot_general(h_scr[...], w_ref[c0:c0 + MXU_N, :], (((1,), (1,)), ((), ())),
                                  preferred_element_type=F32)
            epilogue(acc, c0)

    def rope_epilogue(tabs, half, mul, first_head_only, onehot):
        def epilogue(acc, c0):
            cos_t, sin_m, sin_p = tabs[0][...], tabs[1][...], tabs[2][...]
            lane = lax.broadcasted_iota(jnp.int32, (tm, LANES), 1)
            if first_head_only:
                keep = lane < A_HEAD_DIM
                cos_t = jnp.where(keep, cos_t, 1.0)
                sin_m = jnp.where(keep, sin_m, 0.0)
                sin_p = jnp.where(keep, sin_p, 0.0)
            if onehot:
                t = (i * tm + lax.broadcasted_iota(jnp.int32, (tm, LANES), 0)) % seq
                hot = (lane >= AUG_LANE) & (lane < AUG_LANE + seq // SEL_BLOCK) & \
                      ((t // SEL_BLOCK) == (lane - AUG_LANE))
            for cidx in range(MXU_N // LANES):
                xc = acc[:, cidx * LANES:(cidx + 1) * LANES]
                r = _rope_apply(xc, cos_t, sin_m, sin_p, half)
                if mul != 1.0:
                    r = r * mul
                if onehot:
                    r = jnp.where(hot, 1.0, r)
                o_ref[:, c0 + cidx * LANES:c0 + (cidx + 1) * LANES] = r.astype(o_ref.dtype)
        return epilogue

    def plain_epilogue(acc, c0):
        o_ref[:, c0:c0 + MXU_N] = acc.astype(o_ref.dtype)

    def ones_lane_epilogue(acc, c0):
        lane = lax.broadcasted_iota(jnp.int32, acc.shape, 1) % LANES
        o_ref[:, c0:c0 + MXU_N] = jnp.where(lane == ONES_LANE, 1.0, acc).astype(o_ref.dtype)

    def kcv_epilogue(acc, c0):
        kcv_ref[:, c0:c0 + MXU_N] = acc

    ta = (ca, sma, spa)
    tb = (cb, smb, spb)
    is_vpad = (j == COL_VS // TN) | (j == COL_VW // TN)
    pl.when(j < COL_KS // TN)(
        lambda: tile(rope_epilogue(ta, 8, A_HEAD_DIM ** -0.5 * LOG2E, False, False)))
    pl.when(j == COL_KS // TN)(lambda: tile(rope_epilogue(ta, 8, 1.0, True, True)))
    pl.when(j == COL_KW // TN)(lambda: tile(rope_epilogue(ta, 8, 1.0, True, False)))
    pl.when((j >= COL_QB // TN) & (j < COL_KB // TN))(
        lambda: tile(rope_epilogue(tb, 16, B_HEAD_DIM ** -0.5 * LOG2E, False, False)))
    pl.when((j >= COL_KB // TN) & (j < COL_VB // TN))(
        lambda: tile(rope_epilogue(tb, 16, 1.0, False, False)))
    pl.when((j >= COL_VB // TN) & (j < COL_KCV // TN) & jnp.logical_not(is_vpad))(
        lambda: tile(plain_epilogue))
    pl.when(is_vpad)(lambda: tile(ones_lane_epilogue))
    pl.when(j == COL_KCV // TN)(lambda: tile(kcv_epilogue))


def _proj(x2d, ada3, g_pre, pos2d, w_slab_t, seq):
    M, D = x2d.shape
    tm = 1024
    per_b = seq // tm
    inv_a, inv_b = _rope_inv_tables()
    tab = pltpu.VMEM((tm, LANES), F32)
    last_slab_tile = N_SLAB // TN - 1
    return pl.pallas_call(
        functools.partial(_proj_kernel, tm=tm, seq=seq),
        grid=(M // tm, N_PROJ // TN),
        in_specs=[pl.BlockSpec((tm, D), lambda i, j: (i, 0)),
                  pl.BlockSpec((1, 1, D), lambda i, j: (i // per_b, 0, 0)),
                  pl.BlockSpec((1, 1, D), lambda i, j: (i // per_b, 0, 1)),
                  pl.BlockSpec((1, D), lambda i, j: (0, 0)),
                  pl.BlockSpec((tm, 1), lambda i, j: (i, 0)),
                  pl.BlockSpec((1, LANES), lambda i, j: (0, 0)),
                  pl.BlockSpec((1, LANES), lambda i, j: (0, 0)),
                  pl.BlockSpec((TN, D), lambda i, j: (j, 0))],
        out_specs=[pl.BlockSpec((tm, TN), lambda i, j: (i, jnp.minimum(j, last_slab_tile))),
                   pl.BlockSpec((tm, TN), lambda i, j: (i, 0))],
        out_shape=[jax.ShapeDtypeStruct((M, N_SLAB), BF16),
                   jax.ShapeDtypeStruct((M, N_PROJ - N_SLAB), F32)],
        scratch_shapes=[pltpu.VMEM((tm, D), BF16), tab, tab, tab, tab, tab, tab],
        compiler_params=_cparams(("arbitrary", "arbitrary")),
        name="proj",
    )(x2d, ada3, ada3, g_pre.reshape(1, D), pos2d, jnp.asarray(inv_a), jnp.asarray(inv_b), w_slab_t)


_SRC_SIZES = (A_WIDTH,) + (A_KV_WIDTH,) * 6 + (A_HEADS * 3, A_WIDTH) + (B_WIDTH,) * 4 + (D_MODEL, D_MODEL)
_SRC = [int(v) for v in np.concatenate([[0], np.cumsum(_SRC_SIZES)])]
(SRC_QA, SRC_KC, SRC_VC, SRC_KS, SRC_VS, SRC_KW, SRC_VW, SRC_GA, SRC_ZA, SRC_QB, SRC_KB, SRC_VB,
 SRC_ZB, SRC_MA, SRC_MB, SRC_END) = _SRC


def _w_slab_kernel(w_ref, o_ref):
    cols = w_ref.shape[1]

    def put(dst, src, width):
        o_ref[dst:dst + width, :] = w_ref[src:src + width, :].astype(o_ref.dtype)

    def clear(dst, width):
        o_ref[dst:dst + width, :] = jnp.zeros((width, cols), o_ref.dtype)

    put(COL_QA, SRC_QA, A_WIDTH)
    put(COL_QB, SRC_QB, B_WIDTH)
    put(COL_KB, SRC_KB, B_WIDTH)
    put(COL_VB, SRC_VB, B_WIDTH)
    put(COL_ZA, SRC_ZA, A_WIDTH)
    put(COL_ZB, SRC_ZB, B_WIDTH)
    put(COL_MA, SRC_MA, D_MODEL)
    put(COL_MB, SRC_MB, D_MODEL)
    ngate = A_REP * 3
    for g in range(A_KV_GROUPS):
        d = g * LANES
        s = g * A_HEAD_DIM
        put(COL_KS + d, SRC_KS + s, A_HEAD_DIM)
        clear(COL_KS + d + A_HEAD_DIM, GATE_LANE - A_HEAD_DIM)
        put(COL_KS + d + GATE_LANE, SRC_GA + g * ngate, ngate)
        clear(COL_KS + d + GATE_LANE + ngate, LANES - GATE_LANE - ngate)
        for col, src in ((COL_KW, SRC_KW), (COL_VS, SRC_VS), (COL_VW, SRC_VW)):
            put(col + d, src + s, A_HEAD_DIM)
            clear(col + d + A_HEAD_DIM, LANES - A_HEAD_DIM)
        put(COL_KCV + d, SRC_KC + s, A_HEAD_DIM)
        put(COL_KCV + d + A_HEAD_DIM, SRC_VC + s, A_HEAD_DIM)


def _build_w_slab_t(w_in_t):
    n_in, D = w_in_t.shape
    assert n_in == SRC_END
    cols = 256
    return pl.pallas_call(
        _w_slab_kernel,
        grid=(D // cols,),
        in_specs=[pl.BlockSpec((n_in, cols), lambda i: (0, i))],
        out_specs=pl.BlockSpec((N_PROJ, cols), lambda i: (0, i)),
        out_shape=jax.ShapeDtypeStruct((N_PROJ, D), BF16),
        compiler_params=_cparams(("arbitrary",)),
        name="w_slab",
    )(w_in_t)


def _cmp_kv_kernel(t_ref, pe_ref, posc_ref, inva_ref, w1_ref, wk2_ref, wv2_ref, kc2_ref, vc2_ref):
    nblk = t_ref.shape[0] // CMP_STRIDE
    hidden2 = w1_ref.shape[-1]
    p_lo = jnp.zeros((nblk, hidden2), F32)
    p_hi = jnp.zeros((nblk, hidden2), F32)
    for l in range(CMP_STRIDE):
        x = t_ref[pl.ds(l, nblk, stride=CMP_STRIDE), :]
        p_lo += jnp.dot((x + pe_ref[l:l + 1, :]).astype(BF16), w1_ref[l],
                        preferred_element_type=F32)
        p_hi += jnp.dot((x + pe_ref[CMP_STRIDE + l:CMP_STRIDE + l + 1, :]).astype(BF16),
                        w1_ref[CMP_STRIDE + l], preferred_element_type=F32)
    hid = p_lo + pltpu.roll(p_hi, nblk - 1, 0)
    act = jax.nn.gelu(hid).astype(BF16)
    kk = jnp.dot(act[:, :CMP_HIDDEN], wk2_ref[...], preferred_element_type=F32)
    vv = jnp.dot(act[:, CMP_HIDDEN:], wv2_ref[...], preferred_element_type=F32)
    pos_f = posc_ref[...].astype(F32)
    cos_t, sin_m, sin_p = _rope_tables(pos_f, inva_ref[...], A_HEAD_DIM, 8)
    kk = _rope_apply(kk, cos_t, sin_m, sin_p, 8)
    lane = lax.broadcasted_iota(jnp.int32, kk.shape, 1)
    lo = lane < A_HEAD_DIM
    kc2_ref[0:nblk, :] = jnp.where(lo, kk, 0.0).astype(kc2_ref.dtype)
    kc2_ref[nblk:2 * nblk, :] = jnp.where(lo, 0.0, kk).astype(kc2_ref.dtype)
    vc2_ref[0:nblk, :] = jnp.where(lo, vv, 0.0).astype(vc2_ref.dtype)
    vc2_ref[nblk:2 * nblk, :] = jnp.where(lo, 0.0, vv).astype(vc2_ref.dtype)


def _cmp_kv(kcv, pe2, posc, w1, wk2d, wv2d, B, seq):
    G = A_KV_GROUPS
    nblk = seq // CMP_STRIDE
    inv_a, _ = _rope_inv_tables()
    full = lambda a: pl.BlockSpec(a.shape, lambda b, g: (0,) * a.ndim)
    out = jax.ShapeDtypeStruct((B, G, 2 * nblk, LANES), BF16)
    ospec = pl.BlockSpec((None, None, 2 * nblk, LANES), lambda b, g: (b, g, 0, 0))
    return pl.pallas_call(
        _cmp_kv_kernel,
        grid=(B, G),
        in_specs=[pl.BlockSpec((seq, LANES), lambda b, g: (b, g)),
                  full(pe2),
                  pl.BlockSpec((None, nblk, 1), lambda b, g: (b, 0, 0)),
                  pl.BlockSpec((1, LANES), lambda b, g: (0, 0)),
                  full(w1), full(wk2d), full(wv2d)],
        out_specs=[ospec, ospec],
        out_shape=[out, out],
        compiler_params=_cparams(("arbitrary", "arbitrary")),
        name="cmp_kv",
    )(kcv, pe2, posc, jnp.asarray(inv_a), w1, wk2d, wv2d)


def _build_cmp_w1(w_ck1, w_cv1):
    wk = w_ck1.reshape(CMP_BLOCK, A_HEAD_DIM, CMP_HIDDEN)
    wv = w_cv1.reshape(CMP_BLOCK, A_HEAD_DIM, CMP_HIDDEN)
    z = jnp.zeros_like(wk)
    return jnp.concatenate([jnp.concatenate([wk, z], axis=2),
                            jnp.concatenate([z, wv], axis=2)], axis=1).astype(BF16)


def _gate_expand_table(branch):
    e = np.zeros((LANES, A_REP * LANES), np.float32)
    for r in range(A_REP):
        e[GATE_LANE + 3 * r + branch, r * LANES:(r + 1) * LANES] = 1.0
    return e


def _branch_gates(gate_blk, e_ref):
    sg = jax.nn.sigmoid(gate_blk.astype(F32)).astype(BF16)
    return jnp.dot(sg, e_ref[...], preferred_element_type=F32)


def _overlap_table_t(seq):
    n_c = (seq - CMP_BLOCK) // CMP_STRIDE + 1
    n_s = seq // SEL_BLOCK
    cs = np.arange(n_c) * CMP_STRIDE
    ss = np.arange(n_s) * SEL_BLOCK
    ov = np.clip(np.minimum(cs[:, None] + CMP_BLOCK, ss[None, :] + SEL_BLOCK)
                 - np.maximum(cs[:, None], ss[None, :]), 0, None).astype(np.float32) / CMP_BLOCK
    full = np.zeros((LANES, LANES), np.float32)
    full[AUG_LANE:AUG_LANE + n_s, :n_c] = ov.T
    return full


def _split3(x):
    hi = x.astype(BF16)
    r1 = x - hi.astype(F32)
    mid = r1.astype(BF16)
    lo = (r1 - mid.astype(F32)).astype(BF16)
    return hi, mid, lo


def _cmp_attn_kernel(q_ref, kc2_ref, vc2_ref, ovt_ref, gate_ref, e_ref, ocmp_ref, qaug_ref, *, tq, n_sel_blocks):
    qi = pl.program_id(2)
    t = qi * tq + lax.broadcasted_iota(jnp.int32, (tq, LANES), 0)
    lane = lax.broadcasted_iota(jnp.int32, (tq, LANES), 1)
    cmask = (CMP_STRIDE * lane + (CMP_BLOCK - 1)) <= t
    cmask_f = cmask.astype(F32)
    nt = (((1,), (1,)), ((), ()))
    psum = jnp.zeros((tq, LANES), F32)
    gates = _branch_gates(gate_ref[...], e_ref)
    for pair in range(A_REP // 2):
        qp = q_ref[:, pair * LANES:(pair + 1) * LANES]
        s2 = lax.dot_general(qp, kc2_ref[...], nt, preferred_element_type=F32)
        probs = []
        for hh in range(2):
            s = jnp.where(cmask, s2[:, hh * LANES:(hh + 1) * LANES], NEG)
            mx = jnp.max(s, axis=-1, keepdims=True)
            e = jnp.exp2(s - mx) * cmask_f
            pr = e / jnp.maximum(jnp.sum(e, axis=-1, keepdims=True), 1.0)
            probs.append(pr)
            psum = psum + pr
        p2 = jnp.concatenate(probs, axis=1).astype(BF16)
        o_pair = jnp.dot(p2, vc2_ref[...], preferred_element_type=F32)
        gate = jnp.where(lane < A_HEAD_DIM, gates[:, (2 * pair) * LANES:(2 * pair + 1) * LANES],
                         gates[:, (2 * pair + 1) * LANES:(2 * pair + 2) * LANES])
        ocmp_ref[:, pair * LANES:(pair + 1) * LANES] = (gate * o_pair).astype(ocmp_ref.dtype)

    ovt = ovt_ref[...]
    imp_t = sum(lax.dot_general(ovt, part, nt, preferred_element_type=F32) for part in _split3(psum))
    imp_t = imp_t[AUG_LANE:AUG_LANE + n_sel_blocks, :]
    jb = lax.broadcasted_iota(jnp.int32, imp_t.shape, 0)
    tcol = qi * tq + lax.broadcasted_iota(jnp.int32, imp_t.shape, 1)
    cur = tcol // SEL_BLOCK
    valid = (SEL_BLOCK * jb) <= tcol
    forced = valid & ((jb == 0) | (jb == cur) | (jb == cur - 1))
    score = jnp.where(valid, imp_t + FORCE_BONUS * forced.astype(F32), NEG)
    rank = jnp.zeros(imp_t.shape, F32)
    for i in range(n_sel_blocks):
        other = score[i:i + 1, :]
        beats = (other > score) | ((other == score) & (jb > i))
        rank = rank + beats.astype(F32)
    selb = jnp.where(rank < float(min(SEL_TOPK, n_sel_blocks)), 0.0, NEG)
    aug_t = jnp.concatenate([jnp.zeros((AUG_LANE, tq), F32), selb,
                             jnp.zeros((LANES - AUG_LANE - n_sel_blocks, tq), F32)], axis=0)
    aug = aug_t.T.astype(BF16)
    lo = lane < A_HEAD_DIM
    for r in range(A_REP):
        qp = q_ref[:, (r // 2) * LANES:(r // 2 + 1) * LANES]
        if r % 2:
            qp = jnp.concatenate([qp[:, A_HEAD_DIM:], qp[:, :A_HEAD_DIM]], axis=1)
        qaug_ref[r] = jnp.where(lo, qp, aug)


def _cmp_attn(slab, kc2, vc2, B, seq):
    tq = T_ATT
    nq = seq // tq
    G = A_KV_GROUPS
    n_s = seq // SEL_BLOCK
    ovt = jnp.asarray(_overlap_table_t(seq), BF16)
    e = jnp.asarray(_gate_expand_table(0), BF16)
    kv_spec = pl.BlockSpec((None, None) + kc2.shape[2:], lambda b, g, q: (b, g, 0, 0))
    return pl.pallas_call(
        functools.partial(_cmp_attn_kernel, tq=tq, n_sel_blocks=n_s),
        grid=(B, G, nq),
        in_specs=[pl.BlockSpec((tq, A_REP * A_HEAD_DIM), lambda b, g, q: (b * nq + q, g)),
                  kv_spec, kv_spec,
                  pl.BlockSpec((LANES, LANES), lambda b, g, q: (0, 0)),
                  pl.BlockSpec((tq, LANES), lambda b, g, q: (b * nq + q, COL_KS // LANES + g)),
                  pl.BlockSpec(e.shape, lambda b, g, q: (0, 0))],
        out_specs=[pl.BlockSpec((tq, A_REP * A_HEAD_DIM), lambda b, g, q: (b * nq + q, g)),
                   pl.BlockSpec((None, None, A_REP, tq, LANES), lambda b, g, q: (b, g, 0, q, 0))],
        out_shape=[jax.ShapeDtypeStruct((B * seq, A_WIDTH), BF16),
                   jax.ShapeDtypeStruct((B, G, A_REP, seq, LANES), BF16)],
        compiler_params=_cparams(("arbitrary", "arbitrary", "arbitrary")),
        name="cmp_attn",
    )(slab, kc2, vc2, ovt, slab, e)


def _flash_init(m_scr, acc_scr, l_scr=None):
    m_scr[...] = jnp.full(m_scr.shape, M_INIT, F32)
    acc_scr[...] = jnp.zeros(acc_scr.shape, F32)
    if l_scr is not None:
        l_scr[...] = jnp.zeros(l_scr.shape, F32)


def _scores(q, k):
    return lax.dot_general(q, k, (((1,), (1,)), ((), ())), preferred_element_type=F32)


def _add_bias(s, bias, reps):
    tq, tk = bias.shape
    return (s.reshape(reps, tq, tk) + bias[None]).reshape(reps * tq, tk)


def _flash_update(s, v, m_scr, acc_scr, l_scr=None):
    chunks = [s[:, c * LANES:(c + 1) * LANES] for c in range(s.shape[1] // LANES)]
    mx = functools.reduce(jnp.maximum, chunks)
    m_prev = m_scr[...]
    m_new = jnp.maximum(m_prev, jnp.max(mx, axis=-1, keepdims=True))
    alpha = jnp.exp2(m_prev - m_new)
    if l_scr is None:
        p = jnp.concatenate([jnp.exp2((ch - m_new).astype(v.dtype)) for ch in chunks], axis=1)
    else:
        ps = [jnp.exp2(ch - m_new) for ch in chunks]
        p = jnp.concatenate(ps, axis=1).astype(v.dtype)
        l_scr[...] = alpha * l_scr[...] + functools.reduce(jnp.add, ps)
    acc_scr[...] = alpha * acc_scr[...] + jnp.dot(p, v, preferred_element_type=F32)
    m_scr[...] = m_new


def _gated_head(acc, gates, r):
    lane = lax.broadcasted_iota(jnp.int32, acc.shape, 1)
    scale = gates[:, r * LANES:(r + 1) * LANES] / acc[:, ONES_LANE:ONES_LANE + 1]
    return jnp.where(lane < A_HEAD_DIM, acc * scale, 0.0)


def _store_head_pair(o_ref, pair, even, odd):
    o_ref[:, pair * LANES:(pair + 1) * LANES] = (even + pltpu.roll(odd, A_HEAD_DIM, 1)).astype(o_ref.dtype)


def _win_bias_table(t, span, window):
    delta = np.arange(t)[:, None] - np.arange(span)[None, :]
    dist = np.stack([n * t + delta for n in range(span // t)])
    return np.where((dist >= 0) & (dist < window), 0.0, NEG).astype(np.float32)


def _window_heads(q_ref, kw_ref, vw_ref, bias_ref, gates, qi, t, span):
    start = pl.multiple_of(jnp.maximum((qi + 1) * t - span, 0), t)
    k = kw_ref[pl.ds(start, span), :]
    v = vw_ref[pl.ds(start, span), :]
    bias = bias_ref[jnp.minimum(qi, span // t - 1)]
    heads = []
    for r in range(A_REP):
        s = _scores(q_ref[r], k) + bias
        chunks = [s[:, c * LANES:(c + 1) * LANES] for c in range(span // LANES)]
        m = jnp.max(functools.reduce(jnp.maximum, chunks), axis=-1, keepdims=True)
        p = jnp.concatenate([jnp.exp2((ch - m).astype(BF16)) for ch in chunks], axis=1)
        acc = jnp.dot(p, v, preferred_element_type=F32)
        heads.append(_gated_head(acc, gates, r))
    return heads


def _nsa_kernel(q_ref, ks_ref, vs_ref, kw_ref, vw_ref, esel_ref, ewin_ref, bias_ref, osel_ref, owin_ref,
                m_scr, acc_scr, s0_scr, s1_scr, *, t, span):
    qi = pl.program_id(2)
    q = q_ref[...].reshape(A_REP * t, LANES)
    _flash_init(m_scr, acc_scr)
    rows_of = lambda ref, kj: ref[pl.ds(pl.multiple_of(kj * t, t), t), :]

    def score_into(buf, kj):
        buf[...] = _scores(q, rows_of(ks_ref, kj))

    def fold(buf, kj, causal=False):
        s = buf[...]
        if causal:
            delta = (lax.broadcasted_iota(jnp.int32, (t, t), 0)
                     - lax.broadcasted_iota(jnp.int32, (t, t), 1))
            s = _add_bias(s, jnp.where(delta >= 0, 0.0, NEG), A_REP)
        _flash_update(s, rows_of(vs_ref, kj), m_scr, acc_scr)

    first = qi % 2

    @pl.when(first == 1)
    def _():
        score_into(s1_scr, 0)
        fold(s1_scr, 0)

    score_into(s0_scr, first)

    def body(i, carry):
        kj = first + 2 * i
        score_into(s1_scr, kj + 1)
        fold(s0_scr, kj)
        score_into(s0_scr, kj + 2)
        fold(s1_scr, kj + 1)
        return carry
    lax.fori_loop(0, (qi - first) // 2, body, 0)

    gate_blk = rows_of(ks_ref, qi)
    win_heads = _window_heads(q_ref, kw_ref, vw_ref, bias_ref, _branch_gates(gate_blk, ewin_ref), qi, t, span)
    fold(s0_scr, qi, causal=True)
    gates = _branch_gates(gate_blk, esel_ref)
    sel_heads = [_gated_head(acc_scr[r * t:(r + 1) * t, :], gates, r) for r in range(A_REP)]
    for pair in range(A_REP // 2):
        _store_head_pair(owin_ref, pair, win_heads[2 * pair], win_heads[2 * pair + 1])
        _store_head_pair(osel_ref, pair, sel_heads[2 * pair], sel_heads[2 * pair + 1])


def _nsa_attend(qaug, slab, B, seq):
    t = T_ATT
    nq = seq // t
    G = A_KV_GROUPS
    rows = A_REP * t
    span = min(((WIN_SIZE - 1 + t - 1) // t + 1) * t, seq)
    esel = jnp.asarray(_gate_expand_table(1), BF16)
    ewin = jnp.asarray(_gate_expand_table(2), BF16)
    bias = jnp.asarray(_win_bias_table(t, span, WIN_SIZE))
    kv = lambda col: pl.BlockSpec((seq, LANES), lambda b, g, q: (b, col // LANES + g))
    const = lambda a: pl.BlockSpec(a.shape, lambda b, g, q: (0,) * a.ndim)
    out = jax.ShapeDtypeStruct((B * seq, A_WIDTH), BF16)
    ospec = pl.BlockSpec((t, A_REP * A_HEAD_DIM), lambda b, g, q: (b * nq + q, g))
    return pl.pallas_call(
        functools.partial(_nsa_kernel, t=t, span=span),
        grid=(B, G, nq),
        in_specs=[pl.BlockSpec((None, None, A_REP, t, LANES), lambda b, g, q: (b, g, 0, q, 0)),
                  kv(COL_KS), kv(COL_VS), kv(COL_KW), kv(COL_VW), const(esel), const(ewin), const(bias)],
        out_specs=[ospec, ospec],
        out_shape=[out, out],
        scratch_shapes=[pltpu.VMEM((rows, LANES), F32), pltpu.VMEM((rows, LANES), F32),
                        pltpu.VMEM((rows, t), F32), pltpu.VMEM((rows, t), F32)],
        compiler_params=_cparams(("arbitrary", "arbitrary", "arbitrary")),
        name="nsa_sel_win",
    )(qaug, slab, slab, slab, slab, esel, ewin, bias)


DILATIONS = ((128, 1), (512, 4), (2048, 16))
TQ_DIL = 512
TK_DIL = 256


def _dil_bias_tables(tq, tk, seq):
    w_near = max(w for w, r in DILATIONS if w < seq)
    near_keys = tq + -(-w_near // tk) * tk
    w_far, r_far = DILATIONS[-1]
    assert w_far >= seq and tq % r_far == 0 and tk % r_far == 0 and near_keys == 2 * tq
    assert tq % (2 * tk) == 0
    delta = np.arange(tq)[:, None] - np.arange(near_keys)[None, :]
    near = []
    for base in (near_keys - tq, 0):
        dist = base + delta
        mult = sum(((dist >= 0) & (dist <= w) & (dist % r == 0)).astype(np.float64) for w, r in DILATIONS)
        near.append(np.where(mult > 0, np.log2(np.maximum(mult, 1.0)), NEG))
    far = np.where(delta[:, :tk] % r_far == 0, 0.0, NEG)
    return np.stack(near).astype(np.float32), far.astype(np.float32)


def _dil_kernel(q_ref, k_ref, v_ref, near_ref, far_ref, o_ref, m_scr, acc_scr, l_scr, s0_scr, s1_scr,
                *, tq, tk):
    qi = pl.program_id(2)
    q = q_ref[...]
    near_keys = near_ref.shape[-1]
    _flash_init(m_scr, acc_scr, l_scr)
    update = functools.partial(_flash_update, m_scr=m_scr, acc_scr=acc_scr, l_scr=l_scr)
    rows_of = lambda ref, kj: ref[pl.ds(pl.multiple_of(kj * tk, tk), tk), :]
    clamped = (qi + 1) * tq < near_keys
    start = pl.multiple_of(jnp.maximum((qi + 1) * tq - near_keys, 0), tk)
    n_far = start // tk

    def score_into(buf, kj):
        buf[...] = _scores(q, rows_of(k_ref, kj))

    def fold(buf, kj):
        update(buf[...] + far_ref[...], rows_of(v_ref, kj))

    @pl.when(n_far > 0)
    def _():
        score_into(s0_scr, 0)

        def body(i, carry):
            kj = 2 * i
            score_into(s1_scr, kj + 1)
            fold(s0_scr, kj)
            score_into(s0_scr, kj + 2)
            fold(s1_scr, kj + 1)
            return carry
        lax.fori_loop(0, n_far // 2, body, 0)

    s = _scores(q, k_ref[pl.ds(start, near_keys), :]) + near_ref[clamped.astype(jnp.int32)]
    update(s, v_ref[pl.ds(start, near_keys), :])
    l = jnp.sum(l_scr[...], axis=-1, keepdims=True)
    o_ref[...] = (acc_scr[...] / l).astype(o_ref.dtype)


def _dil(slab, B, seq):
    tq, tk = TQ_DIL, TK_DIL
    nq = seq // tq
    H = B_HEADS
    near, far = (jnp.asarray(a) for a in _dil_bias_tables(tq, tk, seq))
    return pl.pallas_call(
        functools.partial(_dil_kernel, tq=tq, tk=tk),
        grid=(B, H, nq),
        in_specs=[pl.BlockSpec((tq, LANES), lambda b, h, q: (b * nq + q, COL_QB // LANES + h)),
                  pl.BlockSpec((seq, LANES), lambda b, h, q: (b, COL_KB // LANES + h)),
                  pl.BlockSpec((seq, LANES), lambda b, h, q: (b, COL_VB // LANES + h)),
                  pl.BlockSpec(near.shape, lambda b, h, q: (0, 0, 0)),
                  pl.BlockSpec(far.shape, lambda b, h, q: (0, 0))],
        out_specs=pl.BlockSpec((tq, LANES), lambda b, h, q: (b * nq + q, h)),
        out_shape=jax.ShapeDtypeStruct((B * seq, B_WIDTH), BF16),
        scratch_shapes=[pltpu.VMEM((tq, LANES), F32)] * 3 + [pltpu.VMEM((tq, tk), F32)] * 2,
        compiler_params=_cparams(("arbitrary", "arbitrary", "arbitrary")),
        name="dilated",
    )(slab, slab, slab, near, far)


def _silu(z):
    return z * jax.nn.sigmoid(z)


def _out_kernel(ocmp_ref, osel_ref, owin_ref, za_ref, ob_ref, zb_ref, ma_ref, mb_ref,
                x_ref, gate_ref, gpost_ref, wa_ref, wb_ref, wo_ref, o_ref):
    oa = ocmp_ref[...].astype(F32) + osel_ref[...].astype(F32) + owin_ref[...].astype(F32)
    a_in = (oa * _silu(za_ref[...].astype(F32))).astype(BF16)
    ya = jnp.dot(a_in, wa_ref[...], preferred_element_type=F32)
    b_in = (ob_ref[...].astype(F32) * _silu(zb_ref[...].astype(F32))).astype(BF16)
    yb = jnp.dot(b_in, wb_ref[...], preferred_element_type=F32)
    merged = (jax.nn.sigmoid(ma_ref[...].astype(F32)) * ya
              + jax.nn.sigmoid(mb_ref[...].astype(F32)) * yb)
    out = jnp.dot(merged.astype(BF16), wo_ref[...], preferred_element_type=F32)
    y = out * lax.rsqrt(jnp.mean(out * out, axis=-1, keepdims=True) + EPS) * gpost_ref[...]
    o_ref[...] = x_ref[...] + gate_ref[0] * y


def _out(ocmp, osel, owin, ob, slab, x2d, ada3, g_post, wa, wb, wo, seq):
    M, D = x2d.shape
    tm = 256
    per_b = seq // tm
    const = lambda a: pl.BlockSpec(a.shape, lambda i: (0,) * a.ndim, pipeline_mode=pl.Buffered(1))
    row = lambda w, cb: pl.BlockSpec((tm, w), lambda i: (i, cb))
    return pl.pallas_call(
        _out_kernel,
        grid=(M // tm,),
        in_specs=[row(A_WIDTH, 0), row(A_WIDTH, 0), row(A_WIDTH, 0),
                  row(A_WIDTH, COL_ZA // A_WIDTH), row(B_WIDTH, 0), row(B_WIDTH, COL_ZB // B_WIDTH),
                  row(D_MODEL, COL_MA // D_MODEL), row(D_MODEL, COL_MB // D_MODEL),
                  row(D, 0),
                  pl.BlockSpec((1, 1, D), lambda i: (i // per_b, 0, 2)),
                  pl.BlockSpec((1, D), lambda i: (0, 0)),
                  const(wa), const(wb), const(wo)],
        out_specs=pl.BlockSpec((tm, D), lambda i: (i, 0)),
        out_shape=jax.ShapeDtypeStruct((M, D), F32),
        compiler_params=_cparams(("arbitrary",)),
        name="out",
    )(ocmp, osel, owin, slab, ob, slab, slab, slab, x2d, ada3, g_post.reshape(1, D), wa, wb, wo)


def _layer(x, c, positions, w_ada, b_ada, g_pre, g_post, w_in, pe_ck, pe_cv, w_ck1, w_ck2,
           w_cv1, w_cv2, w_br_a, w_br_b, w_out):
    B, S, D = x.shape
    x2d = x.reshape(B * S, D)
    ada3 = _ada(c, w_ada, b_ada).reshape(B, 1, 3 * D)
    slab, kcv = _proj(x2d, ada3, g_pre, positions.reshape(B * S, 1), _build_w_slab_t(jnp.swapaxes(w_in, 0, 1)), S)

    nblk = S // CMP_STRIDE
    assert (S - CMP_BLOCK) // CMP_STRIDE + 1 == nblk - 1
    cmp_end = np.minimum(np.arange(nblk) * CMP_STRIDE + CMP_BLOCK - 1, S - 1)
    posc = positions[:, cmp_end].reshape(B, nblk, 1)
    dup = lambda w: jnp.concatenate([w, w], axis=1).astype(BF16)
    pe2 = jnp.concatenate([pe_ck, pe_cv], axis=1)
    kc2, vc2 = _cmp_kv(kcv, pe2, posc, _build_cmp_w1(w_ck1, w_cv1), dup(w_ck2), dup(w_cv2), B, S)

    ocmp, qaug = _cmp_attn(slab, kc2, vc2, B, S)
    osel, owin = _nsa_attend(qaug, slab, B, S)
    ob = _dil(slab, B, S)
    out = _out(ocmp, osel, owin, ob, slab, x2d, ada3, g_post,
               w_br_a.astype(BF16), w_br_b.astype(BF16), w_out.astype(BF16), S)
    return out.reshape(B, S, D)


def kernel(x, c, positions, w_ada, b_ada, g_pre, g_post, w_in, pe_ck, pe_cv, w_ck1, w_ck2, w_cv1, w_cv2, w_br_a, w_br_b, w_out):
    h = x
    for layer in range(w_ada.shape[0]):
        h = _layer(h, c, positions, w_ada[layer], b_ada[layer], g_pre[layer], g_post[layer],
                   w_in[layer], pe_ck[layer], pe_cv[layer], w_ck1[layer], w_ck2[layer],
                   w_cv1[layer], w_cv2[layer], w_br_a[layer], w_br_b[layer], w_out[layer])
    return h
```

```python
import functools

import numpy as np
import jax
import jax.numpy as jnp
from jax import lax
from jax.experimental import pallas as pl
from jax.experimental.pallas import tpu as pltpu

F32 = jnp.float32
BF16 = jnp.bfloat16

D_MODEL = 2048
A_HEADS = 16
A_HEAD_DIM = 64
A_KV_GROUPS = 4
A_REP = A_HEADS // A_KV_GROUPS
A_WIDTH = A_HEADS * A_HEAD_DIM
A_KV_WIDTH = A_KV_GROUPS * A_HEAD_DIM
CMP_BLOCK = 32
CMP_STRIDE = 16
CMP_HIDDEN = 4 * A_HEAD_DIM
SEL_BLOCK = 64
SEL_TOPK = 16
WIN_SIZE = 512
FORCE_BONUS = 1.0e4
B_HEADS = 8
B_HEAD_DIM = 128
B_WIDTH = B_HEADS * B_HEAD_DIM
ROPE_THETA = 500000.0
EPS = 1e-6
NEG = -1e30
M_INIT = -1.0e38
LOG2E = 1.4426950408889634

LANES = 128
MXU_N = 256
VMEM_LIMIT = 56 * 1024 * 1024

TN = 512
COL_QA = 0
COL_KS = 1024
COL_KW = 1536
COL_QB = 2048
COL_KB = 3072
COL_VB = 4096
COL_ZA = 5120
COL_ZB = 6144
COL_VS = 7168
COL_VW = 7680
COL_MA = 8192
COL_MB = 10240
N_SLAB = 12288
COL_KCV = N_SLAB
N_PROJ = N_SLAB + 2 * A_KV_WIDTH
GATE_LANE = 96
AUG_LANE = 64
ONES_LANE = 64

T_ATT = 256


def _cparams(sem):
    return pltpu.CompilerParams(dimension_semantics=sem, vmem_limit_bytes=VMEM_LIMIT)


def _ada_kernel(c_ref, w_ref, b_ref, o_ref):
    o_ref[...] = jnp.dot(c_ref[...], w_ref[...], preferred_element_type=F32,
                         precision=lax.Precision.HIGHEST) + b_ref[...]


def _ada(c, w_ada, b_ada):
    B, D = c.shape
    N = w_ada.shape[1]
    tn = 768
    return pl.pallas_call(
        _ada_kernel,
        grid=(N // tn,),
        in_specs=[pl.BlockSpec((B, D), lambda j: (0, 0)),
                  pl.BlockSpec((D, tn), lambda j: (0, j)),
                  pl.BlockSpec((1, tn), lambda j: (0, j))],
        out_specs=pl.BlockSpec((B, tn), lambda j: (0, j)),
        out_shape=jax.ShapeDtypeStruct((B, N), F32),
        compiler_params=_cparams(("arbitrary",)),
        name="ada",
    )(c, w_ada, b_ada.reshape(1, N))


def _rope_inv_tables():
    lane = np.arange(LANES)
    inv_a8 = (ROPE_THETA ** (-np.arange(0, 16, 2) / 16)).astype(np.float32)
    inv_b16 = (ROPE_THETA ** (-np.arange(0, 32, 2) / 32)).astype(np.float32)
    la = lane % A_HEAD_DIM
    inv_a = np.where(la < 16, inv_a8[la % 8], 0.0).astype(np.float32)
    inv_b = np.where(lane < 32, inv_b16[lane % 16], 0.0).astype(np.float32)
    return inv_a.reshape(1, LANES), inv_b.reshape(1, LANES)


def _rope_tables(pos_f, inv, period, half):
    ang = pos_f * inv
    c, s = jnp.cos(ang), jnp.sin(ang)
    lane = lax.broadcasted_iota(jnp.int32, ang.shape, 1) % period
    cos_t = jnp.where(lane < 2 * half, c, 1.0)
    sin_t = jnp.where(lane < half, -s, jnp.where(lane < 2 * half, s, 0.0))
    return cos_t, sin_t


def _rope_apply(x, cos_t, sin_t, half):
    lane = lax.broadcasted_iota(jnp.int32, x.shape, 1)
    partner = jnp.take_along_axis(x, lane ^ half, axis=1)
    return x * cos_t + partner * sin_t


def _proj_kernel(x_ref, shift_ref, scale_ref, gpre_ref, pos_ref, inva_ref, invb_ref, w_ref,
                 o_ref, kcv_ref, h_scr, ca, sa, cb, sb, *, tm, seq):
    i = pl.program_id(0)
    j = pl.program_id(1)

    @pl.when(j == 0)
    def _():
        x = x_ref[...]
        ms = jnp.mean(x * x, axis=-1, keepdims=True)
        y = x * lax.rsqrt(ms + EPS) * gpre_ref[...]
        h = y * (1.0 + scale_ref[0]) + shift_ref[0]
        h_scr[...] = h.astype(BF16)
        pos_f = pos_ref[...].astype(F32)
        ca[...], sa[...] = _rope_tables(pos_f, inva_ref[...], A_HEAD_DIM, 8)
        cb[...], sb[...] = _rope_tables(pos_f, invb_ref[...], LANES, 16)

    def tile(epilogue):
        for c0 in range(0, TN, MXU_N):
            acc = lax.dot_general(h_scr[...], w_ref[c0:c0 + MXU_N, :], (((1,), (1,)), ((), ())),
                                  preferred_element_type=F32)
            epilogue(acc, c0)

    def rope_epilogue(tabs, half, mul, first_head_only, onehot):
        def epilogue(acc, c0):
            cos_t, sin_t = tabs[0][...], tabs[1][...]
            lane = lax.broadcasted_iota(jnp.int32, (tm, LANES), 1)
            if first_head_only:
                keep = lane < A_HEAD_DIM
                cos_t = jnp.where(keep, cos_t, 1.0)
                sin_t = jnp.where(keep, sin_t, 0.0)
            if onehot:
                t = (i * tm + lax.broadcasted_iota(jnp.int32, (tm, LANES), 0)) % seq
                hot = (lane >= AUG_LANE) & (lane < AUG_LANE + seq // SEL_BLOCK) & \
                      ((t // SEL_BLOCK) == (lane - AUG_LANE))
            for cidx in range(MXU_N // LANES):
                xc = acc[:, cidx * LANES:(cidx + 1) * LANES]
                r = _rope_apply(xc, cos_t, sin_t, half)
                if mul != 1.0:
                    r = r * mul
                if onehot:
                    r = jnp.where(hot, 1.0, r)
                o_ref[:, c0 + cidx * LANES:c0 + (cidx + 1) * LANES] = r.astype(o_ref.dtype)
        return epilogue

    def plain_epilogue(acc, c0):
        o_ref[:, c0:c0 + MXU_N] = acc.astype(o_ref.dtype)

    def ones_lane_epilogue(acc, c0):
        lane = lax.broadcasted_iota(jnp.int32, acc.shape, 1) % LANES
        o_ref[:, c0:c0 + MXU_N] = jnp.where(lane == ONES_LANE, 1.0, acc).astype(o_ref.dtype)

    def kcv_epilogue(acc, c0):
        kcv_ref[:, c0:c0 + MXU_N] = acc

    ta = (ca, sa)
    tb = (cb, sb)
    is_vpad = (j == COL_VS // TN) | (j == COL_VW // TN)
    pl.when(j < COL_KS // TN)(
        lambda: tile(rope_epilogue(ta, 8, A_HEAD_DIM ** -0.5 * LOG2E, False, False)))
    pl.when(j == COL_KS // TN)(lambda: tile(rope_epilogue(ta, 8, 1.0, True, True)))
    pl.when(j == COL_KW // TN)(lambda: tile(rope_epilogue(ta, 8, 1.0, True, False)))
    pl.when((j >= COL_QB // TN) & (j < COL_KB // TN))(
        lambda: tile(rope_epilogue(tb, 16, B_HEAD_DIM ** -0.5 * LOG2E, False, False)))
    pl.when((j >= COL_KB // TN) & (j < COL_VB // TN))(
        lambda: tile(rope_epilogue(tb, 16, 1.0, False, False)))
    pl.when((j >= COL_VB // TN) & (j < COL_KCV // TN) & jnp.logical_not(is_vpad))(
        lambda: tile(plain_epilogue))
    pl.when(is_vpad)(lambda: tile(ones_lane_epilogue))
    pl.when(j == COL_KCV // TN)(lambda: tile(kcv_epilogue))


def _proj(x2d, ada3, g_pre, pos2d, w_slab_t, seq):
    M, D = x2d.shape
    tm = 1024
    per_b = seq // tm
    inv_a, inv_b = _rope_inv_tables()
    tab = pltpu.VMEM((tm, LANES), F32)
    last_slab_tile = N_SLAB // TN - 1
    return pl.pallas_call(
        functools.partial(_proj_kernel, tm=tm, seq=seq),
        grid=(M // tm, N_PROJ // TN),
        in_specs=[pl.BlockSpec((tm, D), lambda i, j: (i, 0)),
                  pl.BlockSpec((1, 1, D), lambda i, j: (i // per_b, 0, 0)),
                  pl.BlockSpec((1, 1, D), lambda i, j: (i // per_b, 0, 1)),
                  pl.BlockSpec((1, D), lambda i, j: (0, 0)),
                  pl.BlockSpec((tm, 1), lambda i, j: (i, 0)),
                  pl.BlockSpec((1, LANES), lambda i, j: (0, 0)),
                  pl.BlockSpec((1, LANES), lambda i, j: (0, 0)),
                  pl.BlockSpec((TN, D), lambda i, j: (j, 0))],
        out_specs=[pl.BlockSpec((tm, TN), lambda i, j: (i, jnp.minimum(j, last_slab_tile))),
                   pl.BlockSpec((tm, TN), lambda i, j: (i, 0))],
        out_shape=[jax.ShapeDtypeStruct((M, N_SLAB), BF16),
                   jax.ShapeDtypeStruct((M, N_PROJ - N_SLAB), F32)],
        scratch_shapes=[pltpu.VMEM((tm, D), BF16), tab, tab, tab, tab],
        compiler_params=_cparams(("arbitrary", "arbitrary")),
        name="proj",
    )(x2d, ada3, ada3, g_pre.reshape(1, D), pos2d, jnp.asarray(inv_a), jnp.asarray(inv_b), w_slab_t)


_SRC_SIZES = (A_WIDTH,) + (A_KV_WIDTH,) * 6 + (A_HEADS * 3, A_WIDTH) + (B_WIDTH,) * 4 + (D_MODEL, D_MODEL)
_SRC = [int(v) for v in np.concatenate([[0], np.cumsum(_SRC_SIZES)])]
(SRC_QA, SRC_KC, SRC_VC, SRC_KS, SRC_VS, SRC_KW, SRC_VW, SRC_GA, SRC_ZA, SRC_QB, SRC_KB, SRC_VB,
 SRC_ZB, SRC_MA, SRC_MB, SRC_END) = _SRC


def _w_slab_kernel(w_ref, o_ref):
    cols = w_ref.shape[1]

    def put(dst, src, width):
        o_ref[dst:dst + width, :] = w_ref[src:src + width, :].astype(o_ref.dtype)

    def clear(dst, width):
        o_ref[dst:dst + width, :] = jnp.zeros((width, cols), o_ref.dtype)

    put(COL_QA, SRC_QA, A_WIDTH)
    put(COL_QB, SRC_QB, B_WIDTH)
    put(COL_KB, SRC_KB, B_WIDTH)
    put(COL_VB, SRC_VB, B_WIDTH)
    put(COL_ZA, SRC_ZA, A_WIDTH)
    put(COL_ZB, SRC_ZB, B_WIDTH)
    put(COL_MA, SRC_MA, D_MODEL)
    put(COL_MB, SRC_MB, D_MODEL)
    ngate = A_REP * 3
    for g in range(A_KV_GROUPS):
        d = g * LANES
        s = g * A_HEAD_DIM
        put(COL_KS + d, SRC_KS + s, A_HEAD_DIM)
        clear(COL_KS + d + A_HEAD_DIM, GATE_LANE - A_HEAD_DIM)
        put(COL_KS + d + GATE_LANE, SRC_GA + g * ngate, ngate)
        clear(COL_KS + d + GATE_LANE + ngate, LANES - GATE_LANE - ngate)
        for col, src in ((COL_KW, SRC_KW), (COL_VS, SRC_VS), (COL_VW, SRC_VW)):
            put(col + d, src + s, A_HEAD_DIM)
            clear(col + d + A_HEAD_DIM, LANES - A_HEAD_DIM)
        put(COL_KCV + d, SRC_KC + s, A_HEAD_DIM)
        put(COL_KCV + d + A_HEAD_DIM, SRC_VC + s, A_HEAD_DIM)


def _build_w_slab_t(w_in_t):
    n_in, D = w_in_t.shape
    assert n_in == SRC_END
    cols = 256
    return pl.pallas_call(
        _w_slab_kernel,
        grid=(D // cols,),
        in_specs=[pl.BlockSpec((n_in, cols), lambda i: (0, i))],
        out_specs=pl.BlockSpec((N_PROJ, cols), lambda i: (0, i)),
        out_shape=jax.ShapeDtypeStruct((N_PROJ, D), BF16),
        compiler_params=_cparams(("arbitrary",)),
        name="w_slab",
    )(w_in_t)


def _cmp_kv_kernel(t_ref, pe_ref, posc_ref, inva_ref, w1_ref, wk2_ref, wv2_ref, kc2_ref, vc2_ref):
    nblk = t_ref.shape[0] // CMP_STRIDE
    hidden2 = w1_ref.shape[-1]
    p_lo = jnp.zeros((nblk, hidden2), F32)
    p_hi = jnp.zeros((nblk, hidden2), F32)
    for l in range(CMP_STRIDE):
        x = t_ref[pl.ds(l, nblk, stride=CMP_STRIDE), :]
        p_lo += jnp.dot((x + pe_ref[l:l + 1, :]).astype(BF16), w1_ref[l],
                        preferred_element_type=F32)
        p_hi += jnp.dot((x + pe_ref[CMP_STRIDE + l:CMP_STRIDE + l + 1, :]).astype(BF16),
                        w1_ref[CMP_STRIDE + l], preferred_element_type=F32)
    hid = p_lo + pltpu.roll(p_hi, nblk - 1, 0)
    act = jax.nn.gelu(hid).astype(BF16)
    kk = jnp.dot(act[:, :CMP_HIDDEN], wk2_ref[...], preferred_element_type=F32)
    vv = jnp.dot(act[:, CMP_HIDDEN:], wv2_ref[...], preferred_element_type=F32)
    pos_f = posc_ref[...].astype(F32)
    cos_t, sin_t = _rope_tables(pos_f, inva_ref[...], A_HEAD_DIM, 8)
    kk = _rope_apply(kk, cos_t, sin_t, 8)
    lane = lax.broadcasted_iota(jnp.int32, kk.shape, 1)
    lo = lane < A_HEAD_DIM
    kc2_ref[0:nblk, :] = jnp.where(lo, kk, 0.0).astype(kc2_ref.dtype)
    kc2_ref[nblk:2 * nblk, :] = jnp.where(lo, 0.0, kk).astype(kc2_ref.dtype)
    vc2_ref[0:nblk, :] = jnp.where(lo, vv, 0.0).astype(vc2_ref.dtype)
    vc2_ref[nblk:2 * nblk, :] = jnp.where(lo, 0.0, vv).astype(vc2_ref.dtype)


def _cmp_kv(kcv, pe2, posc, w1, wk2d, wv2d, B, seq):
    G = A_KV_GROUPS
    nblk = seq // CMP_STRIDE
    inv_a, _ = _rope_inv_tables()
    full = lambda a: pl.BlockSpec(a.shape, lambda b, g: (0,) * a.ndim)
    out = jax.ShapeDtypeStruct((B, G, 2 * nblk, LANES), BF16)
    ospec = pl.BlockSpec((None, None, 2 * nblk, LANES), lambda b, g: (b, g, 0, 0))
    return pl.pallas_call(
        _cmp_kv_kernel,
        grid=(B, G),
        in_specs=[pl.BlockSpec((seq, LANES), lambda b, g: (b, g)),
                  full(pe2),
                  pl.BlockSpec((None, nblk, 1), lambda b, g: (b, 0, 0)),
                  pl.BlockSpec((1, LANES), lambda b, g: (0, 0)),
                  full(w1), full(wk2d), full(wv2d)],
        out_specs=[ospec, ospec],
        out_shape=[out, out],
        compiler_params=_cparams(("arbitrary", "arbitrary")),
        name="cmp_kv",
    )(kcv, pe2, posc, jnp.asarray(inv_a), w1, wk2d, wv2d)


def _build_cmp_w1(w_ck1, w_cv1):
    wk = w_ck1.reshape(CMP_BLOCK, A_HEAD_DIM, CMP_HIDDEN)
    wv = w_cv1.reshape(CMP_BLOCK, A_HEAD_DIM, CMP_HIDDEN)
    z = jnp.zeros_like(wk)
    return jnp.concatenate([jnp.concatenate([wk, z], axis=2),
                            jnp.concatenate([z, wv], axis=2)], axis=1).astype(BF16)


def _gate_expand_table(branch):
    e = np.zeros((LANES, A_REP * LANES), np.float32)
    for r in range(A_REP):
        e[GATE_LANE + 3 * r + branch, r * LANES:(r + 1) * LANES] = 1.0
    return e


def _branch_gates(gate_blk, e_ref):
    sg = jax.nn.sigmoid(gate_blk.astype(F32)).astype(BF16)
    return jnp.dot(sg, e_ref[...], preferred_element_type=F32)


def _overlap_table_t(seq):
    n_c = (seq - CMP_BLOCK) // CMP_STRIDE + 1
    n_s = seq // SEL_BLOCK
    cs = np.arange(n_c) * CMP_STRIDE
    ss = np.arange(n_s) * SEL_BLOCK
    ov = np.clip(np.minimum(cs[:, None] + CMP_BLOCK, ss[None, :] + SEL_BLOCK)
                 - np.maximum(cs[:, None], ss[None, :]), 0, None).astype(np.float32) / CMP_BLOCK
    full = np.zeros((LANES, LANES), np.float32)
    full[AUG_LANE:AUG_LANE + n_s, :n_c] = ov.T
    return full


def _split3(x):
    hi = x.astype(BF16)
    r1 = x - hi.astype(F32)
    mid = r1.astype(BF16)
    lo = (r1 - mid.astype(F32)).astype(BF16)
    return hi, mid, lo


def _compressed_branch(q_ref, kc2_ref, vc2_ref, ovt_ref, gates, ocmp_ref, qi, tq, n_sel_blocks):
    t = qi * tq + lax.broadcasted_iota(jnp.int32, (tq, LANES), 0)
    lane = lax.broadcasted_iota(jnp.int32, (tq, LANES), 1)
    cmask = (CMP_STRIDE * lane + (CMP_BLOCK - 1)) <= t
    cmask_f = cmask.astype(F32)
    nt = (((1,), (1,)), ((), ()))
    psum = jnp.zeros((tq, LANES), F32)
    for pair in range(A_REP // 2):
        qp = q_ref[:, pair * LANES:(pair + 1) * LANES]
        s2 = lax.dot_general(qp, kc2_ref[...], nt, preferred_element_type=F32)
        probs = []
        for hh in range(2):
            s = jnp.where(cmask, s2[:, hh * LANES:(hh + 1) * LANES], NEG)
            mx = jnp.max(s, axis=-1, keepdims=True)
            e = jnp.exp2(s - mx) * cmask_f
            pr = e / jnp.maximum(jnp.sum(e, axis=-1, keepdims=True), 1.0)
            probs.append(pr)
            psum = psum + pr
        p2 = jnp.concatenate(probs, axis=1).astype(BF16)
        o_pair = jnp.dot(p2, vc2_ref[...], preferred_element_type=F32)
        gate = jnp.where(lane < A_HEAD_DIM, gates[:, (2 * pair) * LANES:(2 * pair + 1) * LANES],
                         gates[:, (2 * pair + 1) * LANES:(2 * pair + 2) * LANES])
        ocmp_ref[:, pair * LANES:(pair + 1) * LANES] = (gate * o_pair).astype(ocmp_ref.dtype)

    ovt = ovt_ref[...]
    imp_t = sum(lax.dot_general(ovt, part, nt, preferred_element_type=F32) for part in _split3(psum))
    imp_t = imp_t[AUG_LANE:AUG_LANE + n_sel_blocks, :]
    jb = lax.broadcasted_iota(jnp.int32, imp_t.shape, 0)
    tcol = qi * tq + lax.broadcasted_iota(jnp.int32, imp_t.shape, 1)
    cur = tcol // SEL_BLOCK
    valid = (SEL_BLOCK * jb) <= tcol
    forced = valid & ((jb == 0) | (jb == cur) | (jb == cur - 1))
    score = jnp.where(valid, imp_t + FORCE_BONUS * forced.astype(F32), NEG)
    rank = jnp.zeros(imp_t.shape, F32)
    for i in range(n_sel_blocks):
        other = score[i:i + 1, :]
        beats = (other > score) | ((other == score) & (jb > i))
        rank = rank + beats.astype(F32)
    selb = jnp.where(rank < float(min(SEL_TOPK, n_sel_blocks)), 0.0, NEG)
    aug_t = jnp.concatenate([jnp.zeros((AUG_LANE, tq), F32), selb,
                             jnp.zeros((LANES - AUG_LANE - n_sel_blocks, tq), F32)], axis=0)
    return aug_t.T.astype(BF16)


def _flash_init(m_scr, acc_scr, l_scr=None):
    m_scr[...] = jnp.full(m_scr.shape, M_INIT, F32)
    acc_scr[...] = jnp.zeros(acc_scr.shape, F32)
    if l_scr is not None:
        l_scr[...] = jnp.zeros(l_scr.shape, F32)


def _scores(q, k):
    return lax.dot_general(q, k, (((1,), (1,)), ((), ())), preferred_element_type=F32)


def _add_bias(s, bias, reps):
    tq, tk = bias.shape
    return (s.reshape(reps, tq, tk) + bias[None]).reshape(reps * tq, tk)


def _flash_update(s, v, m_scr, acc_scr, l_scr=None):
    chunks = [s[:, c * LANES:(c + 1) * LANES] for c in range(s.shape[1] // LANES)]
    mx = functools.reduce(jnp.maximum, chunks)
    m_prev = m_scr[...]
    m_new = jnp.maximum(m_prev, jnp.max(mx, axis=-1, keepdims=True))
    alpha = jnp.exp2(m_prev - m_new)
    if l_scr is None:
        p = jnp.concatenate([jnp.exp2((ch - m_new).astype(v.dtype)) for ch in chunks], axis=1)
    else:
        ps = [jnp.exp2(ch - m_new) for ch in chunks]
        p = jnp.concatenate(ps, axis=1).astype(v.dtype)
        l_scr[...] = alpha * l_scr[...] + functools.reduce(jnp.add, ps)
    acc_scr[...] = alpha * acc_scr[...] + jnp.dot(p, v, preferred_element_type=F32)
    m_scr[...] = m_new


def _gated_head(acc, gates, r):
    lane = lax.broadcasted_iota(jnp.int32, acc.shape, 1)
    scale = gates[:, r * LANES:(r + 1) * LANES] / acc[:, ONES_LANE:ONES_LANE + 1]
    return jnp.where(lane < A_HEAD_DIM, acc * scale, 0.0)


def _store_head_pair(o_ref, pair, even, odd):
    o_ref[:, pair * LANES:(pair + 1) * LANES] = (even + pltpu.roll(odd, A_HEAD_DIM, 1)).astype(o_ref.dtype)


def _win_bias_table(t, span, window):
    delta = np.arange(t)[:, None] - np.arange(span)[None, :]
    dist = np.stack([n * t + delta for n in range(span // t)])
    return np.where((dist >= 0) & (dist < window), 0.0, NEG).astype(np.float32)


def _window_heads(q_heads, kw_ref, vw_ref, bias_ref, gates, qi, t, span):
    start = pl.multiple_of(jnp.maximum((qi + 1) * t - span, 0), t)
    k = kw_ref[pl.ds(start, span), :]
    v = vw_ref[pl.ds(start, span), :]
    bias = bias_ref[jnp.minimum(qi, span // t - 1)]
    heads = []
    for r in range(A_REP):
        s = _scores(q_heads[r], k) + bias
        chunks = [s[:, c * LANES:(c + 1) * LANES] for c in range(span // LANES)]
        m = jnp.max(functools.reduce(jnp.maximum, chunks), axis=-1, keepdims=True)
        p = jnp.concatenate([jnp.exp2((ch - m).astype(BF16)) for ch in chunks], axis=1)
        acc = jnp.dot(p, v, preferred_element_type=F32)
        heads.append(_gated_head(acc, gates, r))
    return heads


def _nsa_kernel(q_ref, kc2_ref, vc2_ref, ovt_ref, ks_ref, vs_ref, kw_ref, vw_ref, ecmp_ref, esel_ref,
                ewin_ref, bias_ref, ocmp_ref, osel_ref, owin_ref, qaug_scr, m_scr, acc_scr, s0_scr, s1_scr,
                *, t, span, n_sel_blocks):
    qi = pl.program_id(2)
    rows_of = lambda ref, kj: ref[pl.ds(pl.multiple_of(kj * t, t), t), :]
    gate_blk = rows_of(ks_ref, qi)
    lo = lax.broadcasted_iota(jnp.int32, (t, LANES), 1) < A_HEAD_DIM
    q_heads = []
    for r in range(A_REP):
        qp = q_ref[:, (r // 2) * LANES:(r // 2 + 1) * LANES]
        if r % 2:
            qp = jnp.concatenate([qp[:, A_HEAD_DIM:], qp[:, :A_HEAD_DIM]], axis=1)
        q_heads.append(qp)

    win_heads = _window_heads([jnp.where(lo, qh, jnp.zeros_like(qh)) for qh in q_heads], kw_ref, vw_ref,
                              bias_ref, _branch_gates(gate_blk, ewin_ref), qi, t, span)
    aug = _compressed_branch(q_ref, kc2_ref, vc2_ref, ovt_ref, _branch_gates(gate_blk, ecmp_ref),
                             ocmp_ref, qi, t, n_sel_blocks)
    for r in range(A_REP):
        qaug_scr[r] = jnp.where(lo, q_heads[r], aug)
    for pair in range(A_REP // 2):
        _store_head_pair(owin_ref, pair, win_heads[2 * pair], win_heads[2 * pair + 1])

    q = qaug_scr[...].reshape(A_REP * t, LANES)
    _flash_init(m_scr, acc_scr)

    def score_into(buf, kj):
        buf[...] = _scores(q, rows_of(ks_ref, kj))

    def fold(buf, kj, causal=False):
        s = buf[...]
        if causal:
            delta = (lax.broadcasted_iota(jnp.int32, (t, t), 0)
                     - lax.broadcasted_iota(jnp.int32, (t, t), 1))
            s = _add_bias(s, jnp.where(delta >= 0, 0.0, NEG), A_REP)
        _flash_update(s, rows_of(vs_ref, kj), m_scr, acc_scr)

    first = qi % 2

    @pl.when(first == 1)
    def _():
        score_into(s1_scr, 0)
        fold(s1_scr, 0)

    score_into(s0_scr, first)

    def body(i, carry):
        kj = first + 2 * i
        score_into(s1_scr, kj + 1)
        fold(s0_scr, kj)
        score_into(s0_scr, kj + 2)
        fold(s1_scr, kj + 1)
        return carry
    lax.fori_loop(0, (qi - first) // 2, body, 0)

    fold(s0_scr, qi, causal=True)
    gates = _branch_gates(gate_blk, esel_ref)
    sel_heads = [_gated_head(acc_scr[r * t:(r + 1) * t, :], gates, r) for r in range(A_REP)]
    for pair in range(A_REP // 2):
        _store_head_pair(osel_ref, pair, sel_heads[2 * pair], sel_heads[2 * pair + 1])


def _nsa_attend(slab, kc2, vc2, B, seq):
    t = T_ATT
    nq = seq // t
    G = A_KV_GROUPS
    rows = A_REP * t
    span = min(((WIN_SIZE - 1 + t - 1) // t + 1) * t, seq)
    ovt = jnp.asarray(_overlap_table_t(seq), BF16)
    ecmp, esel, ewin = (jnp.asarray(_gate_expand_table(br), BF16) for br in range(3))
    bias = jnp.asarray(_win_bias_table(t, span, WIN_SIZE))
    kv = lambda col: pl.BlockSpec((seq, LANES), lambda b, g, q: (b, col // LANES + g))
    const = lambda a: pl.BlockSpec(a.shape, lambda b, g, q: (0,) * a.ndim)
    ckv = pl.BlockSpec((None, None) + kc2.shape[2:], lambda b, g, q: (b, g, 0, 0))
    out = jax.ShapeDtypeStruct((B * seq, A_WIDTH), BF16)
    ospec = pl.BlockSpec((t, A_REP * A_HEAD_DIM), lambda b, g, q: (b * nq + q, g))
    return pl.pallas_call(
        functools.partial(_nsa_kernel, t=t, span=span, n_sel_blocks=seq // SEL_BLOCK),
        grid=(B, G, nq),
        in_specs=[pl.BlockSpec((t, A_REP * A_HEAD_DIM), lambda b, g, q: (b * nq + q, g)),
                  ckv, ckv, const(ovt), kv(COL_KS), kv(COL_VS), kv(COL_KW), kv(COL_VW),
                  const(ecmp), const(esel), const(ewin), const(bias)],
        out_specs=[ospec, ospec, ospec],
        out_shape=[out, out, out],
        scratch_shapes=[pltpu.VMEM((A_REP, t, LANES), BF16),
                        pltpu.VMEM((rows, LANES), F32), pltpu.VMEM((rows, LANES), F32),
                        pltpu.VMEM((rows, t), F32), pltpu.VMEM((rows, t), F32)],
        compiler_params=_cparams(("arbitrary", "arbitrary", "arbitrary")),
        name="nsa",
    )(slab, kc2, vc2, ovt, slab, slab, slab, slab, ecmp, esel, ewin, bias)


DILATIONS = ((128, 1), (512, 4), (2048, 16))
TQ_DIL = 512
TK_DIL = 256


def _dil_bias_tables(tq, tk, seq):
    w_near = max(w for w, r in DILATIONS if w < seq)
    near_keys = tq + -(-w_near // tk) * tk
    w_far, r_far = DILATIONS[-1]
    assert w_far >= seq and tq % r_far == 0 and tk % r_far == 0 and near_keys == 2 * tq
    assert tq % (2 * tk) == 0
    delta = np.arange(tq)[:, None] - np.arange(near_keys)[None, :]
    near = []
    for base in (near_keys - tq, 0):
        dist = base + delta
        mult = sum(((dist >= 0) & (dist <= w) & (dist % r == 0)).astype(np.float64) for w, r in DILATIONS)
        near.append(np.where(mult > 0, np.log2(np.maximum(mult, 1.0)), NEG))
    far = np.where(delta[:, :tk] % r_far == 0, 0.0, NEG)
    return np.stack(near).astype(np.float32), far.astype(np.float32)


def _dil_kernel(q_ref, k_ref, v_ref, near_ref, far_ref, o_ref, m_scr, acc_scr, l_scr, s0_scr, s1_scr,
                *, tq, tk):
    qi = pl.program_id(2)
    q = q_ref[...]
    near_keys = near_ref.shape[-1]
    _flash_init(m_scr, acc_scr, l_scr)
    update = functools.partial(_flash_update, m_scr=m_scr, acc_scr=acc_scr, l_scr=l_scr)
    rows_of = lambda ref, kj: ref[pl.ds(pl.multiple_of(kj * tk, tk), tk), :]
    clamped = (qi + 1) * tq < near_keys
    start = pl.multiple_of(jnp.maximum((qi + 1) * tq - near_keys, 0), tk)
    n_far = start // tk

    def score_into(buf, kj):
        buf[...] = _scores(q, rows_of(k_ref, kj))

    def fold(buf, kj):
        update(buf[...] + far_ref[...], rows_of(v_ref, kj))

    @pl.when(n_far > 0)
    def _():
        score_into(s0_scr, 0)

        def body(i, carry):
            kj = 2 * i
            score_into(s1_scr, kj + 1)
            fold(s0_scr, kj)
            score_into(s0_scr, kj + 2)
            fold(s1_scr, kj + 1)
            return carry
        lax.fori_loop(0, n_far // 2, body, 0)

    s = _scores(q, k_ref[pl.ds(start, near_keys), :]) + near_ref[clamped.astype(jnp.int32)]
    update(s, v_ref[pl.ds(start, near_keys), :])
    l = jnp.sum(l_scr[...], axis=-1, keepdims=True)
    o_ref[...] = (acc_scr[...] / l).astype(o_ref.dtype)


def _dil(slab, B, seq):
    tq, tk = TQ_DIL, TK_DIL
    nq = seq // tq
    H = B_HEADS
    near, far = (jnp.asarray(a) for a in _dil_bias_tables(tq, tk, seq))
    return pl.pallas_call(
        functools.partial(_dil_kernel, tq=tq, tk=tk),
        grid=(B, H, nq),
        in_specs=[pl.BlockSpec((tq, LANES), lambda b, h, q: (b * nq + q, COL_QB // LANES + h)),
                  pl.BlockSpec((seq, LANES), lambda b, h, q: (b, COL_KB // LANES + h)),
                  pl.BlockSpec((seq, LANES), lambda b, h, q: (b, COL_VB // LANES + h)),
                  pl.BlockSpec(near.shape, lambda b, h, q: (0, 0, 0)),
                  pl.BlockSpec(far.shape, lambda b, h, q: (0, 0))],
        out_specs=pl.BlockSpec((tq, LANES), lambda b, h, q: (b * nq + q, h)),
        out_shape=jax.ShapeDtypeStruct((B * seq, B_WIDTH), BF16),
        scratch_shapes=[pltpu.VMEM((tq, LANES), F32)] * 3 + [pltpu.VMEM((tq, tk), F32)] * 2,
        compiler_params=_cparams(("arbitrary", "arbitrary", "arbitrary")),
        name="dilated",
    )(slab, slab, slab, near, far)


def _silu(z):
    return z * jax.nn.sigmoid(z)


def _out_kernel(ocmp_ref, osel_ref, owin_ref, za_ref, ob_ref, zb_ref, ma_ref, mb_ref,
                x_ref, gate_ref, gpost_ref, wa_ref, wb_ref, wo_ref, o_ref):
    oa = ocmp_ref[...].astype(F32) + osel_ref[...].astype(F32) + owin_ref[...].astype(F32)
    a_in = (oa * _silu(za_ref[...].astype(F32))).astype(BF16)
    ya = jnp.dot(a_in, wa_ref[...], preferred_element_type=F32)
    b_in = (ob_ref[...].astype(F32) * _silu(zb_ref[...].astype(F32))).astype(BF16)
    yb = jnp.dot(b_in, wb_ref[...], preferred_element_type=F32)
    merged = (jax.nn.sigmoid(ma_ref[...].astype(F32)) * ya
              + jax.nn.sigmoid(mb_ref[...].astype(F32)) * yb)
    out = jnp.dot(merged.astype(BF16), wo_ref[...], preferred_element_type=F32)
    y = out * lax.rsqrt(jnp.mean(out * out, axis=-1, keepdims=True) + EPS) * gpost_ref[...]
    o_ref[...] = x_ref[...] + gate_ref[0] * y


def _out(ocmp, osel, owin, ob, slab, x2d, ada3, g_post, wa, wb, wo, seq):
    M, D = x2d.shape
    tm = 256
    per_b = seq // tm
    const = lambda a: pl.BlockSpec(a.shape, lambda i: (0,) * a.ndim, pipeline_mode=pl.Buffered(1))
    row = lambda w, cb: pl.BlockSpec((tm, w), lambda i: (i, cb))
    return pl.pallas_call(
        _out_kernel,
        grid=(M // tm,),
        in_specs=[row(A_WIDTH, 0), row(A_WIDTH, 0), row(A_WIDTH, 0),
                  row(A_WIDTH, COL_ZA // A_WIDTH), row(B_WIDTH, 0), row(B_WIDTH, COL_ZB // B_WIDTH),
                  row(D_MODEL, COL_MA // D_MODEL), row(D_MODEL, COL_MB // D_MODEL),
                  row(D, 0),
                  pl.BlockSpec((1, 1, D), lambda i: (i // per_b, 0, 2)),
                  pl.BlockSpec((1, D), lambda i: (0, 0)),
                  const(wa), const(wb), const(wo)],
        out_specs=pl.BlockSpec((tm, D), lambda i: (i, 0)),
        out_shape=jax.ShapeDtypeStruct((M, D), F32),
        compiler_params=_cparams(("arbitrary",)),
        name="out",
    )(ocmp, osel, owin, slab, ob, slab, slab, slab, x2d, ada3, g_post.reshape(1, D), wa, wb, wo)


def _layer(x, c, positions, w_ada, b_ada, g_pre, g_post, w_in, pe_ck, pe_cv, w_ck1, w_ck2,
           w_cv1, w_cv2, w_br_a, w_br_b, w_out):
    B, S, D = x.shape
    x2d = x.reshape(B * S, D)
    ada3 = _ada(c, w_ada, b_ada).reshape(B, 1, 3 * D)
    slab, kcv = _proj(x2d, ada3, g_pre, positions.reshape(B * S, 1), _build_w_slab_t(jnp.swapaxes(w_in, 0, 1)), S)

    nblk = S // CMP_STRIDE
    assert (S - CMP_BLOCK) // CMP_STRIDE + 1 == nblk - 1
    cmp_end = np.minimum(np.arange(nblk) * CMP_STRIDE + CMP_BLOCK - 1, S - 1)
    posc = positions[:, cmp_end].reshape(B, nblk, 1)
    dup = lambda w: jnp.concatenate([w, w], axis=1).astype(BF16)
    pe2 = jnp.concatenate([pe_ck, pe_cv], axis=1)
    kc2, vc2 = _cmp_kv(kcv, pe2, posc, _build_cmp_w1(w_ck1, w_cv1), dup(w_ck2), dup(w_cv2), B, S)

    ocmp, osel, owin = _nsa_attend(slab, kc2, vc2, B, S)
    ob = _dil(slab, B, S)
    out = _out(ocmp, osel, owin, ob, slab, x2d, ada3, g_post,
               w_br_a.astype(BF16), w_br_b.astype(BF16), w_out.astype(BF16), S)
    return out.reshape(B, S, D)


def kernel(x, c, positions, w_ada, b_ada, g_pre, g_post, w_in, pe_ck, pe_cv, w_ck1, w_ck2, w_cv1, w_cv2, w_br_a, w_br_b, w_out):
    h = x
    for layer in range(w_ada.shape[0]):
        h = _layer(h, c, positions, w_ada[layer], b_ada[layer], g_pre[layer], g_post[layer],
                   w_in[layer], pe_ck[layer], pe_cv[layer], w_ck1[layer], w_ck2[layer],
                   w_cv1[layer], w_cv2[layer], w_br_a[layer], w_br_b[layer], w_out[layer])
    return h
```

```python
import functools

import numpy as np
import jax
import jax.numpy as jnp
from jax import lax
from jax.experimental import pallas as pl
from jax.experimental.pallas import tpu as pltpu

F32 = jnp.float32
BF16 = jnp.bfloat16

D_MODEL = 2048
A_HEADS = 16
A_HEAD_DIM = 64
A_KV_GROUPS = 4
A_REP = A_HEADS // A_KV_GROUPS
A_WIDTH = A_HEADS * A_HEAD_DIM
A_KV_WIDTH = A_KV_GROUPS * A_HEAD_DIM
CMP_BLOCK = 32
CMP_STRIDE = 16
CMP_HIDDEN = 4 * A_HEAD_DIM
SEL_BLOCK = 64
SEL_TOPK = 16
WIN_SIZE = 512
FORCE_BONUS = 1.0e4
B_HEADS = 8
B_HEAD_DIM = 128
B_WIDTH = B_HEADS * B_HEAD_DIM
ROPE_THETA = 500000.0
EPS = 1e-6
NEG = -1e30
M_INIT = -1.0e38
LOG2E = 1.4426950408889634

LANES = 128
MXU_N = 256
VMEM_LIMIT = 56 * 1024 * 1024

TN = 512
COL_QA = 0
COL_KS = 1024
COL_KW = 1536
COL_QB = 2048
COL_KB = 3072
COL_VB = 4096
COL_ZA = 5120
COL_ZB = 6144
COL_VS = 7168
COL_VW = 7680
COL_MA = 8192
COL_MB = 10240
N_SLAB = 12288
COL_KCV = N_SLAB
N_PROJ = N_SLAB + 2 * A_KV_WIDTH
GATE_LANE = 96
AUG_LANE = 64
ONES_LANE = 64

T_ATT = 256


def _cparams(sem):
    return pltpu.CompilerParams(dimension_semantics=sem, vmem_limit_bytes=VMEM_LIMIT)


def _ada_kernel(c_ref, w_ref, b_ref, o_ref):
    o_ref[...] = jnp.dot(c_ref[...], w_ref[...], preferred_element_type=F32,
                         precision=lax.Precision.HIGHEST) + b_ref[...]


def _ada(c, w_ada, b_ada):
    B, D = c.shape
    N = w_ada.shape[1]
    tn = 768
    return pl.pallas_call(
        _ada_kernel,
        grid=(N // tn,),
        in_specs=[pl.BlockSpec((B, D), lambda j: (0, 0)),
                  pl.BlockSpec((D, tn), lambda j: (0, j)),
                  pl.BlockSpec((1, tn), lambda j: (0, j))],
        out_specs=pl.BlockSpec((B, tn), lambda j: (0, j)),
        out_shape=jax.ShapeDtypeStruct((B, N), F32),
        compiler_params=_cparams(("arbitrary",)),
        name="ada",
    )(c, w_ada, b_ada.reshape(1, N))


def _rope_inv_tables():
    lane = np.arange(LANES)
    inv_a8 = (ROPE_THETA ** (-np.arange(0, 16, 2) / 16)).astype(np.float32)
    inv_b16 = (ROPE_THETA ** (-np.arange(0, 32, 2) / 32)).astype(np.float32)
    la = lane % A_HEAD_DIM
    inv_a = np.where(la < 16, inv_a8[la % 8], 0.0).astype(np.float32)
    inv_b = np.where(lane < 32, inv_b16[lane % 16], 0.0).astype(np.float32)
    return inv_a.reshape(1, LANES), inv_b.reshape(1, LANES)


def _rope_tables(pos_f, inv, period, half):
    ang = pos_f * inv
    c, s = jnp.cos(ang), jnp.sin(ang)
    lane = lax.broadcasted_iota(jnp.int32, ang.shape, 1) % period
    cos_t = jnp.where(lane < 2 * half, c, 1.0)
    sin_t = jnp.where(lane < half, -s, jnp.where(lane < 2 * half, s, 0.0))
    return cos_t, sin_t


def _rope_apply(x, cos_t, sin_t, half):
    lane = lax.broadcasted_iota(jnp.int32, x.shape, 1)
    partner = jnp.take_along_axis(x, lane ^ half, axis=1)
    return x * cos_t + partner * sin_t


def _proj_kernel(x_ref, shift_ref, scale_ref, gpre_ref, pos_ref, inva_ref, invb_ref, w_ref,
                 o_ref, kcv_ref, h_scr, ca, sa, cb, sb, *, tm, seq):
    i = pl.program_id(0)
    j = pl.program_id(1)

    @pl.when(j == 0)
    def _():
        x = x_ref[...]
        ms = jnp.mean(x * x, axis=-1, keepdims=True)
        y = x * lax.rsqrt(ms + EPS) * gpre_ref[...]
        h = y * (1.0 + scale_ref[0]) + shift_ref[0]
        h_scr[...] = h.astype(BF16)
        pos_f = pos_ref[...].astype(F32)
        ca[...], sa[...] = _rope_tables(pos_f, inva_ref[...], A_HEAD_DIM, 8)
        cb[...], sb[...] = _rope_tables(pos_f, invb_ref[...], LANES, 16)

    def tile(epilogue):
        for c0 in range(0, TN, MXU_N):
            acc = lax.dot_general(h_scr[...], w_ref[c0:c0 + MXU_N, :], (((1,), (1,)), ((), ())),
                                  preferred_element_type=F32)
            epilogue(acc, c0)

    def rope_epilogue(tabs, half, mul, first_head_only, onehot):
        def epilogue(acc, c0):
            cos_t, sin_t = tabs[0][...], tabs[1][...]
            lane = lax.broadcasted_iota(jnp.int32, (tm, LANES), 1)
            if first_head_only:
                keep = lane < A_HEAD_DIM
                cos_t = jnp.where(keep, cos_t, 1.0)
                sin_t = jnp.where(keep, sin_t, 0.0)
            if onehot:
                t = (i * tm + lax.broadcasted_iota(jnp.int32, (tm, LANES), 0)) % seq
                hot = (lane >= AUG_LANE) & (lane < AUG_LANE + seq // SEL_BLOCK) & \
                      ((t // SEL_BLOCK) == (lane - AUG_LANE))
            for cidx in range(MXU_N // LANES):
                xc = acc[:, cidx * LANES:(cidx + 1) * LANES]
                r = _rope_apply(xc, cos_t, sin_t, half)
                if mul != 1.0:
                    r = r * mul
                if onehot:
                    r = jnp.where(hot, 1.0, r)
                o_ref[:, c0 + cidx * LANES:c0 + (cidx + 1) * LANES] = r.astype(o_ref.dtype)
        return epilogue

    def plain_epilogue(acc, c0):
        o_ref[:, c0:c0 + MXU_N] = acc.astype(o_ref.dtype)

    def ones_lane_epilogue(acc, c0):
        lane = lax.broadcasted_iota(jnp.int32, acc.shape, 1) % LANES
        o_ref[:, c0:c0 + MXU_N] = jnp.where(lane == ONES_LANE, 1.0, acc).astype(o_ref.dtype)

    def kcv_epilogue(acc, c0):
        kcv_ref[:, c0:c0 + MXU_N] = acc

    ta = (ca, sa)
    tb = (cb, sb)
    is_vpad = (j == COL_VS // TN) | (j == COL_VW // TN)
    pl.when(j < COL_KS // TN)(
        lambda: tile(rope_epilogue(ta, 8, A_HEAD_DIM ** -0.5 * LOG2E, False, False)))
    pl.when(j == COL_KS // TN)(lambda: tile(rope_epilogue(ta, 8, 1.0, True, True)))
    pl.when(j == COL_KW // TN)(lambda: tile(rope_epilogue(ta, 8, 1.0, True, False)))
    pl.when((j >= COL_QB // TN) & (j < COL_KB // TN))(
        lambda: tile(rope_epilogue(tb, 16, B_HEAD_DIM ** -0.5 * LOG2E, False, False)))
    pl.when((j >= COL_KB // TN) & (j < COL_VB // TN))(
        lambda: tile(rope_epilogue(tb, 16, 1.0, False, False)))
    pl.when((j >= COL_VB // TN) & (j < COL_KCV // TN) & jnp.logical_not(is_vpad))(
        lambda: tile(plain_epilogue))
    pl.when(is_vpad)(lambda: tile(ones_lane_epilogue))
    pl.when(j == COL_KCV // TN)(lambda: tile(kcv_epilogue))


def _proj(x2d, ada3, g_pre, pos2d, w_slab_t, seq):
    M, D = x2d.shape
    tm = 1024
    per_b = seq // tm
    inv_a, inv_b = _rope_inv_tables()
    tab = pltpu.VMEM((tm, LANES), F32)
    last_slab_tile = N_SLAB // TN - 1
    return pl.pallas_call(
        functools.partial(_proj_kernel, tm=tm, seq=seq),
        grid=(M // tm, N_PROJ // TN),
        in_specs=[pl.BlockSpec((tm, D), lambda i, j: (i, 0)),
                  pl.BlockSpec((1, 1, D), lambda i, j: (i // per_b, 0, 0)),
                  pl.BlockSpec((1, 1, D), lambda i, j: (i // per_b, 0, 1)),
                  pl.BlockSpec((1, D), lambda i, j: (0, 0)),
                  pl.BlockSpec((tm, 1), lambda i, j: (i, 0)),
                  pl.BlockSpec((1, LANES), lambda i, j: (0, 0)),
                  pl.BlockSpec((1, LANES), lambda i, j: (0, 0)),
                  pl.BlockSpec((TN, D), lambda i, j: (j, 0))],
        out_specs=[pl.BlockSpec((tm, TN), lambda i, j: (i, jnp.minimum(j, last_slab_tile))),
                   pl.BlockSpec((tm, TN), lambda i, j: (i, 0))],
        out_shape=[jax.ShapeDtypeStruct((M, N_SLAB), BF16),
                   jax.ShapeDtypeStruct((M, N_PROJ - N_SLAB), F32)],
        scratch_shapes=[pltpu.VMEM((tm, D), BF16), tab, tab, tab, tab],
        compiler_params=_cparams(("arbitrary", "arbitrary")),
        name="proj",
    )(x2d, ada3, ada3, g_pre.reshape(1, D), pos2d, jnp.asarray(inv_a), jnp.asarray(inv_b), w_slab_t)


_SRC_SIZES = (A_WIDTH,) + (A_KV_WIDTH,) * 6 + (A_HEADS * 3, A_WIDTH) + (B_WIDTH,) * 4 + (D_MODEL, D_MODEL)
_SRC = [int(v) for v in np.concatenate([[0], np.cumsum(_SRC_SIZES)])]
(SRC_QA, SRC_KC, SRC_VC, SRC_KS, SRC_VS, SRC_KW, SRC_VW, SRC_GA, SRC_ZA, SRC_QB, SRC_KB, SRC_VB,
 SRC_ZB, SRC_MA, SRC_MB, SRC_END) = _SRC


def _w_slab_kernel(w_ref, o_ref):
    cols = w_ref.shape[1]

    def put(dst, src, width):
        o_ref[dst:dst + width, :] = w_ref[src:src + width, :].astype(o_ref.dtype)

    def clear(dst, width):
        o_ref[dst:dst + width, :] = jnp.zeros((width, cols), o_ref.dtype)

    put(COL_QA, SRC_QA, A_WIDTH)
    put(COL_QB, SRC_QB, B_WIDTH)
    put(COL_KB, SRC_KB, B_WIDTH)
    put(COL_VB, SRC_VB, B_WIDTH)
    put(COL_ZA, SRC_ZA, A_WIDTH)
    put(COL_ZB, SRC_ZB, B_WIDTH)
    put(COL_MA, SRC_MA, D_MODEL)
    put(COL_MB, SRC_MB, D_MODEL)
    ngate = A_REP * 3
    for g in range(A_KV_GROUPS):
        d = g * LANES
        s = g * A_HEAD_DIM
        put(COL_KS + d, SRC_KS + s, A_HEAD_DIM)
        clear(COL_KS + d + A_HEAD_DIM, GATE_LANE - A_HEAD_DIM)
        put(COL_KS + d + GATE_LANE, SRC_GA + g * ngate, ngate)
        clear(COL_KS + d + GATE_LANE + ngate, LANES - GATE_LANE - ngate)
        for col, src in ((COL_KW, SRC_KW), (COL_VS, SRC_VS), (COL_VW, SRC_VW)):
            put(col + d, src + s, A_HEAD_DIM)
            clear(col + d + A_HEAD_DIM, LANES - A_HEAD_DIM)
        put(COL_KCV + d, SRC_KC + s, A_HEAD_DIM)
        put(COL_KCV + d + A_HEAD_DIM, SRC_VC + s, A_HEAD_DIM)


def _build_w_slab_t(w_in_t):
    n_in, D = w_in_t.shape
    assert n_in == SRC_END
    cols = 256
    return pl.pallas_call(
        _w_slab_kernel,
        grid=(D // cols,),
        in_specs=[pl.BlockSpec((n_in, cols), lambda i: (0, i))],
        out_specs=pl.BlockSpec((N_PROJ, cols), lambda i: (0, i)),
        out_shape=jax.ShapeDtypeStruct((N_PROJ, D), BF16),
        compiler_params=_cparams(("arbitrary",)),
        name="w_slab",
    )(w_in_t)


def _cmp_kv_kernel(t_ref, pe_ref, posc_ref, inva_ref, w1_ref, wk2_ref, wv2_ref, kc2_ref, vc2_ref):
    nblk = t_ref.shape[0] // CMP_STRIDE
    hidden2 = w1_ref.shape[-1]
    p_lo = jnp.zeros((nblk, hidden2), F32)
    p_hi = jnp.zeros((nblk, hidden2), F32)
    for l in range(CMP_STRIDE):
        x = t_ref[pl.ds(l, nblk, stride=CMP_STRIDE), :]
        p_lo += jnp.dot((x + pe_ref[l:l + 1, :]).astype(BF16), w1_ref[l],
                        preferred_element_type=F32)
        p_hi += jnp.dot((x + pe_ref[CMP_STRIDE + l:CMP_STRIDE + l + 1, :]).astype(BF16),
                        w1_ref[CMP_STRIDE + l], preferred_element_type=F32)
    hid = p_lo + pltpu.roll(p_hi, nblk - 1, 0)
    act = jax.nn.gelu(hid).astype(BF16)
    kk = jnp.dot(act[:, :CMP_HIDDEN], wk2_ref[...], preferred_element_type=F32)
    vv = jnp.dot(act[:, CMP_HIDDEN:], wv2_ref[...], preferred_element_type=F32)
    pos_f = posc_ref[...].astype(F32)
    cos_t, sin_t = _rope_tables(pos_f, inva_ref[...], A_HEAD_DIM, 8)
    kk = _rope_apply(kk, cos_t, sin_t, 8)
    lane = lax.broadcasted_iota(jnp.int32, kk.shape, 1)
    lo = lane < A_HEAD_DIM
    kc2_ref[0:nblk, :] = jnp.where(lo, kk, 0.0).astype(kc2_ref.dtype)
    kc2_ref[nblk:2 * nblk, :] = jnp.where(lo, 0.0, kk).astype(kc2_ref.dtype)
    vc2_ref[0:nblk, :] = jnp.where(lo, vv, 0.0).astype(vc2_ref.dtype)
    vc2_ref[nblk:2 * nblk, :] = jnp.where(lo, 0.0, vv).astype(vc2_ref.dtype)


def _cmp_kv(kcv, pe2, posc, w1, wk2d, wv2d, B, seq):
    G = A_KV_GROUPS
    nblk = seq // CMP_STRIDE
    inv_a, _ = _rope_inv_tables()
    full = lambda a: pl.BlockSpec(a.shape, lambda b, g: (0,) * a.ndim)
    out = jax.ShapeDtypeStruct((B, G, 2 * nblk, LANES), BF16)
    ospec = pl.BlockSpec((None, None, 2 * nblk, LANES), lambda b, g: (b, g, 0, 0))
    return pl.pallas_call(
        _cmp_kv_kernel,
        grid=(B, G),
        in_specs=[pl.BlockSpec((seq, LANES), lambda b, g: (b, g)),
                  full(pe2),
                  pl.BlockSpec((None, nblk, 1), lambda b, g: (b, 0, 0)),
                  pl.BlockSpec((1, LANES), lambda b, g: (0, 0)),
                  full(w1), full(wk2d), full(wv2d)],
        out_specs=[ospec, ospec],
        out_shape=[out, out],
        compiler_params=_cparams(("arbitrary", "arbitrary")),
        name="cmp_kv",
    )(kcv, pe2, posc, jnp.asarray(inv_a), w1, wk2d, wv2d)


def _build_cmp_w1(w_ck1, w_cv1):
    wk = w_ck1.reshape(CMP_BLOCK, A_HEAD_DIM, CMP_HIDDEN)
    wv = w_cv1.reshape(CMP_BLOCK, A_HEAD_DIM, CMP_HIDDEN)
    z = jnp.zeros_like(wk)
    return jnp.concatenate([jnp.concatenate([wk, z], axis=2),
                            jnp.concatenate([z, wv], axis=2)], axis=1).astype(BF16)


def _gate_expand_table(branch):
    e = np.zeros((LANES, A_REP * LANES), np.float32)
    for r in range(A_REP):
        e[GATE_LANE + 3 * r + branch, r * LANES:(r + 1) * LANES] = 1.0
    return e


def _branch_gates(gate_blk, e_ref):
    sg = jax.nn.sigmoid(gate_blk.astype(F32)).astype(BF16)
    return jnp.dot(sg, e_ref[...], preferred_element_type=F32)


def _overlap_table_t(seq):
    n_c = (seq - CMP_BLOCK) // CMP_STRIDE + 1
    n_s = seq // SEL_BLOCK
    cs = np.arange(n_c) * CMP_STRIDE
    ss = np.arange(n_s) * SEL_BLOCK
    ov = np.clip(np.minimum(cs[:, None] + CMP_BLOCK, ss[None, :] + SEL_BLOCK)
                 - np.maximum(cs[:, None], ss[None, :]), 0, None).astype(np.float32) / CMP_BLOCK
    full = np.zeros((LANES, LANES), np.float32)
    full[AUG_LANE:AUG_LANE + n_s, :n_c] = ov.T
    return full


def _split3(x):
    hi = x.astype(BF16)
    r1 = x - hi.astype(F32)
    mid = r1.astype(BF16)
    lo = (r1 - mid.astype(F32)).astype(BF16)
    return hi, mid, lo


def _compressed_branch(q_ref, kc2_ref, vc2_ref, ovt_ref, gates, ocmp_ref, qi, tq, n_sel_blocks):
    t = qi * tq + lax.broadcasted_iota(jnp.int32, (tq, LANES), 0)
    lane = lax.broadcasted_iota(jnp.int32, (tq, LANES), 1)
    cmask = (CMP_STRIDE * lane + (CMP_BLOCK - 1)) <= t
    cmask_f = cmask.astype(F32)
    nt = (((1,), (1,)), ((), ()))
    psum = jnp.zeros((tq, LANES), F32)
    for pair in range(A_REP // 2):
        qp = q_ref[:, pair * LANES:(pair + 1) * LANES]
        s2 = lax.dot_general(qp, kc2_ref[...], nt, preferred_element_type=F32)
        probs = []
        for hh in range(2):
            s = jnp.where(cmask, s2[:, hh * LANES:(hh + 1) * LANES], NEG)
            mx = jnp.max(s, axis=-1, keepdims=True)
            e = jnp.exp2(s - mx) * cmask_f
            pr = e / jnp.maximum(jnp.sum(e, axis=-1, keepdims=True), 1.0)
            probs.append(pr)
            psum = psum + pr
        p2 = jnp.concatenate(probs, axis=1).astype(BF16)
        o_pair = jnp.dot(p2, vc2_ref[...], preferred_element_type=F32)
        gate = jnp.where(lane < A_HEAD_DIM, gates[:, (2 * pair) * LANES:(2 * pair + 1) * LANES],
                         gates[:, (2 * pair + 1) * LANES:(2 * pair + 2) * LANES])
        ocmp_ref[:, pair * LANES:(pair + 1) * LANES] = (gate * o_pair).astype(ocmp_ref.dtype)

    ovt = ovt_ref[...]
    imp_t = sum(lax.dot_general(ovt, part, nt, preferred_element_type=F32) for part in _split3(psum))
    imp_t = imp_t[AUG_LANE:AUG_LANE + n_sel_blocks, :]
    jb = lax.broadcasted_iota(jnp.int32, imp_t.shape, 0)
    tcol = qi * tq + lax.broadcasted_iota(jnp.int32, imp_t.shape, 1)
    cur = tcol // SEL_BLOCK
    valid = (SEL_BLOCK * jb) <= tcol
    forced = valid & ((jb == 0) | (jb == cur) | (jb == cur - 1))
    score = jnp.where(valid, imp_t + FORCE_BONUS * forced.astype(F32), NEG)
    rank = jnp.zeros(imp_t.shape, F32)
    for i in range(n_sel_blocks):
        other = score[i:i + 1, :]
        beats = (other > score) | ((other == score) & (jb > i))
        rank = rank + beats.astype(F32)
    selb = jnp.where(rank < float(min(SEL_TOPK, n_sel_blocks)), 0.0, NEG)
    aug_t = jnp.concatenate([jnp.zeros((AUG_LANE, tq), F32), selb,
                             jnp.zeros((LANES - AUG_LANE - n_sel_blocks, tq), F32)], axis=0)
    return aug_t.T.astype(BF16)


def _flash_init(m_scr, acc_scr, l_scr=None):
    m_scr[...] = jnp.full(m_scr.shape, M_INIT, F32)
    acc_scr[...] = jnp.zeros(acc_scr.shape, F32)
    if l_scr is not None:
        l_scr[...] = jnp.zeros(l_scr.shape, F32)


def _scores(q, k):
    return lax.dot_general(q, k, (((1,), (1,)), ((), ())), preferred_element_type=F32)


def _add_bias(s, bias, reps):
    tq, tk = bias.shape
    return (s.reshape(reps, tq, tk) + bias[None]).reshape(reps * tq, tk)


def _flash_update(s, v, m_scr, acc_scr, l_scr=None, first=False):
    chunks = [s[:, c * LANES:(c + 1) * LANES] for c in range(s.shape[1] // LANES)]
    mx = functools.reduce(jnp.maximum, chunks)
    if first:
        assert l_scr is None
        m_new = jnp.broadcast_to(jnp.max(mx, axis=-1, keepdims=True), mx.shape)
        p = jnp.concatenate([jnp.exp2((ch - m_new).astype(v.dtype)) for ch in chunks], axis=1)
        acc_scr[...] = jnp.dot(p, v, preferred_element_type=F32)
        m_scr[...] = m_new
        return
    m_prev = m_scr[...]
    m_new = jnp.maximum(m_prev, jnp.max(mx, axis=-1, keepdims=True))
    alpha = jnp.exp2(m_prev - m_new)
    if l_scr is None:
        p = jnp.concatenate([jnp.exp2((ch - m_new).astype(v.dtype)) for ch in chunks], axis=1)
    else:
        ps = [jnp.exp2(ch - m_new) for ch in chunks]
        p = jnp.concatenate(ps, axis=1).astype(v.dtype)
        l_scr[...] = alpha * l_scr[...] + functools.reduce(jnp.add, ps)
    acc_scr[...] = alpha * acc_scr[...] + jnp.dot(p, v, preferred_element_type=F32)
    m_scr[...] = m_new


def _gated_head(acc, gates, r):
    lane = lax.broadcasted_iota(jnp.int32, acc.shape, 1)
    scale = gates[:, r * LANES:(r + 1) * LANES] / acc[:, ONES_LANE:ONES_LANE + 1]
    return jnp.where(lane < A_HEAD_DIM, acc * scale, 0.0)


def _store_head_pair(o_ref, pair, even, odd):
    o_ref[:, pair * LANES:(pair + 1) * LANES] = (even + pltpu.roll(odd, A_HEAD_DIM, 1)).astype(o_ref.dtype)


def _win_bias_table(t, span, window):
    delta = np.arange(t)[:, None] - np.arange(span)[None, :]
    dist = np.stack([n * t + delta for n in range(span // t)])
    return np.where((dist >= 0) & (dist < window), 0.0, NEG).astype(np.float32)


def _window_branch(q_heads, kw_ref, vw_ref, bias_ref, gates, owin_ref, m_scr, acc_scr, s0_scr, s1_scr,
                   qi, t, span):
    n_tiles = span // t
    start = pl.multiple_of(jnp.maximum((qi + 1) * t - span, 0), t)
    variant = jnp.minimum(qi, n_tiles - 1)
    q = jnp.concatenate(q_heads, axis=0)
    rows_of = lambda ref, w: ref[pl.ds(pl.multiple_of(start + w * t, t), t), :]
    bufs = (s0_scr, s1_scr)

    def score_into(w):
        bufs[w % 2][...] = _scores(q, rows_of(kw_ref, w))

    def fold(w):
        s = _add_bias(bufs[w % 2][...], bias_ref[variant, :, w * t:(w + 1) * t], A_REP)
        _flash_update(s, rows_of(vw_ref, w), m_scr, acc_scr, first=(w == 0))

    score_into(0)
    for w in range(n_tiles):
        if w + 1 < n_tiles:
            score_into(w + 1)
        fold(w)
    heads = [_gated_head(acc_scr[r * t:(r + 1) * t, :], gates, r) for r in range(A_REP)]
    for pair in range(A_REP // 2):
        _store_head_pair(owin_ref, pair, heads[2 * pair], heads[2 * pair + 1])


def _nsa_kernel(q_ref, kc2_ref, vc2_ref, ovt_ref, ks_ref, vs_ref, kw_ref, vw_ref, ecmp_ref, esel_ref,
                ewin_ref, bias_ref, ocmp_ref, osel_ref, owin_ref, qaug_scr, m_scr, acc_scr, s0_scr, s1_scr,
                mw_scr, accw_scr, sw0_scr, sw1_scr, *, t, span, n_sel_blocks):
    qi = pl.program_id(2)
    rows_of = lambda ref, kj: ref[pl.ds(pl.multiple_of(kj * t, t), t), :]
    gate_blk = rows_of(ks_ref, qi)
    lo = lax.broadcasted_iota(jnp.int32, (t, LANES), 1) < A_HEAD_DIM
    q_heads = []
    for r in range(A_REP):
        qp = q_ref[:, (r // 2) * LANES:(r // 2 + 1) * LANES]
        if r % 2:
            qp = jnp.concatenate([qp[:, A_HEAD_DIM:], qp[:, :A_HEAD_DIM]], axis=1)
        q_heads.append(qp)

    _window_branch([jnp.where(lo, qh, jnp.zeros_like(qh)) for qh in q_heads], kw_ref, vw_ref, bias_ref,
                   _branch_gates(gate_blk, ewin_ref), owin_ref, mw_scr, accw_scr, sw0_scr, sw1_scr, qi, t, span)
    aug = _compressed_branch(q_ref, kc2_ref, vc2_ref, ovt_ref, _branch_gates(gate_blk, ecmp_ref),
                             ocmp_ref, qi, t, n_sel_blocks)
    for r in range(A_REP):
        qaug_scr[r] = jnp.where(lo, q_heads[r], aug)

    q = qaug_scr[...].reshape(A_REP * t, LANES)
    _flash_init(m_scr, acc_scr)

    def score_into(buf, kj):
        buf[...] = _scores(q, rows_of(ks_ref, kj))

    def fold(buf, kj, causal=False):
        s = buf[...]
        if causal:
            delta = (lax.broadcasted_iota(jnp.int32, (t, t), 0)
                     - lax.broadcasted_iota(jnp.int32, (t, t), 1))
            s = _add_bias(s, jnp.where(delta >= 0, 0.0, NEG), A_REP)
        _flash_update(s, rows_of(vs_ref, kj), m_scr, acc_scr)

    first = qi % 2
    score_into(s1_scr, 0)
    score_into(s0_scr, first)

    @pl.when(first == 1)
    def _():
        fold(s1_scr, 0)

    def body(i, carry):
        kj = first + 2 * i
        score_into(s1_scr, kj + 1)
        fold(s0_scr, kj)
        score_into(s0_scr, kj + 2)
        fold(s1_scr, kj + 1)
        return carry
    lax.fori_loop(0, (qi - first) // 2, body, 0)

    fold(s0_scr, qi, causal=True)
    gates = _branch_gates(gate_blk, esel_ref)
    sel_heads = [_gated_head(acc_scr[r * t:(r + 1) * t, :], gates, r) for r in range(A_REP)]
    for pair in range(A_REP // 2):
        _store_head_pair(osel_ref, pair, sel_heads[2 * pair], sel_heads[2 * pair + 1])


def _nsa_attend(slab, kc2, vc2, B, seq):
    t = T_ATT
    nq = seq // t
    G = A_KV_GROUPS
    rows = A_REP * t
    span = min(((WIN_SIZE - 1 + t - 1) // t + 1) * t, seq)
    ovt = jnp.asarray(_overlap_table_t(seq), BF16)
    ecmp, esel, ewin = (jnp.asarray(_gate_expand_table(br), BF16) for br in range(3))
    bias = jnp.asarray(_win_bias_table(t, span, WIN_SIZE))
    kv = lambda col: pl.BlockSpec((seq, LANES), lambda b, g, q: (b, col // LANES + g))
    const = lambda a: pl.BlockSpec(a.shape, lambda b, g, q: (0,) * a.ndim)
    ckv = pl.BlockSpec((None, None) + kc2.shape[2:], lambda b, g, q: (b, g, 0, 0))
    out = jax.ShapeDtypeStruct((B * seq, A_WIDTH), BF16)
    ospec = pl.BlockSpec((t, A_REP * A_HEAD_DIM), lambda b, g, q: (b * nq + q, g))
    return pl.pallas_call(
        functools.partial(_nsa_kernel, t=t, span=span, n_sel_blocks=seq // SEL_BLOCK),
        grid=(B, G, nq),
        in_specs=[pl.BlockSpec((t, A_REP * A_HEAD_DIM), lambda b, g, q: (b * nq + q, g)),
                  ckv, ckv, const(ovt), kv(COL_KS), kv(COL_VS), kv(COL_KW), kv(COL_VW),
                  const(ecmp), const(esel), const(ewin), const(bias)],
        out_specs=[ospec, ospec, ospec],
        out_shape=[out, out, out],
        scratch_shapes=[pltpu.VMEM((A_REP, t, LANES), BF16)]
        + [pltpu.VMEM((rows, LANES), F32), pltpu.VMEM((rows, LANES), F32),
           pltpu.VMEM((rows, t), F32), pltpu.VMEM((rows, t), F32)] * 2,
        compiler_params=_cparams(("arbitrary", "arbitrary", "arbitrary")),
        name="nsa",
    )(slab, kc2, vc2, ovt, slab, slab, slab, slab, ecmp, esel, ewin, bias)


DILATIONS = ((128, 1), (512, 4), (2048, 16))
TQ_DIL = 512
TK_DIL = 256


def _dil_bias_tables(tq, tk, seq):
    w_near = max(w for w, r in DILATIONS if w < seq)
    near_keys = tq + -(-w_near // tk) * tk
    w_far, r_far = DILATIONS[-1]
    assert w_far >= seq and tq % r_far == 0 and tk % r_far == 0 and near_keys == 2 * tq
    assert tq % (2 * tk) == 0
    delta = np.arange(tq)[:, None] - np.arange(near_keys)[None, :]
    near = []
    for base in (near_keys - tq, 0):
        dist = base + delta
        mult = sum(((dist >= 0) & (dist <= w) & (dist % r == 0)).astype(np.float64) for w, r in DILATIONS)
        near.append(np.where(mult > 0, np.log2(np.maximum(mult, 1.0)), NEG))
    far = np.where(delta[:, :tk] % r_far == 0, 0.0, NEG)
    return np.stack(near).astype(np.float32), far.astype(np.float32)


def _dil_kernel(q_ref, k_ref, v_ref, near_ref, far_ref, o_ref, m_scr, acc_scr, l_scr, s0_scr, s1_scr,
                *, tq, tk):
    qi = pl.program_id(2)
    q = q_ref[...]
    near_keys = near_ref.shape[-1]
    _flash_init(m_scr, acc_scr, l_scr)
    update = functools.partial(_flash_update, m_scr=m_scr, acc_scr=acc_scr, l_scr=l_scr)
    rows_of = lambda ref, kj: ref[pl.ds(pl.multiple_of(kj * tk, tk), tk), :]
    clamped = (qi + 1) * tq < near_keys
    start = pl.multiple_of(jnp.maximum((qi + 1) * tq - near_keys, 0), tk)
    n_far = start // tk

    def score_into(buf, kj):
        buf[...] = _scores(q, rows_of(k_ref, kj))

    def fold(buf, kj):
        update(buf[...] + far_ref[...], rows_of(v_ref, kj))

    @pl.when(n_far > 0)
    def _():
        score_into(s0_scr, 0)

        def body(i, carry):
            kj = 2 * i
            score_into(s1_scr, kj + 1)
            fold(s0_scr, kj)
            score_into(s0_scr, kj + 2)
            fold(s1_scr, kj + 1)
            return carry
        lax.fori_loop(0, n_far // 2, body, 0)

    s = _scores(q, k_ref[pl.ds(start, near_keys), :]) + near_ref[clamped.astype(jnp.int32)]
    update(s, v_ref[pl.ds(start, near_keys), :])
    l = jnp.sum(l_scr[...], axis=-1, keepdims=True)
    o_ref[...] = (acc_scr[...] / l).astype(o_ref.dtype)


def _dil(slab, B, seq):
    tq, tk = TQ_DIL, TK_DIL
    nq = seq // tq
    H = B_HEADS
    near, far = (jnp.asarray(a) for a in _dil_bias_tables(tq, tk, seq))
    return pl.pallas_call(
        functools.partial(_dil_kernel, tq=tq, tk=tk),
        grid=(B, H, nq),
        in_specs=[pl.BlockSpec((tq, LANES), lambda b, h, q: (b * nq + q, COL_QB // LANES + h)),
                  pl.BlockSpec((seq, LANES), lambda b, h, q: (b, COL_KB // LANES + h)),
                  pl.BlockSpec((seq, LANES), lambda b, h, q: (b, COL_VB // LANES + h)),
                  pl.BlockSpec(near.shape, lambda b, h, q: (0, 0, 0)),
                  pl.BlockSpec(far.shape, lambda b, h, q: (0, 0))],
        out_specs=pl.BlockSpec((tq, LANES), lambda b, h, q: (b * nq + q, h)),
        out_shape=jax.ShapeDtypeStruct((B * seq, B_WIDTH), BF16),
        scratch_shapes=[pltpu.VMEM((tq, LANES), F32)] * 3 + [pltpu.VMEM((tq, tk), F32)] * 2,
        compiler_params=_cparams(("arbitrary", "arbitrary", "arbitrary")),
        name="dilated",
    )(slab, slab, slab, near, far)


def _silu(z):
    return z * jax.nn.sigmoid(z)


def _out_kernel(ocmp_ref, osel_ref, owin_ref, za_ref, ob_ref, zb_ref, ma_ref, mb_ref,
                x_ref, gate_ref, gpost_ref, wa_ref, wb_ref, wo_ref, o_ref):
    oa = ocmp_ref[...].astype(F32) + osel_ref[...].astype(F32) + owin_ref[...].astype(F32)
    a_in = (oa * _silu(za_ref[...].astype(F32))).astype(BF16)
    ya = jnp.dot(a_in, wa_ref[...], preferred_element_type=F32)
    b_in = (ob_ref[...].astype(F32) * _silu(zb_ref[...].astype(F32))).astype(BF16)
    yb = jnp.dot(b_in, wb_ref[...], preferred_element_type=F32)
    merged = (jax.nn.sigmoid(ma_ref[...].astype(F32)) * ya
              + jax.nn.sigmoid(mb_ref[...].astype(F32)) * yb)
    out = jnp.dot(merged.astype(BF16), wo_ref[...], preferred_element_type=F32)
    y = out * lax.rsqrt(jnp.mean(out * out, axis=-1, keepdims=True) + EPS) * gpost_ref[...]
    o_ref[...] = x_ref[...] + gate_ref[0] * y


def _out(ocmp, osel, owin, ob, slab, x2d, ada3, g_post, wa, wb, wo, seq):
    M, D = x2d.shape
    tm = 256
    per_b = seq // tm
    const = lambda a: pl.BlockSpec(a.shape, lambda i: (0,) * a.ndim, pipeline_mode=pl.Buffered(1))
    row = lambda w, cb: pl.BlockSpec((tm, w), lambda i: (i, cb))
    return pl.pallas_call(
        _out_kernel,
        grid=(M // tm,),
        in_specs=[row(A_WIDTH, 0), row(A_WIDTH, 0), row(A_WIDTH, 0),
                  row(A_WIDTH, COL_ZA // A_WIDTH), row(B_WIDTH, 0), row(B_WIDTH, COL_ZB // B_WIDTH),
                  row(D_MODEL, COL_MA // D_MODEL), row(D_MODEL, COL_MB // D_MODEL),
                  row(D, 0),
                  pl.BlockSpec((1, 1, D), lambda i: (i // per_b, 0, 2)),
                  pl.BlockSpec((1, D), lambda i: (0, 0)),
                  const(wa), const(wb), const(wo)],
        out_specs=pl.BlockSpec((tm, D), lambda i: (i, 0)),
        out_shape=jax.ShapeDtypeStruct((M, D), F32),
        compiler_params=_cparams(("arbitrary",)),
        name="out",
    )(ocmp, osel, owin, slab, ob, slab, slab, slab, x2d, ada3, g_post.reshape(1, D), wa, wb, wo)


def _layer(x, c, positions, w_ada, b_ada, g_pre, g_post, w_in, pe_ck, pe_cv, w_ck1, w_ck2,
           w_cv1, w_cv2, w_br_a, w_br_b, w_out):
    B, S, D = x.shape
    x2d = x.reshape(B * S, D)
    ada3 = _ada(c, w_ada, b_ada).reshape(B, 1, 3 * D)
    slab, kcv = _proj(x2d, ada3, g_pre, positions.reshape(B * S, 1), _build_w_slab_t(jnp.swapaxes(w_in, 0, 1)), S)

    nblk = S // CMP_STRIDE
    assert (S - CMP_BLOCK) // CMP_STRIDE + 1 == nblk - 1
    cmp_end = np.minimum(np.arange(nblk) * CMP_STRIDE + CMP_BLOCK - 1, S - 1)
    posc = positions[:, cmp_end].reshape(B, nblk, 1)
    dup = lambda w: jnp.concatenate([w, w], axis=1).astype(BF16)
    pe2 = jnp.concatenate([pe_ck, pe_cv], axis=1)
    kc2, vc2 = _cmp_kv(kcv, pe2, posc, _build_cmp_w1(w_ck1, w_cv1), dup(w_ck2), dup(w_cv2), B, S)

    ocmp, osel, owin = _nsa_attend(slab, kc2, vc2, B, S)
    ob = _dil(slab, B, S)
    out = _out(ocmp, osel, owin, ob, slab, x2d, ada3, g_post,
               w_br_a.astype(BF16), w_br_b.astype(BF16), w_out.astype(BF16), S)
    return out.reshape(B, S, D)


def kernel(x, c, positions, w_ada, b_ada, g_pre, g_post, w_in, pe_ck, pe_cv, w_ck1, w_ck2, w_cv1, w_cv2, w_br_a, w_br_b, w_out):
    h = x
    for layer in range(w_ada.shape[0]):
        h = _layer(h, c, positions, w_ada[layer], b_ada[layer], g_pre[layer], g_post[layer],
                   w_in[layer], pe_ck[layer], pe_cv[layer], w_ck1[layer], w_ck2[layer],
                   w_cv1[layer], w_cv2[layer], w_br_a[layer], w_br_b[layer], w_out[layer])
    return h
```

```python
import functools

import numpy as np
import jax
import jax.numpy as jnp
from jax import lax
from jax.experimental import pallas as pl
from jax.experimental.pallas import tpu as pltpu

F32 = jnp.float32
BF16 = jnp.bfloat16

D_MODEL = 2048
A_HEADS = 16
A_HEAD_DIM = 64
A_KV_GROUPS = 4
A_REP = A_HEADS // A_KV_GROUPS
A_WIDTH = A_HEADS * A_HEAD_DIM
A_KV_WIDTH = A_KV_GROUPS * A_HEAD_DIM
CMP_BLOCK = 32
CMP_STRIDE = 16
CMP_HIDDEN = 4 * A_HEAD_DIM
SEL_BLOCK = 64
SEL_TOPK = 16
WIN_SIZE = 512
FORCE_BONUS = 1.0e4
B_HEADS = 8
B_HEAD_DIM = 128
B_WIDTH = B_HEADS * B_HEAD_DIM
ROPE_THETA = 500000.0
EPS = 1e-6
NEG = -1e30
M_INIT = -1.0e38
LOG2E = 1.4426950408889634

LANES = 128
MXU_N = 256
VMEM_LIMIT = 56 * 1024 * 1024

TN = 1024
COL_QA = 0
COL_KS = 1024
COL_KW = 1536
COL_QB = 2048
COL_KB = 3072
COL_VB = 4096
COL_ZA = 5120
COL_ZB = 6144
COL_VS = 7168
COL_VW = 7680
COL_MA = 8192
COL_MB = 10240
N_SLAB = 12288
COL_KCV = N_SLAB
N_PROJ = N_SLAB + 2 * A_KV_WIDTH
GATE_LANE = 96
AUG_LANE = 64
ONES_LANE = 64

T_ATT = 256


def _cparams(sem):
    return pltpu.CompilerParams(dimension_semantics=sem, vmem_limit_bytes=VMEM_LIMIT)


def _ada_kernel(c_ref, w_ref, b_ref, o_ref):
    o_ref[...] = jnp.dot(c_ref[...], w_ref[...], preferred_element_type=F32,
                         precision=lax.Precision.HIGHEST) + b_ref[...]


def _ada(c, w_ada, b_ada):
    B, D = c.shape
    N = w_ada.shape[1]
    tn = 768
    return pl.pallas_call(
        _ada_kernel,
        grid=(N // tn,),
        in_specs=[pl.BlockSpec((B, D), lambda j: (0, 0)),
                  pl.BlockSpec((D, tn), lambda j: (0, j)),
                  pl.BlockSpec((1, tn), lambda j: (0, j))],
        out_specs=pl.BlockSpec((B, tn), lambda j: (0, j)),
        out_shape=jax.ShapeDtypeStruct((B, N), F32),
        compiler_params=_cparams(("arbitrary",)),
        name="ada",
    )(c, w_ada, b_ada.reshape(1, N))


def _rope_inv_tables():
    lane = np.arange(LANES)
    inv_a8 = (ROPE_THETA ** (-np.arange(0, 16, 2) / 16)).astype(np.float32)
    inv_b16 = (ROPE_THETA ** (-np.arange(0, 32, 2) / 32)).astype(np.float32)
    la = lane % A_HEAD_DIM
    inv_a = np.where(la < 16, inv_a8[la % 8], 0.0).astype(np.float32)
    inv_b = np.where(lane < 32, inv_b16[lane % 16], 0.0).astype(np.float32)
    return inv_a.reshape(1, LANES), inv_b.reshape(1, LANES)


def _rope_tables(pos_f, inv, period, half):
    ang = pos_f * inv
    c, s = jnp.cos(ang), jnp.sin(ang)
    lane = lax.broadcasted_iota(jnp.int32, ang.shape, 1) % period
    cos_t = jnp.where(lane < 2 * half, c, 1.0)
    sin_t = jnp.where(lane < half, -s, jnp.where(lane < 2 * half, s, 0.0))
    return cos_t, sin_t


def _rope_apply(x, cos_t, sin_t, half):
    lane = lax.broadcasted_iota(jnp.int32, x.shape, 1)
    partner = jnp.take_along_axis(x, lane ^ half, axis=1)
    return x * cos_t + partner * sin_t


def _proj_kernel(x_ref, shift_ref, scale_ref, gpre_ref, pos_ref, inva_ref, invb_ref, w_ref,
                 o_ref, kcv_ref, h_scr, ca, sa, cb, sb, *, tm, seq):
    i = pl.program_id(0)
    j = pl.program_id(1)

    @pl.when(j == 0)
    def _():
        x = x_ref[...]
        ms = jnp.mean(x * x, axis=-1, keepdims=True)
        y = x * lax.rsqrt(ms + EPS) * gpre_ref[...]
        h = y * (1.0 + scale_ref[0]) + shift_ref[0]
        h_scr[...] = h.astype(BF16)
        pos_f = pos_ref[...].astype(F32)
        ca[...], sa[...] = _rope_tables(pos_f, inva_ref[...], A_HEAD_DIM, 8)
        cb[...], sb[...] = _rope_tables(pos_f, invb_ref[...], LANES, 16)

    def tile(epilogues):
        for part, epilogue in enumerate(epilogues):
            c0 = part * MXU_N
            acc = lax.dot_general(h_scr[...], w_ref[c0:c0 + MXU_N, :], (((1,), (1,)), ((), ())),
                                  preferred_element_type=F32)
            epilogue(acc, c0)

    def rope_epilogue(tabs, half, mul, first_head_only, onehot):
        def epilogue(acc, c0):
            cos_t, sin_t = tabs[0][...], tabs[1][...]
            lane = lax.broadcasted_iota(jnp.int32, (tm, LANES), 1)
            if first_head_only:
                keep = lane < A_HEAD_DIM
                cos_t = jnp.where(keep, cos_t, 1.0)
                sin_t = jnp.where(keep, sin_t, 0.0)
            if onehot:
                t = (i * tm + lax.broadcasted_iota(jnp.int32, (tm, LANES), 0)) % seq
                hot = (lane >= AUG_LANE) & (lane < AUG_LANE + seq // SEL_BLOCK) & \
                      ((t // SEL_BLOCK) == (lane - AUG_LANE))
            for cidx in range(MXU_N // LANES):
                xc = acc[:, cidx * LANES:(cidx + 1) * LANES]
                r = _rope_apply(xc, cos_t, sin_t, half)
                if mul != 1.0:
                    r = r * mul
                if onehot:
                    r = jnp.where(hot, 1.0, r)
                o_ref[:, c0 + cidx * LANES:c0 + (cidx + 1) * LANES] = r.astype(o_ref.dtype)
        return epilogue

    def plain_epilogue(acc, c0):
        o_ref[:, c0:c0 + MXU_N] = acc.astype(o_ref.dtype)

    def ones_lane_epilogue(acc, c0):
        lane = lax.broadcasted_iota(jnp.int32, acc.shape, 1) % LANES
        o_ref[:, c0:c0 + MXU_N] = jnp.where(lane == ONES_LANE, 1.0, acc).astype(o_ref.dtype)

    def kcv_epilogue(acc, c0):
        kcv_ref[:, c0:c0 + MXU_N] = acc

    ta = (ca, sa)
    tb = (cb, sb)
    parts = lambda col0, col1, epi: [((col0 // MXU_N + n), epi) for n in range((col1 - col0) // MXU_N)]
    kinds = dict(
        parts(COL_QA, COL_KS, rope_epilogue(ta, 8, A_HEAD_DIM ** -0.5 * LOG2E, False, False))
        + parts(COL_KS, COL_KW, rope_epilogue(ta, 8, 1.0, True, True))
        + parts(COL_KW, COL_QB, rope_epilogue(ta, 8, 1.0, True, False))
        + parts(COL_QB, COL_KB, rope_epilogue(tb, 16, B_HEAD_DIM ** -0.5 * LOG2E, False, False))
        + parts(COL_KB, COL_VB, rope_epilogue(tb, 16, 1.0, False, False))
        + parts(COL_VB, COL_VS, plain_epilogue)
        + parts(COL_VS, COL_MA, ones_lane_epilogue)
        + parts(COL_MA, COL_KCV, plain_epilogue)
        + parts(COL_KCV, N_PROJ, None))
    per_tile = TN // MXU_N
    steps = {}
    for jt in range(pl.cdiv(N_PROJ, TN)):
        tile_kinds = tuple(kinds[p] for p in range(jt * per_tile, (jt + 1) * per_tile) if p in kinds)
        if tile_kinds[0] is None:
            assert all(k is None for k in tile_kinds) and len(tile_kinds) * MXU_N == kcv_ref.shape[1]
            tile_kinds = (kcv_epilogue,) * len(tile_kinds)
        steps.setdefault(tile_kinds, []).append(jt)
    for tile_kinds, jts in steps.items():
        pl.when(functools.reduce(jnp.logical_or, [j == jt for jt in jts]))(
            functools.partial(tile, tile_kinds))


def _proj(x2d, ada3, g_pre, pos2d, w_slab_t, seq):
    M, D = x2d.shape
    tm = 1024
    per_b = seq // tm
    inv_a, inv_b = _rope_inv_tables()
    tab = pltpu.VMEM((tm, LANES), F32)
    last_slab_tile = N_SLAB // TN - 1
    n_kcv = N_PROJ - N_SLAB
    return pl.pallas_call(
        functools.partial(_proj_kernel, tm=tm, seq=seq),
        grid=(M // tm, pl.cdiv(N_PROJ, TN)),
        in_specs=[pl.BlockSpec((tm, D), lambda i, j: (i, 0)),
                  pl.BlockSpec((1, 1, D), lambda i, j: (i // per_b, 0, 0)),
                  pl.BlockSpec((1, 1, D), lambda i, j: (i // per_b, 0, 1)),
                  pl.BlockSpec((1, D), lambda i, j: (0, 0)),
                  pl.BlockSpec((tm, 1), lambda i, j: (i, 0)),
                  pl.BlockSpec((1, LANES), lambda i, j: (0, 0)),
                  pl.BlockSpec((1, LANES), lambda i, j: (0, 0)),
                  pl.BlockSpec((TN, D), lambda i, j: (j, 0))],
        out_specs=[pl.BlockSpec((tm, TN), lambda i, j: (i, jnp.minimum(j, last_slab_tile))),
                   pl.BlockSpec((tm, n_kcv), lambda i, j: (i, 0))],
        out_shape=[jax.ShapeDtypeStruct((M, N_SLAB), BF16),
                   jax.ShapeDtypeStruct((M, n_kcv), F32)],
        scratch_shapes=[pltpu.VMEM((tm, D), BF16), tab, tab, tab, tab],
        compiler_params=_cparams(("arbitrary", "arbitrary")),
        name="proj",
    )(x2d, ada3, ada3, g_pre.reshape(1, D), pos2d, jnp.asarray(inv_a), jnp.asarray(inv_b), w_slab_t)


_SRC_SIZES = (A_WIDTH,) + (A_KV_WIDTH,) * 6 + (A_HEADS * 3, A_WIDTH) + (B_WIDTH,) * 4 + (D_MODEL, D_MODEL)
_SRC = [int(v) for v in np.concatenate([[0], np.cumsum(_SRC_SIZES)])]
(SRC_QA, SRC_KC, SRC_VC, SRC_KS, SRC_VS, SRC_KW, SRC_VW, SRC_GA, SRC_ZA, SRC_QB, SRC_KB, SRC_VB,
 SRC_ZB, SRC_MA, SRC_MB, SRC_END) = _SRC


def _w_slab_kernel(w_ref, o_ref):
    cols = w_ref.shape[1]

    def put(dst, src, width):
        o_ref[dst:dst + width, :] = w_ref[src:src + width, :].astype(o_ref.dtype)

    def clear(dst, width):
        o_ref[dst:dst + width, :] = jnp.zeros((width, cols), o_ref.dtype)

    put(COL_QA, SRC_QA, A_WIDTH)
    put(COL_QB, SRC_QB, B_WIDTH)
    put(COL_KB, SRC_KB, B_WIDTH)
    put(COL_VB, SRC_VB, B_WIDTH)
    put(COL_ZA, SRC_ZA, A_WIDTH)
    put(COL_ZB, SRC_ZB, B_WIDTH)
    put(COL_MA, SRC_MA, D_MODEL)
    put(COL_MB, SRC_MB, D_MODEL)
    ngate = A_REP * 3
    for g in range(A_KV_GROUPS):
        d = g * LANES
        s = g * A_HEAD_DIM
        put(COL_KS + d, SRC_KS + s, A_HEAD_DIM)
        clear(COL_KS + d + A_HEAD_DIM, GATE_LANE - A_HEAD_DIM)
        put(COL_KS + d + GATE_LANE, SRC_GA + g * ngate, ngate)
        clear(COL_KS + d + GATE_LANE + ngate, LANES - GATE_LANE - ngate)
        for col, src in ((COL_KW, SRC_KW), (COL_VS, SRC_VS), (COL_VW, SRC_VW)):
            put(col + d, src + s, A_HEAD_DIM)
            clear(col + d + A_HEAD_DIM, LANES - A_HEAD_DIM)
        put(COL_KCV + d, SRC_KC + s, A_HEAD_DIM)
        put(COL_KCV + d + A_HEAD_DIM, SRC_VC + s, A_HEAD_DIM)
    clear(N_PROJ, o_ref.shape[0] - N_PROJ)


def _build_w_slab_t(w_in_t):
    n_in, D = w_in_t.shape
    assert n_in == SRC_END
    cols = 256
    n_rows = pl.cdiv(N_PROJ, TN) * TN
    return pl.pallas_call(
        _w_slab_kernel,
        grid=(D // cols,),
        in_specs=[pl.BlockSpec((n_in, cols), lambda i: (0, i))],
        out_specs=pl.BlockSpec((n_rows, cols), lambda i: (0, i)),
        out_shape=jax.ShapeDtypeStruct((n_rows, D), BF16),
        compiler_params=_cparams(("arbitrary",)),
        name="w_slab",
    )(w_in_t)


def _cmp_kv_kernel(t_ref, pe_ref, posc_ref, inva_ref, w1_ref, wk2_ref, wv2_ref, kc2_ref, vc2_ref):
    nblk = t_ref.shape[0] // CMP_STRIDE
    hidden2 = w1_ref.shape[-1]
    p_lo = jnp.zeros((nblk, hidden2), F32)
    p_hi = jnp.zeros((nblk, hidden2), F32)
    for l in range(CMP_STRIDE):
        x = t_ref[pl.ds(l, nblk, stride=CMP_STRIDE), :]
        p_lo += jnp.dot((x + pe_ref[l:l + 1, :]).astype(BF16), w1_ref[l],
                        preferred_element_type=F32)
        p_hi += jnp.dot((x + pe_ref[CMP_STRIDE + l:CMP_STRIDE + l + 1, :]).astype(BF16),
                        w1_ref[CMP_STRIDE + l], preferred_element_type=F32)
    hid = p_lo + pltpu.roll(p_hi, nblk - 1, 0)
    act = jax.nn.gelu(hid).astype(BF16)
    kk = jnp.dot(act[:, :CMP_HIDDEN], wk2_ref[...], preferred_element_type=F32)
    vv = jnp.dot(act[:, CMP_HIDDEN:], wv2_ref[...], preferred_element_type=F32)
    pos_f = posc_ref[...].astype(F32)
    cos_t, sin_t = _rope_tables(pos_f, inva_ref[...], A_HEAD_DIM, 8)
    kk = _rope_apply(kk, cos_t, sin_t, 8)
    lane = lax.broadcasted_iota(jnp.int32, kk.shape, 1)
    lo = lane < A_HEAD_DIM
    kc2_ref[0:nblk, :] = jnp.where(lo, kk, 0.0).astype(kc2_ref.dtype)
    kc2_ref[nblk:2 * nblk, :] = jnp.where(lo, 0.0, kk).astype(kc2_ref.dtype)
    vc2_ref[0:nblk, :] = jnp.where(lo, vv, 0.0).astype(vc2_ref.dtype)
    vc2_ref[nblk:2 * nblk, :] = jnp.where(lo, 0.0, vv).astype(vc2_ref.dtype)


def _cmp_kv(kcv, pe2, posc, w1, wk2d, wv2d, B, seq):
    G = A_KV_GROUPS
    nblk = seq // CMP_STRIDE
    inv_a, _ = _rope_inv_tables()
    full = lambda a: pl.BlockSpec(a.shape, lambda b, g: (0,) * a.ndim)
    out = jax.ShapeDtypeStruct((B, G, 2 * nblk, LANES), BF16)
    ospec = pl.BlockSpec((None, None, 2 * nblk, LANES), lambda b, g: (b, g, 0, 0))
    return pl.pallas_call(
        _cmp_kv_kernel,
        grid=(B, G),
        in_specs=[pl.BlockSpec((seq, LANES), lambda b, g: (b, g)),
                  full(pe2),
                  pl.BlockSpec((None, nblk, 1), lambda b, g: (b, 0, 0)),
                  pl.BlockSpec((1, LANES), lambda b, g: (0, 0)),
                  full(w1), full(wk2d), full(wv2d)],
        out_specs=[ospec, ospec],
        out_shape=[out, out],
        compiler_params=_cparams(("arbitrary", "arbitrary")),
        name="cmp_kv",
    )(kcv, pe2, posc, jnp.asarray(inv_a), w1, wk2d, wv2d)


def _build_cmp_w1(w_ck1, w_cv1):
    wk = w_ck1.reshape(CMP_BLOCK, A_HEAD_DIM, CMP_HIDDEN)
    wv = w_cv1.reshape(CMP_BLOCK, A_HEAD_DIM, CMP_HIDDEN)
    z = jnp.zeros_like(wk)
    return jnp.concatenate([jnp.concatenate([wk, z], axis=2),
                            jnp.concatenate([z, wv], axis=2)], axis=1).astype(BF16)


def _gate_expand_table(branch):
    e = np.zeros((LANES, A_REP * LANES), np.float32)
    for r in range(A_REP):
        e[GATE_LANE + 3 * r + branch, r * LANES:(r + 1) * LANES] = 1.0
    return e


def _branch_gates(gate_blk, e_ref):
    sg = jax.nn.sigmoid(gate_blk.astype(F32)).astype(BF16)
    return jnp.dot(sg, e_ref[...], preferred_element_type=F32)


def _overlap_table_t(seq):
    n_c = (seq - CMP_BLOCK) // CMP_STRIDE + 1
    n_s = seq // SEL_BLOCK
    cs = np.arange(n_c) * CMP_STRIDE
    ss = np.arange(n_s) * SEL_BLOCK
    ov = np.clip(np.minimum(cs[:, None] + CMP_BLOCK, ss[None, :] + SEL_BLOCK)
                 - np.maximum(cs[:, None], ss[None, :]), 0, None).astype(np.float32) / CMP_BLOCK
    full = np.zeros((LANES, LANES), np.float32)
    full[AUG_LANE:AUG_LANE + n_s, :n_c] = ov.T
    return full


def _split3(x):
    hi = x.astype(BF16)
    r1 = x - hi.astype(F32)
    mid = r1.astype(BF16)
    lo = (r1 - mid.astype(F32)).astype(BF16)
    return hi, mid, lo


def _compressed_branch(q_ref, kc2_ref, vc2_ref, ovt_ref, gates, ocmp_ref, qi, tq, n_sel_blocks):
    t = qi * tq + lax.broadcasted_iota(jnp.int32, (tq, LANES), 0)
    lane = lax.broadcasted_iota(jnp.int32, (tq, LANES), 1)
    cmask = (CMP_STRIDE * lane + (CMP_BLOCK - 1)) <= t
    cmask_f = cmask.astype(F32)
    nt = (((1,), (1,)), ((), ()))
    psum = jnp.zeros((tq, LANES), F32)
    for pair in range(A_REP // 2):
        qp = q_ref[:, pair * LANES:(pair + 1) * LANES]
        s2 = lax.dot_general(qp, kc2_ref[...], nt, preferred_element_type=F32)
        probs = []
        for hh in range(2):
            s = jnp.where(cmask, s2[:, hh * LANES:(hh + 1) * LANES], NEG)
            mx = jnp.max(s, axis=-1, keepdims=True)
            e = jnp.exp2(s - mx) * cmask_f
            pr = e / jnp.maximum(jnp.sum(e, axis=-1, keepdims=True), 1.0)
            probs.append(pr)
            psum = psum + pr
        p2 = jnp.concatenate(probs, axis=1).astype(BF16)
        o_pair = jnp.dot(p2, vc2_ref[...], preferred_element_type=F32)
        gate = jnp.where(lane < A_HEAD_DIM, gates[:, (2 * pair) * LANES:(2 * pair + 1) * LANES],
                         gates[:, (2 * pair + 1) * LANES:(2 * pair + 2) * LANES])
        ocmp_ref[:, pair * LANES:(pair + 1) * LANES] = (gate * o_pair).astype(ocmp_ref.dtype)

    ovt = ovt_ref[...]
    imp_t = sum(lax.dot_general(ovt, part, nt, preferred_element_type=F32) for part in _split3(psum))
    imp_t = imp_t[AUG_LANE:AUG_LANE + n_sel_blocks, :]
    jb = lax.broadcasted_iota(jnp.int32, imp_t.shape, 0)
    tcol = qi * tq + lax.broadcasted_iota(jnp.int32, imp_t.shape, 1)
    cur = tcol // SEL_BLOCK
    valid = (SEL_BLOCK * jb) <= tcol
    forced = valid & ((jb == 0) | (jb == cur) | (jb == cur - 1))
    score = jnp.where(valid, imp_t + FORCE_BONUS * forced.astype(F32), NEG)
    rank = jnp.zeros(imp_t.shape, F32)
    for i in range(n_sel_blocks):
        other = score[i:i + 1, :]
        beats = (other > score) | ((other == score) & (jb > i))
        rank = rank + beats.astype(F32)
    selb = jnp.where(rank < float(min(SEL_TOPK, n_sel_blocks)), 0.0, NEG)
    aug_t = jnp.concatenate([jnp.zeros((AUG_LANE, tq), F32), selb,
                             jnp.zeros((LANES - AUG_LANE - n_sel_blocks, tq), F32)], axis=0)
    return aug_t.T.astype(BF16)


def _flash_init(m_scr, acc_scr, l_scr=None):
    m_scr[...] = jnp.full(m_scr.shape, M_INIT, F32)
    acc_scr[...] = jnp.zeros(acc_scr.shape, F32)
    if l_scr is not None:
        l_scr[...] = jnp.zeros(l_scr.shape, F32)


def _scores(q, k):
    return lax.dot_general(q, k, (((1,), (1,)), ((), ())), preferred_element_type=F32)


def _add_bias(s, bias, reps):
    tq, tk = bias.shape
    return (s.reshape(reps, tq, tk) + bias[None]).reshape(reps * tq, tk)


def _flash_update(s, v, m_scr, acc_scr, l_scr=None, first=False):
    chunks = [s[:, c * LANES:(c + 1) * LANES] for c in range(s.shape[1] // LANES)]
    mx = functools.reduce(jnp.maximum, chunks)
    if first:
        assert l_scr is None
        m_new = jnp.broadcast_to(jnp.max(mx, axis=-1, keepdims=True), mx.shape)
        p = jnp.concatenate([jnp.exp2((ch - m_new).astype(v.dtype)) for ch in chunks], axis=1)
        acc_scr[...] = jnp.dot(p, v, preferred_element_type=F32)
        m_scr[...] = m_new
        return
    m_prev = m_scr[...]
    m_new = jnp.maximum(m_prev, jnp.max(mx, axis=-1, keepdims=True))
    alpha = jnp.exp2(m_prev - m_new)
    if l_scr is None:
        p = jnp.concatenate([jnp.exp2((ch - m_new).astype(v.dtype)) for ch in chunks], axis=1)
    else:
        ps = [jnp.exp2(ch - m_new) for ch in chunks]
        p = jnp.concatenate(ps, axis=1).astype(v.dtype)
        l_scr[...] = alpha * l_scr[...] + functools.reduce(jnp.add, ps)
    acc_scr[...] = alpha * acc_scr[...] + jnp.dot(p, v, preferred_element_type=F32)
    m_scr[...] = m_new


def _gated_head(acc, gates, r):
    lane = lax.broadcasted_iota(jnp.int32, acc.shape, 1)
    scale = gates[:, r * LANES:(r + 1) * LANES] / acc[:, ONES_LANE:ONES_LANE + 1]
    return jnp.where(lane < A_HEAD_DIM, acc * scale, 0.0)


def _store_head_pair(o_ref, pair, even, odd):
    o_ref[:, pair * LANES:(pair + 1) * LANES] = (even + pltpu.roll(odd, A_HEAD_DIM, 1)).astype(o_ref.dtype)


def _win_bias_table(t, span, window):
    delta = np.arange(t)[:, None] - np.arange(span)[None, :]
    dist = np.stack([n * t + delta for n in range(span // t)])
    return np.where((dist >= 0) & (dist < window), 0.0, NEG).astype(np.float32)


def _window_branch(q_heads, kw_ref, vw_ref, bias_ref, gates, owin_ref, m_scr, acc_scr, s0_scr, s1_scr,
                   qi, t, span):
    n_tiles = span // t
    start = pl.multiple_of(jnp.maximum((qi + 1) * t - span, 0), t)
    variant = jnp.minimum(qi, n_tiles - 1)
    q = jnp.concatenate(q_heads, axis=0)
    rows_of = lambda ref, w: ref[pl.ds(pl.multiple_of(start + w * t, t), t), :]
    bufs = (s0_scr, s1_scr)

    def score_into(w):
        bufs[w % 2][...] = _scores(q, rows_of(kw_ref, w))

    def fold(w):
        s = _add_bias(bufs[w % 2][...], bias_ref[variant, :, w * t:(w + 1) * t], A_REP)
        _flash_update(s, rows_of(vw_ref, w), m_scr, acc_scr, first=(w == 0))

    score_into(0)
    for w in range(n_tiles):
        if w + 1 < n_tiles:
            score_into(w + 1)
        fold(w)
    heads = [_gated_head(acc_scr[r * t:(r + 1) * t, :], gates, r) for r in range(A_REP)]
    for pair in range(A_REP // 2):
        _store_head_pair(owin_ref, pair, heads[2 * pair], heads[2 * pair + 1])


def _nsa_kernel(q_ref, kc2_ref, vc2_ref, ovt_ref, ks_ref, vs_ref, kw_ref, vw_ref, ecmp_ref, esel_ref,
                ewin_ref, bias_ref, ocmp_ref, osel_ref, owin_ref, qaug_scr, m_scr, acc_scr, s0_scr, s1_scr,
                mw_scr, accw_scr, sw0_scr, sw1_scr, *, t, span, n_sel_blocks):
    qi = pl.program_id(2)
    rows_of = lambda ref, kj: ref[pl.ds(pl.multiple_of(kj * t, t), t), :]
    gate_blk = rows_of(ks_ref, qi)
    lo = lax.broadcasted_iota(jnp.int32, (t, LANES), 1) < A_HEAD_DIM
    q_heads = []
    for r in range(A_REP):
        qp = q_ref[:, (r // 2) * LANES:(r // 2 + 1) * LANES]
        if r % 2:
            qp = jnp.concatenate([qp[:, A_HEAD_DIM:], qp[:, :A_HEAD_DIM]], axis=1)
        q_heads.append(qp)

    _window_branch([jnp.where(lo, qh, jnp.zeros_like(qh)) for qh in q_heads], kw_ref, vw_ref, bias_ref,
                   _branch_gates(gate_blk, ewin_ref), owin_ref, mw_scr, accw_scr, sw0_scr, sw1_scr, qi, t, span)
    aug = _compressed_branch(q_ref, kc2_ref, vc2_ref, ovt_ref, _branch_gates(gate_blk, ecmp_ref),
                             ocmp_ref, qi, t, n_sel_blocks)
    for r in range(A_REP):
        qaug_scr[r] = jnp.where(lo, q_heads[r], aug)

    q = qaug_scr[...].reshape(A_REP * t, LANES)
    _flash_init(m_scr, acc_scr)

    def score_into(buf, kj):
        buf[...] = _scores(q, rows_of(ks_ref, kj))

    def fold(buf, kj, causal=False):
        s = buf[...]
        if causal:
            delta = (lax.broadcasted_iota(jnp.int32, (t, t), 0)
                     - lax.broadcasted_iota(jnp.int32, (t, t), 1))
            s = _add_bias(s, jnp.where(delta >= 0, 0.0, NEG), A_REP)
        _flash_update(s, rows_of(vs_ref, kj), m_scr, acc_scr)

    first = qi % 2
    score_into(s1_scr, 0)
    score_into(s0_scr, first)

    @pl.when(first == 1)
    def _():
        fold(s1_scr, 0)

    def body(i, carry):
        kj = first + 2 * i
        score_into(s1_scr, kj + 1)
        fold(s0_scr, kj)
        score_into(s0_scr, kj + 2)
        fold(s1_scr, kj + 1)
        return carry
    lax.fori_loop(0, (qi - first) // 2, body, 0)

    fold(s0_scr, qi, causal=True)
    gates = _branch_gates(gate_blk, esel_ref)
    sel_heads = [_gated_head(acc_scr[r * t:(r + 1) * t, :], gates, r) for r in range(A_REP)]
    for pair in range(A_REP // 2):
        _store_head_pair(osel_ref, pair, sel_heads[2 * pair], sel_heads[2 * pair + 1])


def _nsa_attend(slab, kc2, vc2, B, seq):
    t = T_ATT
    nq = seq // t
    G = A_KV_GROUPS
    rows = A_REP * t
    span = min(((WIN_SIZE - 1 + t - 1) // t + 1) * t, seq)
    ovt = jnp.asarray(_overlap_table_t(seq), BF16)
    ecmp, esel, ewin = (jnp.asarray(_gate_expand_table(br), BF16) for br in range(3))
    bias = jnp.asarray(_win_bias_table(t, span, WIN_SIZE))
    kv = lambda col: pl.BlockSpec((seq, LANES), lambda b, g, q: (b, col // LANES + g))
    const = lambda a: pl.BlockSpec(a.shape, lambda b, g, q: (0,) * a.ndim)
    ckv = pl.BlockSpec((None, None) + kc2.shape[2:], lambda b, g, q: (b, g, 0, 0))
    out = jax.ShapeDtypeStruct((B * seq, A_WIDTH), BF16)
    ospec = pl.BlockSpec((t, A_REP * A_HEAD_DIM), lambda b, g, q: (b * nq + q, g))
    return pl.pallas_call(
        functools.partial(_nsa_kernel, t=t, span=span, n_sel_blocks=seq // SEL_BLOCK),
        grid=(B, G, nq),
        in_specs=[pl.BlockSpec((t, A_REP * A_HEAD_DIM), lambda b, g, q: (b * nq + q, g)),
                  ckv, ckv, const(ovt), kv(COL_KS), kv(COL_VS), kv(COL_KW), kv(COL_VW),
                  const(ecmp), const(esel), const(ewin), const(bias)],
        out_specs=[ospec, ospec, ospec],
        out_shape=[out, out, out],
        scratch_shapes=[pltpu.VMEM((A_REP, t, LANES), BF16)]
        + [pltpu.VMEM((rows, LANES), F32), pltpu.VMEM((rows, LANES), F32),
           pltpu.VMEM((rows, t), F32), pltpu.VMEM((rows, t), F32)] * 2,
        compiler_params=_cparams(("arbitrary", "arbitrary", "arbitrary")),
        name="nsa",
    )(slab, kc2, vc2, ovt, slab, slab, slab, slab, ecmp, esel, ewin, bias)


DILATIONS = ((128, 1), (512, 4), (2048, 16))
TQ_DIL = 512
TK_DIL = 256


def _dil_bias_tables(tq, tk, seq):
    w_near = max(w for w, r in DILATIONS if w < seq)
    near_keys = tq + -(-w_near // tk) * tk
    w_far, r_far = DILATIONS[-1]
    assert w_far >= seq and tq % r_far == 0 and tk % r_far == 0 and near_keys == 2 * tq
    assert tq % (2 * tk) == 0
    delta = np.arange(tq)[:, None] - np.arange(near_keys)[None, :]
    near = []
    for base in (near_keys - tq, 0):
        dist = base + delta
        mult = sum(((dist >= 0) & (dist <= w) & (dist % r == 0)).astype(np.float64) for w, r in DILATIONS)
        near.append(np.where(mult > 0, np.log2(np.maximum(mult, 1.0)), NEG))
    far = np.where(delta[:, :tk] % r_far == 0, 0.0, NEG)
    return np.stack(near).astype(np.float32), far.astype(np.float32)


def _dil_kernel(q_ref, k_ref, v_ref, near_ref, far_ref, o_ref, m_scr, acc_scr, l_scr, s0_scr, s1_scr,
                *, tq, tk):
    qi = pl.program_id(2)
    q = q_ref[...]
    near_keys = near_ref.shape[-1]
    _flash_init(m_scr, acc_scr, l_scr)
    update = functools.partial(_flash_update, m_scr=m_scr, acc_scr=acc_scr, l_scr=l_scr)
    rows_of = lambda ref, kj: ref[pl.ds(pl.multiple_of(kj * tk, tk), tk), :]
    clamped = (qi + 1) * tq < near_keys
    start = pl.multiple_of(jnp.maximum((qi + 1) * tq - near_keys, 0), tk)
    n_far = start // tk

    def score_into(buf, kj):
        buf[...] = _scores(q, rows_of(k_ref, kj))

    def fold(buf, kj):
        update(buf[...] + far_ref[...], rows_of(v_ref, kj))

    @pl.when(n_far > 0)
    def _():
        score_into(s0_scr, 0)

        def body(i, carry):
            kj = 2 * i
            score_into(s1_scr, kj + 1)
            fold(s0_scr, kj)
            score_into(s0_scr, kj + 2)
            fold(s1_scr, kj + 1)
            return carry
        lax.fori_loop(0, n_far // 2, body, 0)

    s = _scores(q, k_ref[pl.ds(start, near_keys), :]) + near_ref[clamped.astype(jnp.int32)]
    update(s, v_ref[pl.ds(start, near_keys), :])
    l = jnp.sum(l_scr[...], axis=-1, keepdims=True)
    o_ref[...] = (acc_scr[...] / l).astype(o_ref.dtype)


def _dil(slab, B, seq):
    tq, tk = TQ_DIL, TK_DIL
    nq = seq // tq
    H = B_HEADS
    near, far = (jnp.asarray(a) for a in _dil_bias_tables(tq, tk, seq))
    return pl.pallas_call(
        functools.partial(_dil_kernel, tq=tq, tk=tk),
        grid=(B, H, nq),
        in_specs=[pl.BlockSpec((tq, LANES), lambda b, h, q: (b * nq + q, COL_QB // LANES + h)),
                  pl.BlockSpec((seq, LANES), lambda b, h, q: (b, COL_KB // LANES + h)),
                  pl.BlockSpec((seq, LANES), lambda b, h, q: (b, COL_VB // LANES + h)),
                  pl.BlockSpec(near.shape, lambda b, h, q: (0, 0, 0)),
                  pl.BlockSpec(far.shape, lambda b, h, q: (0, 0))],
        out_specs=pl.BlockSpec((tq, LANES), lambda b, h, q: (b * nq + q, h)),
        out_shape=jax.ShapeDtypeStruct((B * seq, B_WIDTH), BF16),
        scratch_shapes=[pltpu.VMEM((tq, LANES), F32)] * 3 + [pltpu.VMEM((tq, tk), F32)] * 2,
        compiler_params=_cparams(("arbitrary", "arbitrary", "arbitrary")),
        name="dilated",
    )(slab, slab, slab, near, far)


def _silu(z):
    return z * jax.nn.sigmoid(z)


def _out_kernel(ocmp_ref, osel_ref, owin_ref, za_ref, ob_ref, zb_ref, ma_ref, mb_ref,
                x_ref, gate_ref, gpost_ref, wa_ref, wb_ref, wo_ref, o_ref):
    oa = ocmp_ref[...].astype(F32) + osel_ref[...].astype(F32) + owin_ref[...].astype(F32)
    a_in = (oa * _silu(za_ref[...].astype(F32))).astype(BF16)
    ya = jnp.dot(a_in, wa_ref[...], preferred_element_type=F32)
    b_in = (ob_ref[...].astype(F32) * _silu(zb_ref[...].astype(F32))).astype(BF16)
    yb = jnp.dot(b_in, wb_ref[...], preferred_element_type=F32)
    merged = (jax.nn.sigmoid(ma_ref[...].astype(F32)) * ya
              + jax.nn.sigmoid(mb_ref[...].astype(F32)) * yb)
    out = jnp.dot(merged.astype(BF16), wo_ref[...], preferred_element_type=F32)
    y = out * lax.rsqrt(jnp.mean(out * out, axis=-1, keepdims=True) + EPS) * gpost_ref[...]
    o_ref[...] = x_ref[...] + gate_ref[0] * y


def _out(ocmp, osel, owin, ob, slab, x2d, ada3, g_post, wa, wb, wo, seq):
    M, D = x2d.shape
    tm = 256
    per_b = seq // tm
    const = lambda a: pl.BlockSpec(a.shape, lambda i: (0,) * a.ndim, pipeline_mode=pl.Buffered(1))
    row = lambda w, cb: pl.BlockSpec((tm, w), lambda i: (i, cb))
    return pl.pallas_call(
        _out_kernel,
        grid=(M // tm,),
        in_specs=[row(A_WIDTH, 0), row(A_WIDTH, 0), row(A_WIDTH, 0),
                  row(A_WIDTH, COL_ZA // A_WIDTH), row(B_WIDTH, 0), row(B_WIDTH, COL_ZB // B_WIDTH),
                  row(D_MODEL, COL_MA // D_MODEL), row(D_MODEL, COL_MB // D_MODEL),
                  row(D, 0),
                  pl.BlockSpec((1, 1, D), lambda i: (i // per_b, 0, 2)),
                  pl.BlockSpec((1, D), lambda i: (0, 0)),
                  const(wa), const(wb), const(wo)],
        out_specs=pl.BlockSpec((tm, D), lambda i: (i, 0)),
        out_shape=jax.ShapeDtypeStruct((M, D), F32),
        compiler_params=_cparams(("arbitrary",)),
        name="out",
    )(ocmp, osel, owin, slab, ob, slab, slab, slab, x2d, ada3, g_post.reshape(1, D), wa, wb, wo)


def _layer(x, c, positions, w_ada, b_ada, g_pre, g_post, w_in, pe_ck, pe_cv, w_ck1, w_ck2,
           w_cv1, w_cv2, w_br_a, w_br_b, w_out):
    B, S, D = x.shape
    x2d = x.reshape(B * S, D)
    ada3 = _ada(c, w_ada, b_ada).reshape(B, 1, 3 * D)
    slab, kcv = _proj(x2d, ada3, g_pre, positions.reshape(B * S, 1), _build_w_slab_t(jnp.swapaxes(w_in, 0, 1)), S)

    nblk = S // CMP_STRIDE
    assert (S - CMP_BLOCK) // CMP_STRIDE + 1 == nblk - 1
    cmp_end = np.minimum(np.arange(nblk) * CMP_STRIDE + CMP_BLOCK - 1, S - 1)
    posc = positions[:, cmp_end].reshape(B, nblk, 1)
    dup = lambda w: jnp.concatenate([w, w], axis=1).astype(BF16)
    pe2 = jnp.concatenate([pe_ck, pe_cv], axis=1)
    kc2, vc2 = _cmp_kv(kcv, pe2, posc, _build_cmp_w1(w_ck1, w_cv1), dup(w_ck2), dup(w_cv2), B, S)

    ocmp, osel, owin = _nsa_attend(slab, kc2, vc2, B, S)
    ob = _dil(slab, B, S)
    out = _out(ocmp, osel, owin, ob, slab, x2d, ada3, g_post,
               w_br_a.astype(BF16), w_br_b.astype(BF16), w_out.astype(BF16), S)
    return out.reshape(B, S, D)


def kernel(x, c, positions, w_ada, b_ada, g_pre, g_post, w_in, pe_ck, pe_cv, w_ck1, w_ck2, w_cv1, w_cv2, w_br_a, w_br_b, w_out):
    h = x
    for layer in range(w_ada.shape[0]):
        h = _layer(h, c, positions, w_ada[layer], b_ada[layer], g_pre[layer], g_post[layer],
                   w_in[layer], pe_ck[layer], pe_cv[layer], w_ck1[layer], w_ck2[layer],
                   w_cv1[layer], w_cv2[layer], w_br_a[layer], w_br_b[layer], w_out[layer])
    return h
```

```python
import functools

import numpy as np
import jax
import jax.numpy as jnp
from jax import lax
from jax.experimental import pallas as pl
from jax.experimental.pallas import tpu as pltpu

F32 = jnp.float32
BF16 = jnp.bfloat16

D_MODEL = 2048
A_HEADS = 16
A_HEAD_DIM = 64
A_KV_GROUPS = 4
A_REP = A_HEADS // A_KV_GROUPS
A_WIDTH = A_HEADS * A_HEAD_DIM
A_KV_WIDTH = A_KV_GROUPS * A_HEAD_DIM
CMP_BLOCK = 32
CMP_STRIDE = 16
CMP_HIDDEN = 4 * A_HEAD_DIM
SEL_BLOCK = 64
SEL_TOPK = 16
WIN_SIZE = 512
FORCE_BONUS = 1.0e4
B_HEADS = 8
B_HEAD_DIM = 128
B_WIDTH = B_HEADS * B_HEAD_DIM
ROPE_THETA = 500000.0
EPS = 1e-6
NEG = -1e30
M_INIT = -1.0e38
LOG2E = 1.4426950408889634

LANES = 128
MXU_N = 256
VMEM_LIMIT = 56 * 1024 * 1024

TN = 1024
COL_QA = 0
COL_KS = 1024
COL_KW = 1536
COL_QB = 2048
COL_KB = 3072
COL_VB = 4096
COL_ZA = 5120
COL_ZB = 6144
COL_VS = 7168
COL_VW = 7680
COL_MA = 8192
COL_MB = 10240
N_SLAB = 12288
COL_KCV = N_SLAB
N_PROJ = N_SLAB + 2 * A_KV_WIDTH
GATE_LANE = 96
AUG_LANE = 64
ONES_LANE = 64

T_ATT = 256


def _cparams(sem):
    return pltpu.CompilerParams(dimension_semantics=sem, vmem_limit_bytes=VMEM_LIMIT)


def _ada_kernel(c_ref, w_ref, b_ref, o_ref):
    o_ref[...] = jnp.dot(c_ref[...], w_ref[...], preferred_element_type=F32,
                         precision=lax.Precision.HIGHEST) + b_ref[...]


def _ada(c, w_ada, b_ada):
    B, D = c.shape
    N = w_ada.shape[1]
    tn = 768
    return pl.pallas_call(
        _ada_kernel,
        grid=(N // tn,),
        in_specs=[pl.BlockSpec((B, D), lambda j: (0, 0)),
                  pl.BlockSpec((D, tn), lambda j: (0, j)),
                  pl.BlockSpec((1, tn), lambda j: (0, j))],
        out_specs=pl.BlockSpec((B, tn), lambda j: (0, j)),
        out_shape=jax.ShapeDtypeStruct((B, N), F32),
        compiler_params=_cparams(("arbitrary",)),
        name="ada",
    )(c, w_ada, b_ada.reshape(1, N))


def _rope_inv_tables():
    lane = np.arange(LANES)
    inv_a8 = (ROPE_THETA ** (-np.arange(0, 16, 2) / 16)).astype(np.float32)
    inv_b16 = (ROPE_THETA ** (-np.arange(0, 32, 2) / 32)).astype(np.float32)
    la = lane % A_HEAD_DIM
    inv_a = np.where(la < 16, inv_a8[la % 8], 0.0).astype(np.float32)
    inv_b = np.where(lane < 32, inv_b16[lane % 16], 0.0).astype(np.float32)
    return inv_a.reshape(1, LANES), inv_b.reshape(1, LANES)


def _rope_tables(pos_f, inv, period, half):
    ang = pos_f * inv
    c, s = jnp.cos(ang), jnp.sin(ang)
    lane = lax.broadcasted_iota(jnp.int32, ang.shape, 1) % period
    cos_t = jnp.where(lane < 2 * half, c, 1.0)
    sin_t = jnp.where(lane < half, -s, jnp.where(lane < 2 * half, s, 0.0))
    return cos_t, sin_t


def _rope_apply(x, cos_t, sin_t, half):
    lane = lax.broadcasted_iota(jnp.int32, x.shape, 1)
    partner = jnp.take_along_axis(x, lane ^ half, axis=1)
    return x * cos_t + partner * sin_t


def _proj_kernel(x_ref, shift_ref, scale_ref, gpre_ref, pos_ref, inva_ref, invb_ref, w_ref,
                 o_ref, kcv_ref, h_scr, ca, sa, cb, sb, *, tm, seq):
    i = pl.program_id(0)
    j = pl.program_id(1)

    @pl.when(j == 0)
    def _():
        x = x_ref[...]
        ms = jnp.mean(x * x, axis=-1, keepdims=True)
        y = x * lax.rsqrt(ms + EPS) * gpre_ref[...]
        h = y * (1.0 + scale_ref[0]) + shift_ref[0]
        h_scr[...] = h.astype(BF16)
        pos_f = pos_ref[...].astype(F32)
        ca[...], sa[...] = _rope_tables(pos_f, inva_ref[...], A_HEAD_DIM, 8)
        cb[...], sb[...] = _rope_tables(pos_f, invb_ref[...], LANES, 16)

    def tile(epilogues):
        for part, epilogue in enumerate(epilogues):
            c0 = part * MXU_N
            acc = lax.dot_general(h_scr[...], w_ref[c0:c0 + MXU_N, :], (((1,), (1,)), ((), ())),
                                  preferred_element_type=F32)
            epilogue(acc, c0)

    def rope_epilogue(tabs, half, mul, first_head_only, onehot):
        def epilogue(acc, c0):
            cos_t, sin_t = tabs[0][...], tabs[1][...]
            lane = lax.broadcasted_iota(jnp.int32, (tm, LANES), 1)
            if first_head_only:
                keep = lane < A_HEAD_DIM
                cos_t = jnp.where(keep, cos_t, 1.0)
                sin_t = jnp.where(keep, sin_t, 0.0)
            if onehot:
                t = (i * tm + lax.broadcasted_iota(jnp.int32, (tm, LANES), 0)) % seq
                hot = (lane >= AUG_LANE) & (lane < AUG_LANE + seq // SEL_BLOCK) & \
                      ((t // SEL_BLOCK) == (lane - AUG_LANE))
            for cidx in range(MXU_N // LANES):
                xc = acc[:, cidx * LANES:(cidx + 1) * LANES]
                r = _rope_apply(xc, cos_t, sin_t, half)
                if mul != 1.0:
                    r = r * mul
                if onehot:
                    r = jnp.where(hot, 1.0, r)
                o_ref[:, c0 + cidx * LANES:c0 + (cidx + 1) * LANES] = r.astype(o_ref.dtype)
        return epilogue

    def plain_epilogue(acc, c0):
        o_ref[:, c0:c0 + MXU_N] = acc.astype(o_ref.dtype)

    def ones_lane_epilogue(acc, c0):
        lane = lax.broadcasted_iota(jnp.int32, acc.shape, 1) % LANES
        o_ref[:, c0:c0 + MXU_N] = jnp.where(lane == ONES_LANE, 1.0, acc).astype(o_ref.dtype)

    def kcv_epilogue(acc, c0):
        kcv_ref[:, c0:c0 + MXU_N] = acc

    ta = (ca, sa)
    tb = (cb, sb)
    parts = lambda col0, col1, epi: [((col0 // MXU_N + n), epi) for n in range((col1 - col0) // MXU_N)]
    kinds = dict(
        parts(COL_QA, COL_KS, rope_epilogue(ta, 8, A_HEAD_DIM ** -0.5 * LOG2E, False, False))
        + parts(COL_KS, COL_KW, rope_epilogue(ta, 8, 1.0, True, True))
        + parts(COL_KW, COL_QB, rope_epilogue(ta, 8, 1.0, True, False))
        + parts(COL_QB, COL_KB, rope_epilogue(tb, 16, B_HEAD_DIM ** -0.5 * LOG2E, False, False))
        + parts(COL_KB, COL_VB, rope_epilogue(tb, 16, 1.0, False, False))
        + parts(COL_VB, COL_VS, plain_epilogue)
        + parts(COL_VS, COL_MA, ones_lane_epilogue)
        + parts(COL_MA, COL_KCV, plain_epilogue)
        + parts(COL_KCV, N_PROJ, None))
    per_tile = TN // MXU_N
    steps = {}
    for jt in range(pl.cdiv(N_PROJ, TN)):
        tile_kinds = tuple(kinds[p] for p in range(jt * per_tile, (jt + 1) * per_tile) if p in kinds)
        if tile_kinds[0] is None:
            assert all(k is None for k in tile_kinds) and len(tile_kinds) * MXU_N == kcv_ref.shape[1]
            tile_kinds = (kcv_epilogue,) * len(tile_kinds)
        steps.setdefault(tile_kinds, []).append(jt)
    for tile_kinds, jts in steps.items():
        pl.when(functools.reduce(jnp.logical_or, [j == jt for jt in jts]))(
            functools.partial(tile, tile_kinds))


def _proj(x2d, ada3, g_pre, pos2d, w_slab_t, seq):
    M, D = x2d.shape
    tm = 1024
    per_b = seq // tm
    inv_a, inv_b = _rope_inv_tables()
    tab = pltpu.VMEM((tm, LANES), F32)
    last_slab_tile = N_SLAB // TN - 1
    n_kcv = N_PROJ - N_SLAB
    return pl.pallas_call(
        functools.partial(_proj_kernel, tm=tm, seq=seq),
        grid=(M // tm, pl.cdiv(N_PROJ, TN)),
        in_specs=[pl.BlockSpec((tm, D), lambda i, j: (i, 0)),
                  pl.BlockSpec((1, 1, D), lambda i, j: (i // per_b, 0, 0)),
                  pl.BlockSpec((1, 1, D), lambda i, j: (i // per_b, 0, 1)),
                  pl.BlockSpec((1, D), lambda i, j: (0, 0)),
                  pl.BlockSpec((tm, 1), lambda i, j: (i, 0)),
                  pl.BlockSpec((1, LANES), lambda i, j: (0, 0)),
                  pl.BlockSpec((1, LANES), lambda i, j: (0, 0)),
                  pl.BlockSpec((TN, D), lambda i, j: (j, 0))],
        out_specs=[pl.BlockSpec((tm, TN), lambda i, j: (i, jnp.minimum(j, last_slab_tile))),
                   pl.BlockSpec((tm, n_kcv), lambda i, j: (i, 0))],
        out_shape=[jax.ShapeDtypeStruct((M, N_SLAB), BF16),
                   jax.ShapeDtypeStruct((M, n_kcv), F32)],
        scratch_shapes=[pltpu.VMEM((tm, D), BF16), tab, tab, tab, tab],
        compiler_params=_cparams(("arbitrary", "arbitrary")),
        name="proj",
    )(x2d, ada3, ada3, g_pre.reshape(1, D), pos2d, jnp.asarray(inv_a), jnp.asarray(inv_b), w_slab_t)


_SRC_SIZES = (A_WIDTH,) + (A_KV_WIDTH,) * 6 + (A_HEADS * 3, A_WIDTH) + (B_WIDTH,) * 4 + (D_MODEL, D_MODEL)
_SRC = [int(v) for v in np.concatenate([[0], np.cumsum(_SRC_SIZES)])]
(SRC_QA, SRC_KC, SRC_VC, SRC_KS, SRC_VS, SRC_KW, SRC_VW, SRC_GA, SRC_ZA, SRC_QB, SRC_KB, SRC_VB,
 SRC_ZB, SRC_MA, SRC_MB, SRC_END) = _SRC


def _w_slab_kernel(w_ref, o_ref):
    cols = w_ref.shape[1]

    def put(dst, src, width):
        o_ref[dst:dst + width, :] = w_ref[src:src + width, :].astype(o_ref.dtype)

    def clear(dst, width):
        o_ref[dst:dst + width, :] = jnp.zeros((width, cols), o_ref.dtype)

    put(COL_QA, SRC_QA, A_WIDTH)
    put(COL_QB, SRC_QB, B_WIDTH)
    put(COL_KB, SRC_KB, B_WIDTH)
    put(COL_VB, SRC_VB, B_WIDTH)
    put(COL_ZA, SRC_ZA, A_WIDTH)
    put(COL_ZB, SRC_ZB, B_WIDTH)
    put(COL_MA, SRC_MA, D_MODEL)
    put(COL_MB, SRC_MB, D_MODEL)
    ngate = A_REP * 3
    for g in range(A_KV_GROUPS):
        d = g * LANES
        s = g * A_HEAD_DIM
        put(COL_KS + d, SRC_KS + s, A_HEAD_DIM)
        clear(COL_KS + d + A_HEAD_DIM, GATE_LANE - A_HEAD_DIM)
        put(COL_KS + d + GATE_LANE, SRC_GA + g * ngate, ngate)
        clear(COL_KS + d + GATE_LANE + ngate, LANES - GATE_LANE - ngate)
        for col, src in ((COL_KW, SRC_KW), (COL_VS, SRC_VS), (COL_VW, SRC_VW)):
            put(col + d, src + s, A_HEAD_DIM)
            clear(col + d + A_HEAD_DIM, LANES - A_HEAD_DIM)
        put(COL_KCV + d, SRC_KC + s, A_HEAD_DIM)
        put(COL_KCV + d + A_HEAD_DIM, SRC_VC + s, A_HEAD_DIM)
    clear(N_PROJ, o_ref.shape[0] - N_PROJ)


def _build_w_slab_t(w_in_t):
    n_in, D = w_in_t.shape
    assert n_in == SRC_END
    cols = 256
    n_rows = pl.cdiv(N_PROJ, TN) * TN
    return pl.pallas_call(
        _w_slab_kernel,
        grid=(D // cols,),
        in_specs=[pl.BlockSpec((n_in, cols), lambda i: (0, i))],
        out_specs=pl.BlockSpec((n_rows, cols), lambda i: (0, i)),
        out_shape=jax.ShapeDtypeStruct((n_rows, D), BF16),
        compiler_params=_cparams(("arbitrary",)),
        name="w_slab",
    )(w_in_t)


def _cmp_kv_kernel(*refs):
    groups = A_KV_GROUPS
    t_refs = refs[:groups]
    pe_ref, posc_ref, inva_ref, w1_ref, wk2_ref, wv2_ref, kc2_ref, vc2_ref = refs[groups:]
    nblk = t_refs[0].shape[0] // CMP_STRIDE
    rows = groups * nblk
    hidden2 = w1_ref.shape[-1]
    p_lo = jnp.zeros((rows, hidden2), F32)
    p_hi = jnp.zeros((rows, hidden2), F32)
    def tokens(l):
        return jnp.concatenate([t_ref[pl.ds(l, nblk, stride=CMP_STRIDE), :] for t_ref in t_refs], axis=0)

    def pair(x0, x1, l):
        xs = jnp.concatenate([x0 + pe_ref[l:l + 1, :], x1 + pe_ref[l + 1:l + 2, :]], axis=1)
        return jnp.dot(xs.astype(BF16), w1_ref[l:l + 2].reshape(2 * LANES, hidden2),
                       preferred_element_type=F32)

    for l in range(0, CMP_STRIDE, 2):
        x0, x1 = tokens(l), tokens(l + 1)
        p_lo += pair(x0, x1, l)
        p_hi += pair(x0, x1, CMP_STRIDE + l)
    hid = p_lo + pltpu.roll(p_hi, rows - 1, 0)
    act = jax.nn.gelu(hid).astype(BF16)
    kk = jnp.dot(act[:, :CMP_HIDDEN], wk2_ref[...], preferred_element_type=F32)
    vv = jnp.dot(act[:, CMP_HIDDEN:], wv2_ref[...], preferred_element_type=F32)
    pos_f = jnp.concatenate([posc_ref[...]] * groups, axis=0).astype(F32)
    cos_t, sin_t = _rope_tables(pos_f, inva_ref[...], A_HEAD_DIM, 8)
    kk = _rope_apply(kk, cos_t, sin_t, 8)
    lane = lax.broadcasted_iota(jnp.int32, (nblk, LANES), 1)
    lo = lane < A_HEAD_DIM
    for g in range(groups):
        kg = kk[g * nblk:(g + 1) * nblk]
        vg = vv[g * nblk:(g + 1) * nblk]
        kc2_ref[g, 0:nblk, :] = jnp.where(lo, kg, 0.0).astype(kc2_ref.dtype)
        kc2_ref[g, nblk:2 * nblk, :] = jnp.where(lo, 0.0, kg).astype(kc2_ref.dtype)
        vc2_ref[g, 0:nblk, :] = jnp.where(lo, vg, 0.0).astype(vc2_ref.dtype)
        vc2_ref[g, nblk:2 * nblk, :] = jnp.where(lo, 0.0, vg).astype(vc2_ref.dtype)


def _cmp_kv(kcv, pe2, posc, w1, wk2d, wv2d, B, seq):
    G = A_KV_GROUPS
    nblk = seq // CMP_STRIDE
    inv_a, _ = _rope_inv_tables()
    full = lambda a: pl.BlockSpec(a.shape, lambda b: (0,) * a.ndim)
    out = jax.ShapeDtypeStruct((B, G, 2 * nblk, LANES), BF16)
    ospec = pl.BlockSpec((None, G, 2 * nblk, LANES), lambda b: (b, 0, 0, 0))
    return pl.pallas_call(
        _cmp_kv_kernel,
        grid=(B,),
        in_specs=[pl.BlockSpec((seq, LANES), lambda b, g=g: (b, g)) for g in range(G)] + [
                  full(pe2),
                  pl.BlockSpec((None, nblk, 1), lambda b: (b, 0, 0)),
                  pl.BlockSpec((1, LANES), lambda b: (0, 0)),
                  full(w1), full(wk2d), full(wv2d)],
        out_specs=[ospec, ospec],
        out_shape=[out, out],
        compiler_params=_cparams(("arbitrary",)),
        name="cmp_kv",
    )(*([kcv] * G), pe2, posc, jnp.asarray(inv_a), w1, wk2d, wv2d)


def _build_cmp_w1(w_ck1, w_cv1):
    wk = w_ck1.reshape(CMP_BLOCK, A_HEAD_DIM, CMP_HIDDEN)
    wv = w_cv1.reshape(CMP_BLOCK, A_HEAD_DIM, CMP_HIDDEN)
    z = jnp.zeros_like(wk)
    return jnp.concatenate([jnp.concatenate([wk, z], axis=2),
                            jnp.concatenate([z, wv], axis=2)], axis=1).astype(BF16)


def _gate_expand_table(branch):
    e = np.zeros((LANES, A_REP * LANES), np.float32)
    for r in range(A_REP):
        e[GATE_LANE + 3 * r + branch, r * LANES:(r + 1) * LANES] = 1.0
    return e


def _branch_gates(gate_blk, e_ref):
    sg = jax.nn.sigmoid(gate_blk.astype(F32)).astype(BF16)
    return jnp.dot(sg, e_ref[...], preferred_element_type=F32)


def _overlap_table_t(seq):
    n_c = (seq - CMP_BLOCK) // CMP_STRIDE + 1
    n_s = seq // SEL_BLOCK
    cs = np.arange(n_c) * CMP_STRIDE
    ss = np.arange(n_s) * SEL_BLOCK
    ov = np.clip(np.minimum(cs[:, None] + CMP_BLOCK, ss[None, :] + SEL_BLOCK)
                 - np.maximum(cs[:, None], ss[None, :]), 0, None).astype(np.float32) / CMP_BLOCK
    full = np.zeros((LANES, LANES), np.float32)
    full[AUG_LANE:AUG_LANE + n_s, :n_c] = ov.T
    return full


def _split3(x):
    hi = x.astype(BF16)
    r1 = x - hi.astype(F32)
    mid = r1.astype(BF16)
    lo = (r1 - mid.astype(F32)).astype(BF16)
    return hi, mid, lo


def _compressed_branch(q_ref, kc2_ref, vc2_ref, ovt_ref, gates, ocmp_ref, qi, tq, n_sel_blocks):
    t = qi * tq + lax.broadcasted_iota(jnp.int32, (tq, LANES), 0)
    lane = lax.broadcasted_iota(jnp.int32, (tq, LANES), 1)
    cmask = (CMP_STRIDE * lane + (CMP_BLOCK - 1)) <= t
    cmask_f = cmask.astype(F32)
    nt = (((1,), (1,)), ((), ()))
    psum = jnp.zeros((tq, LANES), F32)
    for pair in range(A_REP // 2):
        qp = q_ref[:, pair * LANES:(pair + 1) * LANES]
        s2 = lax.dot_general(qp, kc2_ref[...], nt, preferred_element_type=F32)
        probs = []
        for hh in range(2):
            s = jnp.where(cmask, s2[:, hh * LANES:(hh + 1) * LANES], NEG)
            mx = jnp.max(s, axis=-1, keepdims=True)
            e = jnp.exp2(s - mx) * cmask_f
            pr = e / jnp.maximum(jnp.sum(e, axis=-1, keepdims=True), 1.0)
            probs.append(pr)
            psum = psum + pr
        p2 = jnp.concatenate(probs, axis=1).astype(BF16)
        o_pair = jnp.dot(p2, vc2_ref[...], preferred_element_type=F32)
        gate = jnp.where(lane < A_HEAD_DIM, gates[:, (2 * pair) * LANES:(2 * pair + 1) * LANES],
                         gates[:, (2 * pair + 1) * LANES:(2 * pair + 2) * LANES])
        ocmp_ref[:, pair * LANES:(pair + 1) * LANES] = (gate * o_pair).astype(ocmp_ref.dtype)

    ovt = ovt_ref[...]
    imp_t = sum(lax.dot_general(ovt, part, nt, preferred_element_type=F32) for part in _split3(psum))
    imp_t = imp_t[AUG_LANE:AUG_LANE + n_sel_blocks, :]
    jb = lax.broadcasted_iota(jnp.int32, imp_t.shape, 0)
    tcol = qi * tq + lax.broadcasted_iota(jnp.int32, imp_t.shape, 1)
    cur = tcol // SEL_BLOCK
    valid = (SEL_BLOCK * jb) <= tcol
    forced = valid & ((jb == 0) | (jb == cur) | (jb == cur - 1))
    score = jnp.where(valid, imp_t + FORCE_BONUS * forced.astype(F32), NEG)
    rank = jnp.zeros(imp_t.shape, F32)
    for i in range(n_sel_blocks):
        other = score[i:i + 1, :]
        beats = (other > score) | ((other == score) & (jb > i))
        rank = rank + beats.astype(F32)
    selb = jnp.where(rank < float(min(SEL_TOPK, n_sel_blocks)), 0.0, NEG)
    aug_t = jnp.concatenate([jnp.zeros((AUG_LANE, tq), F32), selb,
                             jnp.zeros((LANES - AUG_LANE - n_sel_blocks, tq), F32)], axis=0)
    return aug_t.T.astype(BF16)


def _flash_init(m_scr, acc_scr, l_scr=None):
    m_scr[...] = jnp.full(m_scr.shape, M_INIT, F32)
    acc_scr[...] = jnp.zeros(acc_scr.shape, F32)
    if l_scr is not None:
        l_scr[...] = jnp.zeros(l_scr.shape, F32)


def _scores(q, k):
    return lax.dot_general(q, k, (((1,), (1,)), ((), ())), preferred_element_type=F32)


def _add_bias(s, bias, reps):
    tq, tk = bias.shape
    return (s.reshape(reps, tq, tk) + bias[None]).reshape(reps * tq, tk)


def _flash_update(s, v, m_scr, acc_scr, l_scr=None, first=False):
    chunks = [s[:, c * LANES:(c + 1) * LANES] for c in range(s.shape[1] // LANES)]
    mx = functools.reduce(jnp.maximum, chunks)
    if first:
        assert l_scr is None
        m_new = jnp.broadcast_to(jnp.max(mx, axis=-1, keepdims=True), mx.shape)
        p = jnp.concatenate([jnp.exp2((ch - m_new).astype(v.dtype)) for ch in chunks], axis=1)
        acc_scr[...] = jnp.dot(p, v, preferred_element_type=F32)
        m_scr[...] = m_new
        return
    m_prev = m_scr[...]
    m_new = jnp.maximum(m_prev, jnp.max(mx, axis=-1, keepdims=True))
    alpha = jnp.exp2(m_prev - m_new)
    if l_scr is None:
        p = jnp.concatenate([jnp.exp2((ch - m_new).astype(v.dtype)) for ch in chunks], axis=1)
    else:
        ps = [jnp.exp2(ch - m_new) for ch in chunks]
        p = jnp.concatenate(ps, axis=1).astype(v.dtype)
        l_scr[...] = alpha * l_scr[...] + functools.reduce(jnp.add, ps)
    acc_scr[...] = alpha * acc_scr[...] + jnp.dot(p, v, preferred_element_type=F32)
    m_scr[...] = m_new


def _gated_head(acc, gates, r):
    lane = lax.broadcasted_iota(jnp.int32, acc.shape, 1)
    scale = gates[:, r * LANES:(r + 1) * LANES] / acc[:, ONES_LANE:ONES_LANE + 1]
    return jnp.where(lane < A_HEAD_DIM, acc * scale, 0.0)


def _store_head_pair(o_ref, pair, even, odd):
    o_ref[:, pair * LANES:(pair + 1) * LANES] = (even + pltpu.roll(odd, A_HEAD_DIM, 1)).astype(o_ref.dtype)


def _win_bias_table(t, span, window):
    delta = np.arange(t)[:, None] - np.arange(span)[None, :]
    dist = np.stack([n * t + delta for n in range(span // t)])
    return np.where((dist >= 0) & (dist < window), 0.0, NEG).astype(np.float32)


def _window_branch(q_heads, kw_ref, vw_ref, bias_ref, gates, owin_ref, m_scr, acc_scr, s0_scr, s1_scr,
                   qi, t, span):
    n_tiles = span // t
    start = pl.multiple_of(jnp.maximum((qi + 1) * t - span, 0), t)
    variant = jnp.minimum(qi, n_tiles - 1)
    q = jnp.concatenate(q_heads, axis=0)
    rows_of = lambda ref, w: ref[pl.ds(pl.multiple_of(start + w * t, t), t), :]
    bufs = (s0_scr, s1_scr)

    def score_into(w):
        bufs[w % 2][...] = _scores(q, rows_of(kw_ref, w))

    def fold(w):
        s = _add_bias(bufs[w % 2][...], bias_ref[variant, :, w * t:(w + 1) * t], A_REP)
        _flash_update(s, rows_of(vw_ref, w), m_scr, acc_scr, first=(w == 0))

    score_into(0)
    for w in range(n_tiles):
        if w + 1 < n_tiles:
            score_into(w + 1)
        fold(w)
    heads = [_gated_head(acc_scr[r * t:(r + 1) * t, :], gates, r) for r in range(A_REP)]
    for pair in range(A_REP // 2):
        _store_head_pair(owin_ref, pair, heads[2 * pair], heads[2 * pair + 1])


def _nsa_kernel(q_ref, kc2_ref, vc2_ref, ovt_ref, ks_ref, vs_ref, kw_ref, vw_ref, ecmp_ref, esel_ref,
                ewin_ref, bias_ref, ocmp_ref, osel_ref, owin_ref, qaug_scr, m_scr, acc_scr, s0_scr, s1_scr,
                mw_scr, accw_scr, sw0_scr, sw1_scr, *, t, span, n_sel_blocks):
    qi = pl.program_id(2)
    rows_of = lambda ref, kj: ref[pl.ds(pl.multiple_of(kj * t, t), t), :]
    gate_blk = rows_of(ks_ref, qi)
    lo = lax.broadcasted_iota(jnp.int32, (t, LANES), 1) < A_HEAD_DIM
    q_heads = []
    for r in range(A_REP):
        qp = q_ref[:, (r // 2) * LANES:(r // 2 + 1) * LANES]
        if r % 2:
            qp = jnp.concatenate([qp[:, A_HEAD_DIM:], qp[:, :A_HEAD_DIM]], axis=1)
        q_heads.append(qp)

    _window_branch([jnp.where(lo, qh, jnp.zeros_like(qh)) for qh in q_heads], kw_ref, vw_ref, bias_ref,
                   _branch_gates(gate_blk, ewin_ref), owin_ref, mw_scr, accw_scr, sw0_scr, sw1_scr, qi, t, span)
    aug = _compressed_branch(q_ref, kc2_ref, vc2_ref, ovt_ref, _branch_gates(gate_blk, ecmp_ref),
                             ocmp_ref, qi, t, n_sel_blocks)
    for r in range(A_REP):
        qaug_scr[r] = jnp.where(lo, q_heads[r], aug)

    q = qaug_scr[...].reshape(A_REP * t, LANES)
    _flash_init(m_scr, acc_scr)

    def score_into(buf, kj):
        buf[...] = _scores(q, rows_of(ks_ref, kj))

    def fold(buf, kj, causal=False):
        s = buf[...]
        if causal:
            delta = (lax.broadcasted_iota(jnp.int32, (t, t), 0)
                     - lax.broadcasted_iota(jnp.int32, (t, t), 1))
            s = _add_bias(s, jnp.where(delta >= 0, 0.0, NEG), A_REP)
        _flash_update(s, rows_of(vs_ref, kj), m_scr, acc_scr)

    first = qi % 2
    score_into(s1_scr, 0)
    score_into(s0_scr, first)

    @pl.when(first == 1)
    def _():
        fold(s1_scr, 0)

    def body(i, carry):
        kj = first + 2 * i
        score_into(s1_scr, kj + 1)
        fold(s0_scr, kj)
        score_into(s0_scr, kj + 2)
        fold(s1_scr, kj + 1)
        return carry
    lax.fori_loop(0, (qi - first) // 2, body, 0)

    fold(s0_scr, qi, causal=True)
    gates = _branch_gates(gate_blk, esel_ref)
    sel_heads = [_gated_head(acc_scr[r * t:(r + 1) * t, :], gates, r) for r in range(A_REP)]
    for pair in range(A_REP // 2):
        _store_head_pair(osel_ref, pair, sel_heads[2 * pair], sel_heads[2 * pair + 1])


def _nsa_attend(slab, kc2, vc2, B, seq):
    t = T_ATT
    nq = seq // t
    G = A_KV_GROUPS
    rows = A_REP * t
    span = min(((WIN_SIZE - 1 + t - 1) // t + 1) * t, seq)
    ovt = jnp.asarray(_overlap_table_t(seq), BF16)
    ecmp, esel, ewin = (jnp.asarray(_gate_expand_table(br), BF16) for br in range(3))
    bias = jnp.asarray(_win_bias_table(t, span, WIN_SIZE))
    kv = lambda col: pl.BlockSpec((seq, LANES), lambda b, g, q: (b, col // LANES + g))
    const = lambda a: pl.BlockSpec(a.shape, lambda b, g, q: (0,) * a.ndim)
    ckv = pl.BlockSpec((None, None) + kc2.shape[2:], lambda b, g, q: (b, g, 0, 0))
    out = jax.ShapeDtypeStruct((B * seq, A_WIDTH), BF16)
    ospec = pl.BlockSpec((t, A_REP * A_HEAD_DIM), lambda b, g, q: (b * nq + q, g))
    return pl.pallas_call(
        functools.partial(_nsa_kernel, t=t, span=span, n_sel_blocks=seq // SEL_BLOCK),
        grid=(B, G, nq),
        in_specs=[pl.BlockSpec((t, A_REP * A_HEAD_DIM), lambda b, g, q: (b * nq + q, g)),
                  ckv, ckv, const(ovt), kv(COL_KS), kv(COL_VS), kv(COL_KW), kv(COL_VW),
                  const(ecmp), const(esel), const(ewin), const(bias)],
        out_specs=[ospec, ospec, ospec],
        out_shape=[out, out, out],
        scratch_shapes=[pltpu.VMEM((A_REP, t, LANES), BF16)]
        + [pltpu.VMEM((rows, LANES), F32), pltpu.VMEM((rows, LANES), F32),
           pltpu.VMEM((rows, t), F32), pltpu.VMEM((rows, t), F32)] * 2,
        compiler_params=_cparams(("arbitrary", "arbitrary", "arbitrary")),
        name="nsa",
    )(slab, kc2, vc2, ovt, slab, slab, slab, slab, ecmp, esel, ewin, bias)


DILATIONS = ((128, 1), (512, 4), (2048, 16))
TQ_DIL = 512
TK_DIL = 256


def _dil_bias_tables(tq, tk, seq):
    w_near = max(w for w, r in DILATIONS if w < seq)
    near_keys = tq + -(-w_near // tk) * tk
    w_far, r_far = DILATIONS[-1]
    assert w_far >= seq and tq % r_far == 0 and tk % r_far == 0 and near_keys == 2 * tq
    assert tq % (2 * tk) == 0
    delta = np.arange(tq)[:, None] - np.arange(near_keys)[None, :]
    near = []
    for base in (near_keys - tq, 0):
        dist = base + delta
        mult = sum(((dist >= 0) & (dist <= w) & (dist % r == 0)).astype(np.float64) for w, r in DILATIONS)
        near.append(np.where(mult > 0, np.log2(np.maximum(mult, 1.0)), NEG))
    far = np.where(delta[:, :tk] % r_far == 0, 0.0, NEG)
    return np.stack(near).astype(np.float32), far.astype(np.float32)


def _dil_kernel(q_ref, k_ref, v_ref, near_ref, far_ref, o_ref, m_scr, acc_scr, l_scr, s0_scr, s1_scr,
                *, tq, tk):
    qi = pl.program_id(2)
    q = q_ref[...]
    near_keys = near_ref.shape[-1]
    _flash_init(m_scr, acc_scr, l_scr)
    update = functools.partial(_flash_update, m_scr=m_scr, acc_scr=acc_scr, l_scr=l_scr)
    rows_of = lambda ref, kj: ref[pl.ds(pl.multiple_of(kj * tk, tk), tk), :]
    variant = ((qi + 1) * tq < near_keys).astype(jnp.int32)
    n_far = jnp.maximum((qi + 1) * tq - near_keys, 0) // tk
    bufs = (s0_scr, s1_scr)

    def score_into(buf, kj):
        buf[...] = _scores(q, rows_of(k_ref, kj))

    def fold(buf, kj):
        update(buf[...] + far_ref[...], rows_of(v_ref, kj))

    score_into(s0_scr, 0)

    def body(i, carry):
        kj = 2 * i
        score_into(s1_scr, kj + 1)
        fold(s0_scr, kj)
        score_into(s0_scr, kj + 2)
        fold(s1_scr, kj + 1)
        return carry
    lax.fori_loop(0, n_far // 2, body, 0)

    n_near = near_keys // tk
    for w in range(n_near):
        if w + 1 < n_near:
            score_into(bufs[(w + 1) % 2], n_far + w + 1)
        update(bufs[w % 2][...] + near_ref[variant, :, w * tk:(w + 1) * tk], rows_of(v_ref, n_far + w))
    l = jnp.sum(l_scr[...], axis=-1, keepdims=True)
    o_ref[...] = (acc_scr[...] / l).astype(o_ref.dtype)


def _dil(slab, B, seq):
    tq, tk = TQ_DIL, TK_DIL
    nq = seq // tq
    H = B_HEADS
    near, far = (jnp.asarray(a) for a in _dil_bias_tables(tq, tk, seq))
    return pl.pallas_call(
        functools.partial(_dil_kernel, tq=tq, tk=tk),
        grid=(B, H, nq),
        in_specs=[pl.BlockSpec((tq, LANES), lambda b, h, q: (b * nq + q, COL_QB // LANES + h)),
                  pl.BlockSpec((seq, LANES), lambda b, h, q: (b, COL_KB // LANES + h)),
                  pl.BlockSpec((seq, LANES), lambda b, h, q: (b, COL_VB // LANES + h)),
                  pl.BlockSpec(near.shape, lambda b, h, q: (0, 0, 0)),
                  pl.BlockSpec(far.shape, lambda b, h, q: (0, 0))],
        out_specs=pl.BlockSpec((tq, LANES), lambda b, h, q: (b * nq + q, h)),
        out_shape=jax.ShapeDtypeStruct((B * seq, B_WIDTH), BF16),
        scratch_shapes=[pltpu.VMEM((tq, LANES), F32)] * 3 + [pltpu.VMEM((tq, tk), F32)] * 2,
        compiler_params=_cparams(("arbitrary", "arbitrary", "arbitrary")),
        name="dilated",
    )(slab, slab, slab, near, far)


def _silu(z):
    return z * jax.nn.sigmoid(z)


def _out_kernel(ocmp_ref, osel_ref, owin_ref, za_ref, ob_ref, zb_ref, ma_ref, mb_ref,
                x_ref, gate_ref, gpost_ref, wa_ref, wb_ref, wo_ref, o_ref):
    oa = ocmp_ref[...].astype(F32) + osel_ref[...].astype(F32) + owin_ref[...].astype(F32)
    a_in = (oa * _silu(za_ref[...].astype(F32))).astype(BF16)
    ya = jnp.dot(a_in, wa_ref[...], preferred_element_type=F32)
    b_in = (ob_ref[...].astype(F32) * _silu(zb_ref[...].astype(F32))).astype(BF16)
    yb = jnp.dot(b_in, wb_ref[...], preferred_element_type=F32)
    merged = (jax.nn.sigmoid(ma_ref[...].astype(F32)) * ya
              + jax.nn.sigmoid(mb_ref[...].astype(F32)) * yb)
    out = jnp.dot(merged.astype(BF16), wo_ref[...], preferred_element_type=F32)
    y = out * lax.rsqrt(jnp.mean(out * out, axis=-1, keepdims=True) + EPS) * gpost_ref[...]
    o_ref[...] = x_ref[...] + gate_ref[0] * y


def _out(ocmp, osel, owin, ob, slab, x2d, ada3, g_post, wa, wb, wo, seq):
    M, D = x2d.shape
    tm = 256
    per_b = seq // tm
    const = lambda a: pl.BlockSpec(a.shape, lambda i: (0,) * a.ndim, pipeline_mode=pl.Buffered(1))
    row = lambda w, cb: pl.BlockSpec((tm, w), lambda i: (i, cb))
    return pl.pallas_call(
        _out_kernel,
        grid=(M // tm,),
        in_specs=[row(A_WIDTH, 0), row(A_WIDTH, 0), row(A_WIDTH, 0),
                  row(A_WIDTH, COL_ZA // A_WIDTH), row(B_WIDTH, 0), row(B_WIDTH, COL_ZB // B_WIDTH),
                  row(D_MODEL, COL_MA // D_MODEL), row(D_MODEL, COL_MB // D_MODEL),
                  row(D, 0),
                  pl.BlockSpec((1, 1, D), lambda i: (i // per_b, 0, 2)),
                  pl.BlockSpec((1, D), lambda i: (0, 0)),
                  const(wa), const(wb), const(wo)],
        out_specs=pl.BlockSpec((tm, D), lambda i: (i, 0)),
        out_shape=jax.ShapeDtypeStruct((M, D), F32),
        compiler_params=_cparams(("arbitrary",)),
        name="out",
    )(ocmp, osel, owin, slab, ob, slab, slab, slab, x2d, ada3, g_post.reshape(1, D), wa, wb, wo)


def _layer(x, c, positions, w_ada, b_ada, g_pre, g_post, w_in, pe_ck, pe_cv, w_ck1, w_ck2,
           w_cv1, w_cv2, w_br_a, w_br_b, w_out):
    B, S, D = x.shape
    x2d = x.reshape(B * S, D)
    ada3 = _ada(c, w_ada, b_ada).reshape(B, 1, 3 * D)
    slab, kcv = _proj(x2d, ada3, g_pre, positions.reshape(B * S, 1), _build_w_slab_t(jnp.swapaxes(w_in, 0, 1)), S)

    nblk = S // CMP_STRIDE
    assert (S - CMP_BLOCK) // CMP_STRIDE + 1 == nblk - 1
    cmp_end = np.minimum(np.arange(nblk) * CMP_STRIDE + CMP_BLOCK - 1, S - 1)
    posc = positions[:, cmp_end].reshape(B, nblk, 1)
    dup = lambda w: jnp.concatenate([w, w], axis=1).astype(BF16)
    pe2 = jnp.concatenate([pe_ck, pe_cv], axis=1)
    kc2, vc2 = _cmp_kv(kcv, pe2, posc, _build_cmp_w1(w_ck1, w_cv1), dup(w_ck2), dup(w_cv2), B, S)

    ocmp, osel, owin = _nsa_attend(slab, kc2, vc2, B, S)
    ob = _dil(slab, B, S)
    out = _out(ocmp, osel, owin, ob, slab, x2d, ada3, g_post,
               w_br_a.astype(BF16), w_br_b.astype(BF16), w_out.astype(BF16), S)
    return out.reshape(B, S, D)


def kernel(x, c, positions, w_ada, b_ada, g_pre, g_post, w_in, pe_ck, pe_cv, w_ck1, w_ck2, w_cv1, w_cv2, w_br_a, w_br_b, w_out):
    h = x
    for layer in range(w_ada.shape[0]):
        h = _layer(h, c, positions, w_ada[layer], b_ada[layer], g_pre[layer], g_post[layer],
                   w_in[layer], pe_ck[layer], pe_cv[layer], w_ck1[layer], w_ck2[layer],
                   w_cv1[layer], w_cv2[layer], w_br_a[layer], w_br_b[layer], w_out[layer])
    return h
```

```python
import functools

import numpy as np
import jax
import jax.numpy as jnp
from jax import lax
from jax.experimental import pallas as pl
from jax.experimental.pallas import tpu as pltpu

F32 = jnp.float32
BF16 = jnp.bfloat16

D_MODEL = 2048
A_HEADS = 16
A_HEAD_DIM = 64
A_KV_GROUPS = 4
A_REP = A_HEADS // A_KV_GROUPS
A_WIDTH = A_HEADS * A_HEAD_DIM
A_KV_WIDTH = A_KV_GROUPS * A_HEAD_DIM
CMP_BLOCK = 32
CMP_STRIDE = 16
CMP_HIDDEN = 4 * A_HEAD_DIM
SEL_BLOCK = 64
SEL_TOPK = 16
WIN_SIZE = 512
FORCE_BONUS = 1.0e4
B_HEADS = 8
B_HEAD_DIM = 128
B_WIDTH = B_HEADS * B_HEAD_DIM
ROPE_THETA = 500000.0
EPS = 1e-6
NEG = -1e30
M_INIT = -1.0e38
LOG2E = 1.4426950408889634

LANES = 128
MXU_N = 256
VMEM_LIMIT = 56 * 1024 * 1024

TN = 1024
COL_QA = 0
COL_KS = 1024
COL_KW = 1536
COL_QB = 2048
COL_KB = 3072
COL_VB = 4096
COL_ZA = 5120
COL_ZB = 6144
COL_VS = 7168
COL_VW = 7680
COL_MA = 8192
COL_MB = 10240
N_SLAB = 12288
COL_KCV = N_SLAB
N_PROJ = N_SLAB + 2 * A_KV_WIDTH
GATE_LANE = 96
AUG_LANE = 64
ONES_LANE = 64

T_ATT = 256
NSA_STAGE_ORDER = (0, 0, 0, 0, 0)


def _cparams(sem):
    return pltpu.CompilerParams(dimension_semantics=sem, vmem_limit_bytes=VMEM_LIMIT)


def _ada_kernel(c_ref, w_ref, b_ref, o_ref):
    o_ref[...] = jnp.dot(c_ref[...], w_ref[...], preferred_element_type=F32,
                         precision=lax.Precision.HIGHEST) + b_ref[...]


def _ada(c, w_ada, b_ada):
    B, D = c.shape
    N = w_ada.shape[1]
    tn = 768
    return pl.pallas_call(
        _ada_kernel,
        grid=(N // tn,),
        in_specs=[pl.BlockSpec((B, D), lambda j: (0, 0)),
                  pl.BlockSpec((D, tn), lambda j: (0, j)),
                  pl.BlockSpec((1, tn), lambda j: (0, j))],
        out_specs=pl.BlockSpec((B, tn), lambda j: (0, j)),
        out_shape=jax.ShapeDtypeStruct((B, N), F32),
        compiler_params=_cparams(("arbitrary",)),
        name="ada",
    )(c, w_ada, b_ada.reshape(1, N))


def _rope_inv_tables():
    lane = np.arange(LANES)
    inv_a8 = (ROPE_THETA ** (-np.arange(0, 16, 2) / 16)).astype(np.float32)
    inv_b16 = (ROPE_THETA ** (-np.arange(0, 32, 2) / 32)).astype(np.float32)
    la = lane % A_HEAD_DIM
    inv_a = np.where(la < 16, inv_a8[la % 8], 0.0).astype(np.float32)
    inv_b = np.where(lane < 32, inv_b16[lane % 16], 0.0).astype(np.float32)
    return inv_a.reshape(1, LANES), inv_b.reshape(1, LANES)


def _rope_tables(pos_f, inv, period, half):
    ang = pos_f * inv
    c, s = jnp.cos(ang), jnp.sin(ang)
    lane = lax.broadcasted_iota(jnp.int32, ang.shape, 1) % period
    cos_t = jnp.where(lane < 2 * half, c, 1.0)
    sin_t = jnp.where(lane < half, -s, jnp.where(lane < 2 * half, s, 0.0))
    return cos_t, sin_t


def _rope_apply(x, cos_t, sin_t, half):
    lane = lax.broadcasted_iota(jnp.int32, x.shape, 1)
    partner = jnp.take_along_axis(x, lane ^ half, axis=1)
    return x * cos_t + partner * sin_t


def _proj_kernel(x_ref, shift_ref, scale_ref, gpre_ref, pos_ref, inva_ref, invb_ref, w_ref,
                 o_ref, kcv_ref, h_scr, ca, sa, cb, sb, *, tm, seq):
    i = pl.program_id(0)
    j = pl.program_id(1)

    @pl.when(j == 0)
    def _():
        x = x_ref[...]
        ms = jnp.mean(x * x, axis=-1, keepdims=True)
        y = x * lax.rsqrt(ms + EPS) * gpre_ref[...]
        h = y * (1.0 + scale_ref[0]) + shift_ref[0]
        h_scr[...] = h.astype(BF16)
        pos_f = pos_ref[...].astype(F32)
        ca[...], sa[...] = _rope_tables(pos_f, inva_ref[...], A_HEAD_DIM, 8)
        cb[...], sb[...] = _rope_tables(pos_f, invb_ref[...], LANES, 16)

    def tile(epilogues):
        for part, epilogue in enumerate(epilogues):
            c0 = part * MXU_N
            acc = lax.dot_general(h_scr[...], w_ref[c0:c0 + MXU_N, :], (((1,), (1,)), ((), ())),
                                  preferred_element_type=F32)
            epilogue(acc, c0)

    def rope_epilogue(tabs, half, mul, first_head_only, onehot):
        def epilogue(acc, c0):
            cos_t, sin_t = tabs[0][...], tabs[1][...]
            lane = lax.broadcasted_iota(jnp.int32, (tm, LANES), 1)
            if first_head_only:
                keep = lane < A_HEAD_DIM
                cos_t = jnp.where(keep, cos_t, 1.0)
                sin_t = jnp.where(keep, sin_t, 0.0)
            if onehot:
                t = (i * tm + lax.broadcasted_iota(jnp.int32, (tm, LANES), 0)) % seq
                hot = (lane >= AUG_LANE) & (lane < AUG_LANE + seq // SEL_BLOCK) & \
                      ((t // SEL_BLOCK) == (lane - AUG_LANE))
            for cidx in range(MXU_N // LANES):
                xc = acc[:, cidx * LANES:(cidx + 1) * LANES]
                r = _rope_apply(xc, cos_t, sin_t, half)
                if mul != 1.0:
                    r = r * mul
                if onehot:
                    r = jnp.where(hot, 1.0, r)
                o_ref[:, c0 + cidx * LANES:c0 + (cidx + 1) * LANES] = r.astype(o_ref.dtype)
        return epilogue

    def plain_epilogue(acc, c0):
        o_ref[:, c0:c0 + MXU_N] = acc.astype(o_ref.dtype)

    def ones_lane_epilogue(acc, c0):
        lane = lax.broadcasted_iota(jnp.int32, acc.shape, 1) % LANES
        o_ref[:, c0:c0 + MXU_N] = jnp.where(lane == ONES_LANE, 1.0, acc).astype(o_ref.dtype)

    def kcv_epilogue(acc, c0):
        kcv_ref[:, c0:c0 + MXU_N] = acc

    ta = (ca, sa)
    tb = (cb, sb)
    parts = lambda col0, col1, epi: [((col0 // MXU_N + n), epi) for n in range((col1 - col0) // MXU_N)]
    kinds = dict(
        parts(COL_QA, COL_KS, rope_epilogue(ta, 8, A_HEAD_DIM ** -0.5 * LOG2E, False, False))
        + parts(COL_KS, COL_KW, rope_epilogue(ta, 8, 1.0, True, True))
        + parts(COL_KW, COL_QB, rope_epilogue(ta, 8, 1.0, True, False))
        + parts(COL_QB, COL_KB, rope_epilogue(tb, 16, B_HEAD_DIM ** -0.5 * LOG2E, False, False))
        + parts(COL_KB, COL_VB, rope_epilogue(tb, 16, 1.0, False, False))
        + parts(COL_VB, COL_VS, plain_epilogue)
        + parts(COL_VS, COL_MA, ones_lane_epilogue)
        + parts(COL_MA, COL_KCV, plain_epilogue)
        + parts(COL_KCV, N_PROJ, None))
    per_tile = TN // MXU_N
    steps = {}
    for jt in range(pl.cdiv(N_PROJ, TN)):
        tile_kinds = tuple(kinds[p] for p in range(jt * per_tile, (jt + 1) * per_tile) if p in kinds)
        if tile_kinds[0] is None:
            assert all(k is None for k in tile_kinds) and len(tile_kinds) * MXU_N == kcv_ref.shape[1]
            tile_kinds = (kcv_epilogue,) * len(tile_kinds)
        steps.setdefault(tile_kinds, []).append(jt)
    for tile_kinds, jts in steps.items():
        pl.when(functools.reduce(jnp.logical_or, [j == jt for jt in jts]))(
            functools.partial(tile, tile_kinds))


def _proj(x2d, ada3, g_pre, pos2d, w_slab_t, seq):
    M, D = x2d.shape
    tm = 1024
    per_b = seq // tm
    inv_a, inv_b = _rope_inv_tables()
    tab = pltpu.VMEM((tm, LANES), F32)
    last_slab_tile = N_SLAB // TN - 1
    n_kcv = N_PROJ - N_SLAB
    return pl.pallas_call(
        functools.partial(_proj_kernel, tm=tm, seq=seq),
        grid=(M // tm, pl.cdiv(N_PROJ, TN)),
        in_specs=[pl.BlockSpec((tm, D), lambda i, j: (i, 0)),
                  pl.BlockSpec((1, 1, D), lambda i, j: (i // per_b, 0, 0)),
                  pl.BlockSpec((1, 1, D), lambda i, j: (i // per_b, 0, 1)),
                  pl.BlockSpec((1, D), lambda i, j: (0, 0)),
                  pl.BlockSpec((tm, 1), lambda i, j: (i, 0)),
                  pl.BlockSpec((1, LANES), lambda i, j: (0, 0)),
                  pl.BlockSpec((1, LANES), lambda i, j: (0, 0)),
                  pl.BlockSpec((TN, D), lambda i, j: (j, 0))],
        out_specs=[pl.BlockSpec((tm, TN), lambda i, j: (i, jnp.minimum(j, last_slab_tile))),
                   pl.BlockSpec((tm, n_kcv), lambda i, j: (i, 0))],
        out_shape=[jax.ShapeDtypeStruct((M, N_SLAB), BF16),
                   jax.ShapeDtypeStruct((M, n_kcv), F32)],
        scratch_shapes=[pltpu.VMEM((tm, D), BF16), tab, tab, tab, tab],
        compiler_params=_cparams(("arbitrary", "arbitrary")),
        name="proj",
    )(x2d, ada3, ada3, g_pre.reshape(1, D), pos2d, jnp.asarray(inv_a), jnp.asarray(inv_b), w_slab_t)


_SRC_SIZES = (A_WIDTH,) + (A_KV_WIDTH,) * 6 + (A_HEADS * 3, A_WIDTH) + (B_WIDTH,) * 4 + (D_MODEL, D_MODEL)
_SRC = [int(v) for v in np.concatenate([[0], np.cumsum(_SRC_SIZES)])]
(SRC_QA, SRC_KC, SRC_VC, SRC_KS, SRC_VS, SRC_KW, SRC_VW, SRC_GA, SRC_ZA, SRC_QB, SRC_KB, SRC_VB,
 SRC_ZB, SRC_MA, SRC_MB, SRC_END) = _SRC


def _w_slab_kernel(w_ref, o_ref):
    cols = w_ref.shape[1]

    def put(dst, src, width):
        o_ref[dst:dst + width, :] = w_ref[src:src + width, :].astype(o_ref.dtype)

    def clear(dst, width):
        o_ref[dst:dst + width, :] = jnp.zeros((width, cols), o_ref.dtype)

    put(COL_QA, SRC_QA, A_WIDTH)
    put(COL_QB, SRC_QB, B_WIDTH)
    put(COL_KB, SRC_KB, B_WIDTH)
    put(COL_VB, SRC_VB, B_WIDTH)
    put(COL_ZA, SRC_ZA, A_WIDTH)
    put(COL_ZB, SRC_ZB, B_WIDTH)
    put(COL_MA, SRC_MA, D_MODEL)
    put(COL_MB, SRC_MB, D_MODEL)
    ngate = A_REP * 3
    for g in range(A_KV_GROUPS):
        d = g * LANES
        s = g * A_HEAD_DIM
        put(COL_KS + d, SRC_KS + s, A_HEAD_DIM)
        clear(COL_KS + d + A_HEAD_DIM, GATE_LANE - A_HEAD_DIM)
        put(COL_KS + d + GATE_LANE, SRC_GA + g * ngate, ngate)
        clear(COL_KS + d + GATE_LANE + ngate, LANES - GATE_LANE - ngate)
        for col, src in ((COL_KW, SRC_KW), (COL_VS, SRC_VS), (COL_VW, SRC_VW)):
            put(col + d, src + s, A_HEAD_DIM)
            clear(col + d + A_HEAD_DIM, LANES - A_HEAD_DIM)
        put(COL_KCV + d, SRC_KC + s, A_HEAD_DIM)
        put(COL_KCV + d + A_HEAD_DIM, SRC_VC + s, A_HEAD_DIM)
    clear(N_PROJ, o_ref.shape[0] - N_PROJ)


def _build_w_slab_t(w_in_t):
    n_in, D = w_in_t.shape
    assert n_in == SRC_END
    cols = 256
    n_rows = pl.cdiv(N_PROJ, TN) * TN
    return pl.pallas_call(
        _w_slab_kernel,
        grid=(D // cols,),
        in_specs=[pl.BlockSpec((n_in, cols), lambda i: (0, i))],
        out_specs=pl.BlockSpec((n_rows, cols), lambda i: (0, i)),
        out_shape=jax.ShapeDtypeStruct((n_rows, D), BF16),
        compiler_params=_cparams(("arbitrary",)),
        name="w_slab",
    )(w_in_t)


def _cmp_kv_kernel(*refs):
    groups = A_KV_GROUPS
    t_refs = refs[:groups]
    pe_ref, posc_ref, inva_ref, w1_ref, wk2_ref, wv2_ref, kc2_ref, vc2_ref = refs[groups:]
    nblk = t_refs[0].shape[0] // CMP_STRIDE
    rows = groups * nblk
    hidden2 = w1_ref.shape[-1]
    p_lo = jnp.zeros((rows, hidden2), F32)
    p_hi = jnp.zeros((rows, hidden2), F32)
    def tokens(l):
        return jnp.concatenate([t_ref[pl.ds(l, nblk, stride=CMP_STRIDE), :] for t_ref in t_refs], axis=0)

    def pair(x0, x1, l):
        xs = jnp.concatenate([x0 + pe_ref[l:l + 1, :], x1 + pe_ref[l + 1:l + 2, :]], axis=1)
        return jnp.dot(xs.astype(BF16), w1_ref[l:l + 2].reshape(2 * LANES, hidden2),
                       preferred_element_type=F32)

    for l in range(0, CMP_STRIDE, 2):
        x0, x1 = tokens(l), tokens(l + 1)
        p_lo += pair(x0, x1, l)
        p_hi += pair(x0, x1, CMP_STRIDE + l)
    hid = p_lo + pltpu.roll(p_hi, rows - 1, 0)
    act = jax.nn.gelu(hid).astype(BF16)
    kk = jnp.dot(act[:, :CMP_HIDDEN], wk2_ref[...], preferred_element_type=F32)
    vv = jnp.dot(act[:, CMP_HIDDEN:], wv2_ref[...], preferred_element_type=F32)
    pos_f = jnp.concatenate([posc_ref[...]] * groups, axis=0).astype(F32)
    cos_t, sin_t = _rope_tables(pos_f, inva_ref[...], A_HEAD_DIM, 8)
    kk = _rope_apply(kk, cos_t, sin_t, 8)
    lane = lax.broadcasted_iota(jnp.int32, (nblk, LANES), 1)
    lo = lane < A_HEAD_DIM
    for g in range(groups):
        kg = kk[g * nblk:(g + 1) * nblk]
        vg = vv[g * nblk:(g + 1) * nblk]
        kc2_ref[g, 0:nblk, :] = jnp.where(lo, kg, 0.0).astype(kc2_ref.dtype)
        kc2_ref[g, nblk:2 * nblk, :] = jnp.where(lo, 0.0, kg).astype(kc2_ref.dtype)
        vc2_ref[g, 0:nblk, :] = jnp.where(lo, vg, 0.0).astype(vc2_ref.dtype)
        vc2_ref[g, nblk:2 * nblk, :] = jnp.where(lo, 0.0, vg).astype(vc2_ref.dtype)


def _cmp_kv(kcv, pe2, posc, w1, wk2d, wv2d, B, seq):
    G = A_KV_GROUPS
    nblk = seq // CMP_STRIDE
    inv_a, _ = _rope_inv_tables()
    full = lambda a: pl.BlockSpec(a.shape, lambda b: (0,) * a.ndim)
    out = jax.ShapeDtypeStruct((B, G, 2 * nblk, LANES), BF16)
    ospec = pl.BlockSpec((None, G, 2 * nblk, LANES), lambda b: (b, 0, 0, 0))
    return pl.pallas_call(
        _cmp_kv_kernel,
        grid=(B,),
        in_specs=[pl.BlockSpec((seq, LANES), lambda b, g=g: (b, g)) for g in range(G)] + [
                  full(pe2),
                  pl.BlockSpec((None, nblk, 1), lambda b: (b, 0, 0)),
                  pl.BlockSpec((1, LANES), lambda b: (0, 0)),
                  full(w1), full(wk2d), full(wv2d)],
        out_specs=[ospec, ospec],
        out_shape=[out, out],
        compiler_params=_cparams(("arbitrary",)),
        name="cmp_kv",
    )(*([kcv] * G), pe2, posc, jnp.asarray(inv_a), w1, wk2d, wv2d)


def _build_cmp_w1(w_ck1, w_cv1):
    wk = w_ck1.reshape(CMP_BLOCK, A_HEAD_DIM, CMP_HIDDEN)
    wv = w_cv1.reshape(CMP_BLOCK, A_HEAD_DIM, CMP_HIDDEN)
    z = jnp.zeros_like(wk)
    return jnp.concatenate([jnp.concatenate([wk, z], axis=2),
                            jnp.concatenate([z, wv], axis=2)], axis=1).astype(BF16)


def _gate_expand_table(branch):
    e = np.zeros((LANES, A_REP * LANES), np.float32)
    for r in range(A_REP):
        e[GATE_LANE + 3 * r + branch, r * LANES:(r + 1) * LANES] = 1.0
    return e


def _branch_gates(gate_blk, e_ref):
    sg = jax.nn.sigmoid(gate_blk.astype(F32)).astype(BF16)
    return jnp.dot(sg, e_ref[...], preferred_element_type=F32)


def _overlap_table_t(seq):
    n_c = (seq - CMP_BLOCK) // CMP_STRIDE + 1
    n_s = seq // SEL_BLOCK
    cs = np.arange(n_c) * CMP_STRIDE
    ss = np.arange(n_s) * SEL_BLOCK
    ov = np.clip(np.minimum(cs[:, None] + CMP_BLOCK, ss[None, :] + SEL_BLOCK)
                 - np.maximum(cs[:, None], ss[None, :]), 0, None).astype(np.float32) / CMP_BLOCK
    full = np.zeros((LANES, LANES), np.float32)
    full[AUG_LANE:AUG_LANE + n_s, :n_c] = ov.T
    return full


def _split3(x):
    hi = x.astype(BF16)
    r1 = x - hi.astype(F32)
    mid = r1.astype(BF16)
    lo = (r1 - mid.astype(F32)).astype(BF16)
    return hi, mid, lo


def _compressed_branch(q_ref, kc2_ref, vc2_ref, ovt_ref, gates, ocmp_ref, qi, tq, n_sel_blocks):
    t = qi * tq + lax.broadcasted_iota(jnp.int32, (tq, LANES), 0)
    lane = lax.broadcasted_iota(jnp.int32, (tq, LANES), 1)
    cmask = (CMP_STRIDE * lane + (CMP_BLOCK - 1)) <= t
    cmask_f = cmask.astype(F32)
    nt = (((1,), (1,)), ((), ()))
    psum = jnp.zeros((tq, LANES), F32)
    for pair in range(A_REP // 2):
        qp = q_ref[:, pair * LANES:(pair + 1) * LANES]
        s2 = lax.dot_general(qp, kc2_ref[...], nt, preferred_element_type=F32)
        probs = []
        for hh in range(2):
            s = jnp.where(cmask, s2[:, hh * LANES:(hh + 1) * LANES], NEG)
            mx = jnp.max(s, axis=-1, keepdims=True)
            e = jnp.exp2(s - mx) * cmask_f
            pr = e / jnp.maximum(jnp.sum(e, axis=-1, keepdims=True), 1.0)
            probs.append(pr)
            psum = psum + pr
        p2 = jnp.concatenate(probs, axis=1).astype(BF16)
        o_pair = jnp.dot(p2, vc2_ref[...], preferred_element_type=F32)
        gate = jnp.where(lane < A_HEAD_DIM, gates[:, (2 * pair) * LANES:(2 * pair + 1) * LANES],
                         gates[:, (2 * pair + 1) * LANES:(2 * pair + 2) * LANES])
        ocmp_ref[:, pair * LANES:(pair + 1) * LANES] = (gate * o_pair).astype(ocmp_ref.dtype)
        yield

    ovt = ovt_ref[...]
    imp_t = sum(lax.dot_general(ovt, part, nt, preferred_element_type=F32) for part in _split3(psum))
    imp_t = imp_t[AUG_LANE:AUG_LANE + n_sel_blocks, :]
    yield
    jb = lax.broadcasted_iota(jnp.int32, imp_t.shape, 0)
    tcol = qi * tq + lax.broadcasted_iota(jnp.int32, imp_t.shape, 1)
    cur = tcol // SEL_BLOCK
    valid = (SEL_BLOCK * jb) <= tcol
    forced = valid & ((jb == 0) | (jb == cur) | (jb == cur - 1))
    score = jnp.where(valid, imp_t + FORCE_BONUS * forced.astype(F32), NEG)
    rank = jnp.zeros(imp_t.shape, F32)
    for i in range(n_sel_blocks):
        other = score[i:i + 1, :]
        beats = (other > score) | ((other == score) & (jb > i))
        rank = rank + beats.astype(F32)
    selb = jnp.where(rank < float(min(SEL_TOPK, n_sel_blocks)), 0.0, NEG)
    yield
    aug_t = jnp.concatenate([jnp.zeros((AUG_LANE, tq), F32), selb,
                             jnp.zeros((LANES - AUG_LANE - n_sel_blocks, tq), F32)], axis=0)
    return aug_t.T.astype(BF16)


def _flash_init(m_scr, acc_scr, l_scr=None):
    m_scr[...] = jnp.full(m_scr.shape, M_INIT, F32)
    acc_scr[...] = jnp.zeros(acc_scr.shape, F32)
    if l_scr is not None:
        l_scr[...] = jnp.zeros(l_scr.shape, F32)


def _scores(q, k):
    return lax.dot_general(q, k, (((1,), (1,)), ((), ())), preferred_element_type=F32)


def _add_bias(s, bias, reps):
    tq, tk = bias.shape
    return (s.reshape(reps, tq, tk) + bias[None]).reshape(reps * tq, tk)


def _flash_update(s, v, m_scr, acc_scr, l_scr=None, first=False):
    chunks = [s[:, c * LANES:(c + 1) * LANES] for c in range(s.shape[1] // LANES)]
    mx = functools.reduce(jnp.maximum, chunks)
    if first:
        assert l_scr is None
        m_new = jnp.broadcast_to(jnp.max(mx, axis=-1, keepdims=True), mx.shape)
        p = jnp.concatenate([jnp.exp2((ch - m_new).astype(v.dtype)) for ch in chunks], axis=1)
        acc_scr[...] = jnp.dot(p, v, preferred_element_type=F32)
        m_scr[...] = m_new
        return
    m_prev = m_scr[...]
    m_new = jnp.maximum(m_prev, jnp.max(mx, axis=-1, keepdims=True))
    alpha = jnp.exp2(m_prev - m_new)
    if l_scr is None:
        p = jnp.concatenate([jnp.exp2((ch - m_new).astype(v.dtype)) for ch in chunks], axis=1)
    else:
        ps = [jnp.exp2(ch - m_new) for ch in chunks]
        p = jnp.concatenate(ps, axis=1).astype(v.dtype)
        l_scr[...] = alpha * l_scr[...] + functools.reduce(jnp.add, ps)
    acc_scr[...] = alpha * acc_scr[...] + jnp.dot(p, v, preferred_element_type=F32)
    m_scr[...] = m_new


def _gated_head(acc, gates, r):
    lane = lax.broadcasted_iota(jnp.int32, acc.shape, 1)
    scale = gates[:, r * LANES:(r + 1) * LANES] / acc[:, ONES_LANE:ONES_LANE + 1]
    return jnp.where(lane < A_HEAD_DIM, acc * scale, 0.0)


def _store_head_pair(o_ref, pair, even, odd):
    o_ref[:, pair * LANES:(pair + 1) * LANES] = (even + pltpu.roll(odd, A_HEAD_DIM, 1)).astype(o_ref.dtype)


def _win_bias_table(t, span, window):
    delta = np.arange(t)[:, None] - np.arange(span)[None, :]
    dist = np.stack([n * t + delta for n in range(span // t)])
    return np.where((dist >= 0) & (dist < window), 0.0, NEG).astype(np.float32)


def _interleave(order, *gens):
    results = [None] * len(gens)
    done = [False] * len(gens)

    def step(n):
        if not done[n]:
            try:
                next(gens[n])
            except StopIteration as stop:
                results[n], done[n] = stop.value, True
    for n in order:
        step(n)
    for n in range(len(gens)):
        while not done[n]:
            step(n)
    return results


def _window_branch(stacked_q, kw_ref, vw_ref, bias_ref, gates, owin_ref, m_scr, acc_scr, s0_scr, s1_scr,
                   qi, t, span):
    n_tiles = span // t
    start = pl.multiple_of(jnp.maximum((qi + 1) * t - span, 0), t)
    variant = jnp.minimum(qi, n_tiles - 1)
    rows_of = lambda ref, w: ref[pl.ds(pl.multiple_of(start + w * t, t), t), :]
    bufs = (s0_scr, s1_scr)

    def score_into(w):
        bufs[w % 2][...] = _scores(stacked_q(), rows_of(kw_ref, w))

    def fold(w):
        s = _add_bias(bufs[w % 2][...], bias_ref[variant, :, w * t:(w + 1) * t], A_REP)
        _flash_update(s, rows_of(vw_ref, w), m_scr, acc_scr, first=(w == 0))

    score_into(0)
    yield
    for w in range(n_tiles):
        if w + 1 < n_tiles:
            score_into(w + 1)
        fold(w)
        yield
    heads = [_gated_head(acc_scr[r * t:(r + 1) * t, :], gates, r) for r in range(A_REP)]
    for pair in range(A_REP // 2):
        _store_head_pair(owin_ref, pair, heads[2 * pair], heads[2 * pair + 1])


def _nsa_kernel(q_ref, kc2_ref, vc2_ref, ovt_ref, ks_ref, vs_ref, kw_ref, vw_ref, ecmp_ref, esel_ref,
                ewin_ref, bias_ref, ocmp_ref, osel_ref, owin_ref, qaug_scr, m_scr, acc_scr, s0_scr, s1_scr,
                mw_scr, accw_scr, sw0_scr, sw1_scr, *, t, span, n_sel_blocks):
    qi = pl.program_id(2)
    rows_of = lambda ref, kj: ref[pl.ds(pl.multiple_of(kj * t, t), t), :]
    gate_blk = rows_of(ks_ref, qi)
    lo = lax.broadcasted_iota(jnp.int32, (t, LANES), 1) < A_HEAD_DIM
    for r in range(A_REP):
        qp = q_ref[:, (r // 2) * LANES:(r // 2 + 1) * LANES]
        if r % 2:
            qp = jnp.concatenate([qp[:, A_HEAD_DIM:], qp[:, :A_HEAD_DIM]], axis=1)
        qaug_scr[r] = jnp.where(lo, qp, jnp.zeros_like(qp))
    stacked_q = lambda: qaug_scr[...].reshape(A_REP * t, LANES)

    cmp = _compressed_branch(q_ref, kc2_ref, vc2_ref, ovt_ref, _branch_gates(gate_blk, ecmp_ref),
                             ocmp_ref, qi, t, n_sel_blocks)
    win = _window_branch(stacked_q, kw_ref, vw_ref, bias_ref, _branch_gates(gate_blk, ewin_ref), owin_ref,
                         mw_scr, accw_scr, sw0_scr, sw1_scr, qi, t, span)
    aug, _ = _interleave(NSA_STAGE_ORDER, cmp, win)
    qaug_scr[...] = qaug_scr[...] + aug[None]
    _flash_init(m_scr, acc_scr)

    def score_into(buf, kj):
        buf[...] = _scores(stacked_q(), rows_of(ks_ref, kj))

    def fold(buf, kj, causal=False):
        s = buf[...]
        if causal:
            delta = (lax.broadcasted_iota(jnp.int32, (t, t), 0)
                     - lax.broadcasted_iota(jnp.int32, (t, t), 1))
            s = _add_bias(s, jnp.where(delta >= 0, 0.0, NEG), A_REP)
        _flash_update(s, rows_of(vs_ref, kj), m_scr, acc_scr)

    first = qi % 2
    score_into(s1_scr, 0)
    score_into(s0_scr, first)

    @pl.when(first == 1)
    def _():
        fold(s1_scr, 0)

    def body(i, carry):
        kj = first + 2 * i
        score_into(s1_scr, kj + 1)
        fold(s0_scr, kj)
        score_into(s0_scr, kj + 2)
        fold(s1_scr, kj + 1)
        return carry
    lax.fori_loop(0, (qi - first) // 2, body, 0)

    fold(s0_scr, qi, causal=True)
    gates = _branch_gates(gate_blk, esel_ref)
    sel_heads = [_gated_head(acc_scr[r * t:(r + 1) * t, :], gates, r) for r in range(A_REP)]
    for pair in range(A_REP // 2):
        _store_head_pair(osel_ref, pair, sel_heads[2 * pair], sel_heads[2 * pair + 1])


def _nsa_attend(slab, kc2, vc2, B, seq):
    t = T_ATT
    nq = seq // t
    G = A_KV_GROUPS
    rows = A_REP * t
    span = min(((WIN_SIZE - 1 + t - 1) // t + 1) * t, seq)
    ovt = jnp.asarray(_overlap_table_t(seq), BF16)
    ecmp, esel, ewin = (jnp.asarray(_gate_expand_table(br), BF16) for br in range(3))
    bias = jnp.asarray(_win_bias_table(t, span, WIN_SIZE))
    kv = lambda col: pl.BlockSpec((seq, LANES), lambda b, g, q: (b, col // LANES + g))
    const = lambda a: pl.BlockSpec(a.shape, lambda b, g, q: (0,) * a.ndim)
    ckv = pl.BlockSpec((None, None) + kc2.shape[2:], lambda b, g, q: (b, g, 0, 0))
    out = jax.ShapeDtypeStruct((B * seq, A_WIDTH), BF16)
    ospec = pl.BlockSpec((t, A_REP * A_HEAD_DIM), lambda b, g, q: (b * nq + q, g))
    return pl.pallas_call(
        functools.partial(_nsa_kernel, t=t, span=span, n_sel_blocks=seq // SEL_BLOCK),
        grid=(B, G, nq),
        in_specs=[pl.BlockSpec((t, A_REP * A_HEAD_DIM), lambda b, g, q: (b * nq + q, g)),
                  ckv, ckv, const(ovt), kv(COL_KS), kv(COL_VS), kv(COL_KW), kv(COL_VW),
                  const(ecmp), const(esel), const(ewin), const(bias)],
        out_specs=[ospec, ospec, ospec],
        out_shape=[out, out, out],
        scratch_shapes=[pltpu.VMEM((A_REP, t, LANES), BF16)]
        + [pltpu.VMEM((rows, LANES), F32), pltpu.VMEM((rows, LANES), F32),
           pltpu.VMEM((rows, t), F32), pltpu.VMEM((rows, t), F32)] * 2,
        compiler_params=_cparams(("arbitrary", "arbitrary", "arbitrary")),
        name="nsa",
    )(slab, kc2, vc2, ovt, slab, slab, slab, slab, ecmp, esel, ewin, bias)


DILATIONS = ((128, 1), (512, 4), (2048, 16))
TQ_DIL = 512
TK_DIL = 256


def _dil_bias_tables(tq, tk, seq):
    w_near = max(w for w, r in DILATIONS if w < seq)
    near_keys = tq + -(-w_near // tk) * tk
    w_far, r_far = DILATIONS[-1]
    assert w_far >= seq and tq % r_far == 0 and tk % r_far == 0 and near_keys == 2 * tq
    assert tq % (2 * tk) == 0
    delta = np.arange(tq)[:, None] - np.arange(near_keys)[None, :]
    near = []
    for base in (near_keys - tq, 0):
        dist = base + delta
        mult = sum(((dist >= 0) & (dist <= w) & (dist % r == 0)).astype(np.float64) for w, r in DILATIONS)
        near.append(np.where(mult > 0, np.log2(np.maximum(mult, 1.0)), NEG))
    far = np.where(delta[:, :tk] % r_far == 0, 0.0, NEG)
    return np.stack(near).astype(np.float32), far.astype(np.float32)


def _dil_kernel(q_ref, k_ref, v_ref, near_ref, far_ref, o_ref, m_scr, acc_scr, l_scr, s0_scr, s1_scr,
                *, tq, tk):
    qi = pl.program_id(2)
    q = q_ref[...]
    near_keys = near_ref.shape[-1]
    _flash_init(m_scr, acc_scr, l_scr)
    update = functools.partial(_flash_update, m_scr=m_scr, acc_scr=acc_scr, l_scr=l_scr)
    rows_of = lambda ref, kj: ref[pl.ds(pl.multiple_of(kj * tk, tk), tk), :]
    variant = ((qi + 1) * tq < near_keys).astype(jnp.int32)
    n_far = jnp.maximum((qi + 1) * tq - near_keys, 0) // tk
    bufs = (s0_scr, s1_scr)

    def score_into(buf, kj):
        buf[...] = _scores(q, rows_of(k_ref, kj))

    def fold(buf, kj):
        update(buf[...] + far_ref[...], rows_of(v_ref, kj))

    score_into(s0_scr, 0)

    def body(i, carry):
        kj = 2 * i
        score_into(s1_scr, kj + 1)
        fold(s0_scr, kj)
        score_into(s0_scr, kj + 2)
        fold(s1_scr, kj + 1)
        return carry
    lax.fori_loop(0, n_far // 2, body, 0)

    n_near = near_keys // tk
    for w in range(n_near):
        if w + 1 < n_near:
            score_into(bufs[(w + 1) % 2], n_far + w + 1)
        update(bufs[w % 2][...] + near_ref[variant, :, w * tk:(w + 1) * tk], rows_of(v_ref, n_far + w))
    l = jnp.sum(l_scr[...], axis=-1, keepdims=True)
    o_ref[...] = (acc_scr[...] / l).astype(o_ref.dtype)


def _dil(slab, B, seq):
    tq, tk = TQ_DIL, TK_DIL
    nq = seq // tq
    H = B_HEADS
    near, far = (jnp.asarray(a) for a in _dil_bias_tables(tq, tk, seq))
    return pl.pallas_call(
        functools.partial(_dil_kernel, tq=tq, tk=tk),
        grid=(B, H, nq),
        in_specs=[pl.BlockSpec((tq, LANES), lambda b, h, q: (b * nq + q, COL_QB // LANES + h)),
                  pl.BlockSpec((seq, LANES), lambda b, h, q: (b, COL_KB // LANES + h)),
                  pl.BlockSpec((seq, LANES), lambda b, h, q: (b, COL_VB // LANES + h)),
                  pl.BlockSpec(near.shape, lambda b, h, q: (0, 0, 0)),
                  pl.BlockSpec(far.shape, lambda b, h, q: (0, 0))],
        out_specs=pl.BlockSpec((tq, LANES), lambda b, h, q: (b * nq + q, h)),
        out_shape=jax.ShapeDtypeStruct((B * seq, B_WIDTH), BF16),
        scratch_shapes=[pltpu.VMEM((tq, LANES), F32)] * 3 + [pltpu.VMEM((tq, tk), F32)] * 2,
        compiler_params=_cparams(("arbitrary", "arbitrary", "arbitrary")),
        name="dilated",
    )(slab, slab, slab, near, far)


def _silu(z):
    return z * jax.nn.sigmoid(z)


def _out_kernel(ocmp_ref, osel_ref, owin_ref, za_ref, ob_ref, zb_ref, ma_ref, mb_ref,
                x_ref, gate_ref, gpost_ref, wa_ref, wb_ref, wo_ref, o_ref):
    oa = ocmp_ref[...].astype(F32) + osel_ref[...].astype(F32) + owin_ref[...].astype(F32)
    a_in = (oa * _silu(za_ref[...].astype(F32))).astype(BF16)
    ya = jnp.dot(a_in, wa_ref[...], preferred_element_type=F32)
    b_in = (ob_ref[...].astype(F32) * _silu(zb_ref[...].astype(F32))).astype(BF16)
    yb = jnp.dot(b_in, wb_ref[...], preferred_element_type=F32)
    merged = (jax.nn.sigmoid(ma_ref[...].astype(F32)) * ya
              + jax.nn.sigmoid(mb_ref[...].astype(F32)) * yb)
    out = jnp.dot(merged.astype(BF16), wo_ref[...], preferred_element_type=F32)
    y = out * lax.rsqrt(jnp.mean(out * out, axis=-1, keepdims=True) + EPS) * gpost_ref[...]
    o_ref[...] = x_ref[...] + gate_ref[0] * y


def _out(ocmp, osel, owin, ob, slab, x2d, ada3, g_post, wa, wb, wo, seq):
    M, D = x2d.shape
    tm = 256
    per_b = seq // tm
    const = lambda a: pl.BlockSpec(a.shape, lambda i: (0,) * a.ndim, pipeline_mode=pl.Buffered(1))
    row = lambda w, cb: pl.BlockSpec((tm, w), lambda i: (i, cb))
    return pl.pallas_call(
        _out_kernel,
        grid=(M // tm,),
        in_specs=[row(A_WIDTH, 0), row(A_WIDTH, 0), row(A_WIDTH, 0),
                  row(A_WIDTH, COL_ZA // A_WIDTH), row(B_WIDTH, 0), row(B_WIDTH, COL_ZB // B_WIDTH),
                  row(D_MODEL, COL_MA // D_MODEL), row(D_MODEL, COL_MB // D_MODEL),
                  row(D, 0),
                  pl.BlockSpec((1, 1, D), lambda i: (i // per_b, 0, 2)),
                  pl.BlockSpec((1, D), lambda i: (0, 0)),
                  const(wa), const(wb), const(wo)],
        out_specs=pl.BlockSpec((tm, D), lambda i: (i, 0)),
        out_shape=jax.ShapeDtypeStruct((M, D), F32),
        compiler_params=_cparams(("arbitrary",)),
        name="out",
    )(ocmp, osel, owin, slab, ob, slab, slab, slab, x2d, ada3, g_post.reshape(1, D), wa, wb, wo)


def _layer(x, c, positions, w_ada, b_ada, g_pre, g_post, w_in, pe_ck, pe_cv, w_ck1, w_ck2,
           w_cv1, w_cv2, w_br_a, w_br_b, w_out):
    B, S, D = x.shape
    x2d = x.reshape(B * S, D)
    ada3 = _ada(c, w_ada, b_ada).reshape(B, 1, 3 * D)
    slab, kcv = _proj(x2d, ada3, g_pre, positions.reshape(B * S, 1), _build_w_slab_t(jnp.swapaxes(w_in, 0, 1)), S)

    nblk = S // CMP_STRIDE
    assert (S - CMP_BLOCK) // CMP_STRIDE + 1 == nblk - 1
    cmp_end = np.minimum(np.arange(nblk) * CMP_STRIDE + CMP_BLOCK - 1, S - 1)
    posc = positions[:, cmp_end].reshape(B, nblk, 1)
    dup = lambda w: jnp.concatenate([w, w], axis=1).astype(BF16)
    pe2 = jnp.concatenate([pe_ck, pe_cv], axis=1)
    kc2, vc2 = _cmp_kv(kcv, pe2, posc, _build_cmp_w1(w_ck1, w_cv1), dup(w_ck2), dup(w_cv2), B, S)

    ocmp, osel, owin = _nsa_attend(slab, kc2, vc2, B, S)
    ob = _dil(slab, B, S)
    out = _out(ocmp, osel, owin, ob, slab, x2d, ada3, g_post,
               w_br_a.astype(BF16), w_br_b.astype(BF16), w_out.astype(BF16), S)
    return out.reshape(B, S, D)


def kernel(x, c, positions, w_ada, b_ada, g_pre, g_post, w_in, pe_ck, pe_cv, w_ck1, w_ck2, w_cv1, w_cv2, w_br_a, w_br_b, w_out):
    h = x
    for layer in range(w_ada.shape[0]):
        h = _layer(h, c, positions, w_ada[layer], b_ada[layer], g_pre[layer], g_post[layer],
                   w_in[layer], pe_ck[layer], pe_cv[layer], w_ck1[layer], w_ck2[layer],
                   w_cv1[layer], w_cv2[layer], w_br_a[layer], w_br_b[layer], w_out[layer])
    return h
```

```python
import functools

import numpy as np
import jax
import jax.numpy as jnp
from jax import lax
from jax.experimental import pallas as pl
from jax.experimental.pallas import tpu as pltpu

F32 = jnp.float32
BF16 = jnp.bfloat16

D_MODEL = 2048
A_HEADS = 16
A_HEAD_DIM = 64
A_KV_GROUPS = 4
A_REP = A_HEADS // A_KV_GROUPS
A_WIDTH = A_HEADS * A_HEAD_DIM
A_KV_WIDTH = A_KV_GROUPS * A_HEAD_DIM
CMP_BLOCK = 32
CMP_STRIDE = 16
CMP_HIDDEN = 4 * A_HEAD_DIM
SEL_BLOCK = 64
SEL_TOPK = 16
WIN_SIZE = 512
FORCE_BONUS = 1.0e4
B_HEADS = 8
B_HEAD_DIM = 128
B_WIDTH = B_HEADS * B_HEAD_DIM
ROPE_THETA = 500000.0
EPS = 1e-6
NEG = -1e30
M_INIT = -1.0e38
LOG2E = 1.4426950408889634

LANES = 128
MXU_N = 256
VMEM_LIMIT = 56 * 1024 * 1024

TN = 1024
COL_QA = 0
COL_KS = 1024
COL_KW = 1536
COL_QB = 2048
COL_KB = 3072
COL_VB = 4096
COL_ZA = 5120
COL_ZB = 6144
COL_VS = 7168
COL_VW = 7680
COL_MA = 8192
COL_MB = 10240
N_SLAB = 12288
COL_KCV = N_SLAB
N_PROJ = N_SLAB + 2 * A_KV_WIDTH
GATE_LANE = 96
AUG_LANE = 64
ONES_LANE = 64

T_ATT = 256
NSA_STAGE_ORDER = (0, 0, 0, 0, 0)


def _cparams(sem):
    return pltpu.CompilerParams(dimension_semantics=sem, vmem_limit_bytes=VMEM_LIMIT)


def _ada_kernel(c_ref, w_ref, b_ref, o_ref):
    o_ref[...] = jnp.dot(c_ref[...], w_ref[...], preferred_element_type=F32,
                         precision=lax.Precision.HIGHEST) + b_ref[...]


def _ada(c, w_ada, b_ada):
    B, D = c.shape
    N = w_ada.shape[1]
    tn = 768
    return pl.pallas_call(
        _ada_kernel,
        grid=(N // tn,),
        in_specs=[pl.BlockSpec((B, D), lambda j: (0, 0)),
                  pl.BlockSpec((D, tn), lambda j: (0, j)),
                  pl.BlockSpec((1, tn), lambda j: (0, j))],
        out_specs=pl.BlockSpec((B, tn), lambda j: (0, j)),
        out_shape=jax.ShapeDtypeStruct((B, N), F32),
        compiler_params=_cparams(("arbitrary",)),
        name="ada",
    )(c, w_ada, b_ada.reshape(1, N))


def _rope_inv_tables():
    lane = np.arange(LANES)
    inv_a8 = (ROPE_THETA ** (-np.arange(0, 16, 2) / 16)).astype(np.float32)
    inv_b16 = (ROPE_THETA ** (-np.arange(0, 32, 2) / 32)).astype(np.float32)
    la = lane % A_HEAD_DIM
    inv_a = np.where(la < 16, inv_a8[la % 8], 0.0).astype(np.float32)
    inv_b = np.where(lane < 32, inv_b16[lane % 16], 0.0).astype(np.float32)
    return inv_a.reshape(1, LANES), inv_b.reshape(1, LANES)


def _rope_tables(pos_f, inv, period, half):
    ang = pos_f * inv
    c, s = jnp.cos(ang), jnp.sin(ang)
    lane = lax.broadcasted_iota(jnp.int32, ang.shape, 1) % period
    cos_t = jnp.where(lane < 2 * half, c, 1.0)
    sin_t = jnp.where(lane < half, -s, jnp.where(lane < 2 * half, s, 0.0))
    return cos_t, sin_t


A_TABLE_LANE0 = 32


def _rope_inv_combined():
    inv_a, inv_b = _rope_inv_tables()
    comb = inv_b.copy()
    comb[0, A_TABLE_LANE0:A_TABLE_LANE0 + 16] = inv_a[0, :16]
    return comb


def _rope_tables_ab(pos_f, inv_comb):
    ang = pos_f * inv_comb
    c, s = jnp.cos(ang), jnp.sin(ang)
    lane = lax.broadcasted_iota(jnp.int32, ang.shape, 1)
    cos_b = jnp.where(lane < 32, c, 1.0)
    sin_b = jnp.where(lane < 16, -s, jnp.where(lane < 32, s, 0.0))
    la = lane % A_HEAD_DIM
    src = jnp.where(la < 16, A_TABLE_LANE0 + la, lane)
    ca, sa = jnp.take_along_axis(c, src, axis=1), jnp.take_along_axis(s, src, axis=1)
    cos_a = jnp.where(la < 16, ca, 1.0)
    sin_a = jnp.where(la < 8, -sa, jnp.where(la < 16, sa, 0.0))
    return cos_a, sin_a, cos_b, sin_b


def _rope_apply(x, cos_t, sin_t, half):
    lane = lax.broadcasted_iota(jnp.int32, x.shape, 1)
    partner = jnp.take_along_axis(x, lane ^ half, axis=1)
    return x * cos_t + partner * sin_t


def _proj_kernel(x_ref, shift_ref, scale_ref, gpre_ref, pos_ref, inv_ref, w_ref,
                 o_ref, kcv_ref, h_scr, ca, sa, cb, sb, *, tm, seq):
    i = pl.program_id(0)
    j = pl.program_id(1)

    @pl.when(j == 0)
    def _():
        x = x_ref[...]
        ms = jnp.mean(x * x, axis=-1, keepdims=True)
        y = x * lax.rsqrt(ms + EPS) * gpre_ref[...]
        h = y * (1.0 + scale_ref[0]) + shift_ref[0]
        h_scr[...] = h.astype(BF16)
        pos_f = pos_ref[...].astype(F32)
        ca[...], sa[...], cb[...], sb[...] = _rope_tables_ab(pos_f, inv_ref[...])

    def tile(epilogues):
        for part, epilogue in enumerate(epilogues):
            c0 = part * MXU_N
            acc = lax.dot_general(h_scr[...], w_ref[c0:c0 + MXU_N, :], (((1,), (1,)), ((), ())),
                                  preferred_element_type=F32)
            epilogue(acc, c0)

    def rope_epilogue(tabs, half, mul, first_head_only, onehot):
        def epilogue(acc, c0):
            cos_t, sin_t = tabs[0][...], tabs[1][...]
            lane = lax.broadcasted_iota(jnp.int32, (tm, LANES), 1)
            if first_head_only:
                keep = lane < A_HEAD_DIM
                cos_t = jnp.where(keep, cos_t, 1.0)
                sin_t = jnp.where(keep, sin_t, 0.0)
            if onehot:
                t = (i * tm + lax.broadcasted_iota(jnp.int32, (tm, LANES), 0)) % seq
                hot = (lane >= AUG_LANE) & (lane < AUG_LANE + seq // SEL_BLOCK) & \
                      ((t // SEL_BLOCK) == (lane - AUG_LANE))
            for cidx in range(MXU_N // LANES):
                xc = acc[:, cidx * LANES:(cidx + 1) * LANES]
                r = _rope_apply(xc, cos_t, sin_t, half)
                if mul != 1.0:
                    r = r * mul
                if onehot:
                    r = jnp.where(hot, 1.0, r)
                o_ref[:, c0 + cidx * LANES:c0 + (cidx + 1) * LANES] = r.astype(o_ref.dtype)
        return epilogue

    def plain_epilogue(acc, c0):
        o_ref[:, c0:c0 + MXU_N] = acc.astype(o_ref.dtype)

    def ones_lane_epilogue(acc, c0):
        lane = lax.broadcasted_iota(jnp.int32, acc.shape, 1) % LANES
        o_ref[:, c0:c0 + MXU_N] = jnp.where(lane == ONES_LANE, 1.0, acc).astype(o_ref.dtype)

    def kcv_epilogue(acc, c0):
        kcv_ref[:, c0:c0 + MXU_N] = acc

    ta = (ca, sa)
    tb = (cb, sb)
    parts = lambda col0, col1, epi: [((col0 // MXU_N + n), epi) for n in range((col1 - col0) // MXU_N)]
    kinds = dict(
        parts(COL_QA, COL_KS, rope_epilogue(ta, 8, A_HEAD_DIM ** -0.5 * LOG2E, False, False))
        + parts(COL_KS, COL_KW, rope_epilogue(ta, 8, 1.0, True, True))
        + parts(COL_KW, COL_QB, rope_epilogue(ta, 8, 1.0, True, False))
        + parts(COL_QB, COL_KB, rope_epilogue(tb, 16, B_HEAD_DIM ** -0.5 * LOG2E, False, False))
        + parts(COL_KB, COL_VB, rope_epilogue(tb, 16, 1.0, False, False))
        + parts(COL_VB, COL_VS, plain_epilogue)
        + parts(COL_VS, COL_MA, ones_lane_epilogue)
        + parts(COL_MA, COL_KCV, plain_epilogue)
        + parts(COL_KCV, N_PROJ, None))
    per_tile = TN // MXU_N
    steps = {}
    for jt in range(pl.cdiv(N_PROJ, TN)):
        tile_kinds = tuple(kinds[p] for p in range(jt * per_tile, (jt + 1) * per_tile) if p in kinds)
        if tile_kinds[0] is None:
            assert all(k is None for k in tile_kinds) and len(tile_kinds) * MXU_N == kcv_ref.shape[1]
            tile_kinds = (kcv_epilogue,) * len(tile_kinds)
        steps.setdefault(tile_kinds, []).append(jt)
    for tile_kinds, jts in steps.items():
        pl.when(functools.reduce(jnp.logical_or, [j == jt for jt in jts]))(
            functools.partial(tile, tile_kinds))


def _proj(x2d, ada3, g_pre, pos2d, w_slab_t, seq):
    M, D = x2d.shape
    tm = 1024
    per_b = seq // tm
    tab = pltpu.VMEM((tm, LANES), F32)
    last_slab_tile = N_SLAB // TN - 1
    n_kcv = N_PROJ - N_SLAB
    return pl.pallas_call(
        functools.partial(_proj_kernel, tm=tm, seq=seq),
        grid=(M // tm, pl.cdiv(N_PROJ, TN)),
        in_specs=[pl.BlockSpec((tm, D), lambda i, j: (i, 0)),
                  pl.BlockSpec((1, 1, D), lambda i, j: (i // per_b, 0, 0)),
                  pl.BlockSpec((1, 1, D), lambda i, j: (i // per_b, 0, 1)),
                  pl.BlockSpec((1, D), lambda i, j: (0, 0)),
                  pl.BlockSpec((tm, 1), lambda i, j: (i, 0)),
                  pl.BlockSpec((1, LANES), lambda i, j: (0, 0)),
                  pl.BlockSpec((TN, D), lambda i, j: (j, 0))],
        out_specs=[pl.BlockSpec((tm, TN), lambda i, j: (i, jnp.minimum(j, last_slab_tile))),
                   pl.BlockSpec((tm, n_kcv), lambda i, j: (i, 0))],
        out_shape=[jax.ShapeDtypeStruct((M, N_SLAB), BF16),
                   jax.ShapeDtypeStruct((M, n_kcv), F32)],
        scratch_shapes=[pltpu.VMEM((tm, D), BF16), tab, tab, tab, tab],
        compiler_params=_cparams(("arbitrary", "arbitrary")),
        name="proj",
    )(x2d, ada3, ada3, g_pre.reshape(1, D), pos2d, jnp.asarray(_rope_inv_combined()), w_slab_t)


_SRC_SIZES = (A_WIDTH,) + (A_KV_WIDTH,) * 6 + (A_HEADS * 3, A_WIDTH) + (B_WIDTH,) * 4 + (D_MODEL, D_MODEL)
_SRC = [int(v) for v in np.concatenate([[0], np.cumsum(_SRC_SIZES)])]
(SRC_QA, SRC_KC, SRC_VC, SRC_KS, SRC_VS, SRC_KW, SRC_VW, SRC_GA, SRC_ZA, SRC_QB, SRC_KB, SRC_VB,
 SRC_ZB, SRC_MA, SRC_MB, SRC_END) = _SRC


def _w_slab_kernel(w_ref, o_ref):
    cols = w_ref.shape[1]

    def put(dst, src, width):
        o_ref[dst:dst + width, :] = w_ref[src:src + width, :].astype(o_ref.dtype)

    def clear(dst, width):
        o_ref[dst:dst + width, :] = jnp.zeros((width, cols), o_ref.dtype)

    put(COL_QA, SRC_QA, A_WIDTH)
    put(COL_QB, SRC_QB, B_WIDTH)
    put(COL_KB, SRC_KB, B_WIDTH)
    put(COL_VB, SRC_VB, B_WIDTH)
    put(COL_ZA, SRC_ZA, A_WIDTH)
    put(COL_ZB, SRC_ZB, B_WIDTH)
    put(COL_MA, SRC_MA, D_MODEL)
    put(COL_MB, SRC_MB, D_MODEL)
    ngate = A_REP * 3
    for g in range(A_KV_GROUPS):
        d = g * LANES
        s = g * A_HEAD_DIM
        put(COL_KS + d, SRC_KS + s, A_HEAD_DIM)
        clear(COL_KS + d + A_HEAD_DIM, GATE_LANE - A_HEAD_DIM)
        put(COL_KS + d + GATE_LANE, SRC_GA + g * ngate, ngate)
        clear(COL_KS + d + GATE_LANE + ngate, LANES - GATE_LANE - ngate)
        for col, src in ((COL_KW, SRC_KW), (COL_VS, SRC_VS), (COL_VW, SRC_VW)):
            put(col + d, src + s, A_HEAD_DIM)
            clear(col + d + A_HEAD_DIM, LANES - A_HEAD_DIM)
        put(COL_KCV + d, SRC_KC + s, A_HEAD_DIM)
        put(COL_KCV + d + A_HEAD_DIM, SRC_VC + s, A_HEAD_DIM)
    clear(N_PROJ, o_ref.shape[0] - N_PROJ)


def _build_w_slab_t(w_in_t):
    n_in, D = w_in_t.shape
    assert n_in == SRC_END
    cols = 256
    n_rows = pl.cdiv(N_PROJ, TN) * TN
    return pl.pallas_call(
        _w_slab_kernel,
        grid=(D // cols,),
        in_specs=[pl.BlockSpec((n_in, cols), lambda i: (0, i))],
        out_specs=pl.BlockSpec((n_rows, cols), lambda i: (0, i)),
        out_shape=jax.ShapeDtypeStruct((n_rows, D), BF16),
        compiler_params=_cparams(("arbitrary",)),
        name="w_slab",
    )(w_in_t)


def _cmp_kv_kernel(*refs):
    groups = A_KV_GROUPS
    t_refs = refs[:groups]
    pe_ref, posc_ref, inva_ref, w1_ref, wk2_ref, wv2_ref, kc2_ref, vc2_ref = refs[groups:]
    nblk = t_refs[0].shape[0] // CMP_STRIDE
    rows = groups * nblk
    hidden2 = w1_ref.shape[-1]
    p_lo = jnp.zeros((rows, hidden2), F32)
    p_hi = jnp.zeros((rows, hidden2), F32)
    def tokens(l):
        return jnp.concatenate([t_ref[pl.ds(l, nblk, stride=CMP_STRIDE), :] for t_ref in t_refs], axis=0)

    def pair(x0, x1, l):
        xs = jnp.concatenate([x0 + pe_ref[l:l + 1, :], x1 + pe_ref[l + 1:l + 2, :]], axis=1)
        return jnp.dot(xs.astype(BF16), w1_ref[l:l + 2].reshape(2 * LANES, hidden2),
                       preferred_element_type=F32)

    for l in range(0, CMP_STRIDE, 2):
        x0, x1 = tokens(l), tokens(l + 1)
        p_lo += pair(x0, x1, l)
        p_hi += pair(x0, x1, CMP_STRIDE + l)
    hid = p_lo + pltpu.roll(p_hi, rows - 1, 0)
    act = jax.nn.gelu(hid).astype(BF16)
    kk = jnp.dot(act[:, :CMP_HIDDEN], wk2_ref[...], preferred_element_type=F32)
    vv = jnp.dot(act[:, CMP_HIDDEN:], wv2_ref[...], preferred_element_type=F32)
    pos_f = jnp.concatenate([posc_ref[...]] * groups, axis=0).astype(F32)
    cos_t, sin_t = _rope_tables(pos_f, inva_ref[...], A_HEAD_DIM, 8)
    kk = _rope_apply(kk, cos_t, sin_t, 8)
    lane = lax.broadcasted_iota(jnp.int32, (nblk, LANES), 1)
    lo = lane < A_HEAD_DIM
    for g in range(groups):
        kg = kk[g * nblk:(g + 1) * nblk]
        vg = vv[g * nblk:(g + 1) * nblk]
        kc2_ref[g, 0:nblk, :] = jnp.where(lo, kg, 0.0).astype(kc2_ref.dtype)
        kc2_ref[g, nblk:2 * nblk, :] = jnp.where(lo, 0.0, kg).astype(kc2_ref.dtype)
        vc2_ref[g, 0:nblk, :] = jnp.where(lo, vg, 0.0).astype(vc2_ref.dtype)
        vc2_ref[g, nblk:2 * nblk, :] = jnp.where(lo, 0.0, vg).astype(vc2_ref.dtype)


def _cmp_kv(kcv, pe2, posc, w1, wk2d, wv2d, B, seq):
    G = A_KV_GROUPS
    nblk = seq // CMP_STRIDE
    inv_a, _ = _rope_inv_tables()
    full = lambda a: pl.BlockSpec(a.shape, lambda b: (0,) * a.ndim)
    out = jax.ShapeDtypeStruct((B, G, 2 * nblk, LANES), BF16)
    ospec = pl.BlockSpec((None, G, 2 * nblk, LANES), lambda b: (b, 0, 0, 0))
    return pl.pallas_call(
        _cmp_kv_kernel,
        grid=(B,),
        in_specs=[pl.BlockSpec((seq, LANES), lambda b, g=g: (b, g)) for g in range(G)] + [
                  full(pe2),
                  pl.BlockSpec((None, nblk, 1), lambda b: (b, 0, 0)),
                  pl.BlockSpec((1, LANES), lambda b: (0, 0)),
                  full(w1), full(wk2d), full(wv2d)],
        out_specs=[ospec, ospec],
        out_shape=[out, out],
        compiler_params=_cparams(("arbitrary",)),
        name="cmp_kv",
    )(*([kcv] * G), pe2, posc, jnp.asarray(inv_a), w1, wk2d, wv2d)


def _build_cmp_w1(w_ck1, w_cv1):
    wk = w_ck1.reshape(CMP_BLOCK, A_HEAD_DIM, CMP_HIDDEN)
    wv = w_cv1.reshape(CMP_BLOCK, A_HEAD_DIM, CMP_HIDDEN)
    z = jnp.zeros_like(wk)
    return jnp.concatenate([jnp.concatenate([wk, z], axis=2),
                            jnp.concatenate([z, wv], axis=2)], axis=1).astype(BF16)


def _gate_expand_table(branch):
    e = np.zeros((LANES, A_REP * LANES), np.float32)
    for r in range(A_REP):
        e[GATE_LANE + 3 * r + branch, r * LANES:(r + 1) * LANES] = 1.0
    return e


def _branch_gates(gate_blk, e_ref):
    sg = jax.nn.sigmoid(gate_blk.astype(F32)).astype(BF16)
    return jnp.dot(sg, e_ref[...], preferred_element_type=F32)


def _overlap_table_t(seq):
    n_c = (seq - CMP_BLOCK) // CMP_STRIDE + 1
    n_s = seq // SEL_BLOCK
    cs = np.arange(n_c) * CMP_STRIDE
    ss = np.arange(n_s) * SEL_BLOCK
    ov = np.clip(np.minimum(cs[:, None] + CMP_BLOCK, ss[None, :] + SEL_BLOCK)
                 - np.maximum(cs[:, None], ss[None, :]), 0, None).astype(np.float32) / CMP_BLOCK
    full = np.zeros((LANES, LANES), np.float32)
    full[AUG_LANE:AUG_LANE + n_s, :n_c] = ov.T
    return full


def _split3(x):
    hi = x.astype(BF16)
    r1 = x - hi.astype(F32)
    mid = r1.astype(BF16)
    lo = (r1 - mid.astype(F32)).astype(BF16)
    return hi, mid, lo


def _compressed_branch(q_ref, kc2_ref, vc2_ref, ovt_ref, gates, ocmp_ref, qi, tq, n_sel_blocks):
    t = qi * tq + lax.broadcasted_iota(jnp.int32, (tq, LANES), 0)
    lane = lax.broadcasted_iota(jnp.int32, (tq, LANES), 1)
    cmask = (CMP_STRIDE * lane + (CMP_BLOCK - 1)) <= t
    cmask_f = cmask.astype(F32)
    nt = (((1,), (1,)), ((), ()))
    psum = jnp.zeros((tq, LANES), F32)
    for pair in range(A_REP // 2):
        qp = q_ref[:, pair * LANES:(pair + 1) * LANES]
        s2 = lax.dot_general(qp, kc2_ref[...], nt, preferred_element_type=F32)
        probs = []
        for hh in range(2):
            s = jnp.where(cmask, s2[:, hh * LANES:(hh + 1) * LANES], NEG)
            mx = jnp.max(s, axis=-1, keepdims=True)
            e = jnp.exp2(s - mx) * cmask_f
            pr = e / jnp.maximum(jnp.sum(e, axis=-1, keepdims=True), 1.0)
            probs.append(pr)
            psum = psum + pr
        p2 = jnp.concatenate(probs, axis=1).astype(BF16)
        o_pair = jnp.dot(p2, vc2_ref[...], preferred_element_type=F32)
        gate = jnp.where(lane < A_HEAD_DIM, gates[:, (2 * pair) * LANES:(2 * pair + 1) * LANES],
                         gates[:, (2 * pair + 1) * LANES:(2 * pair + 2) * LANES])
        ocmp_ref[:, pair * LANES:(pair + 1) * LANES] = (gate * o_pair).astype(ocmp_ref.dtype)
        yield

    ovt = ovt_ref[...]
    imp_t = sum(lax.dot_general(ovt, part, nt, preferred_element_type=F32) for part in _split3(psum))
    imp_t = imp_t[AUG_LANE:AUG_LANE + n_sel_blocks, :]
    yield
    jb = lax.broadcasted_iota(jnp.int32, imp_t.shape, 0)
    tcol = qi * tq + lax.broadcasted_iota(jnp.int32, imp_t.shape, 1)
    cur = tcol // SEL_BLOCK
    valid = (SEL_BLOCK * jb) <= tcol
    forced = valid & ((jb == 0) | (jb == cur) | (jb == cur - 1))
    score = jnp.where(valid, imp_t + FORCE_BONUS * forced.astype(F32), NEG)
    rank = jnp.zeros(imp_t.shape, F32)
    for i in range(n_sel_blocks):
        other = score[i:i + 1, :]
        beats = (other > score) | ((other == score) & (jb > i))
        rank = rank + beats.astype(F32)
    selb = jnp.where(rank < float(min(SEL_TOPK, n_sel_blocks)), 0.0, NEG)
    yield
    aug_t = jnp.concatenate([jnp.zeros((AUG_LANE, tq), F32), selb,
                             jnp.zeros((LANES - AUG_LANE - n_sel_blocks, tq), F32)], axis=0)
    return aug_t.T.astype(BF16)


def _flash_init(m_scr, acc_scr, l_scr=None):
    m_scr[...] = jnp.full(m_scr.shape, M_INIT, F32)
    acc_scr[...] = jnp.zeros(acc_scr.shape, F32)
    if l_scr is not None:
        l_scr[...] = jnp.zeros(l_scr.shape, F32)


def _scores(q, k):
    return lax.dot_general(q, k, (((1,), (1,)), ((), ())), preferred_element_type=F32)


def _add_bias(s, bias, reps):
    tq, tk = bias.shape
    return (s.reshape(reps, tq, tk) + bias[None]).reshape(reps * tq, tk)


def _flash_update(s, v, m_scr, acc_scr, l_scr=None, first=False):
    chunks = [s[:, c * LANES:(c + 1) * LANES] for c in range(s.shape[1] // LANES)]
    mx = functools.reduce(jnp.maximum, chunks)
    if first:
        assert l_scr is None
        m_new = jnp.broadcast_to(jnp.max(mx, axis=-1, keepdims=True), mx.shape)
        p = jnp.concatenate([jnp.exp2((ch - m_new).astype(v.dtype)) for ch in chunks], axis=1)
        acc_scr[...] = jnp.dot(p, v, preferred_element_type=F32)
        m_scr[...] = m_new
        return
    m_prev = m_scr[...]
    m_new = jnp.maximum(m_prev, jnp.max(mx, axis=-1, keepdims=True))
    alpha = jnp.exp2(m_prev - m_new)
    if l_scr is None:
        p = jnp.concatenate([jnp.exp2((ch - m_new).astype(v.dtype)) for ch in chunks], axis=1)
    else:
        ps = [jnp.exp2(ch - m_new) for ch in chunks]
        p = jnp.concatenate(ps, axis=1).astype(v.dtype)
        l_scr[...] = alpha * l_scr[...] + functools.reduce(jnp.add, ps)
    acc_scr[...] = alpha * acc_scr[...] + jnp.dot(p, v, preferred_element_type=F32)
    m_scr[...] = m_new


def _gated_head(acc, gates, r):
    lane = lax.broadcasted_iota(jnp.int32, acc.shape, 1)
    scale = gates[:, r * LANES:(r + 1) * LANES] / acc[:, ONES_LANE:ONES_LANE + 1]
    return jnp.where(lane < A_HEAD_DIM, acc * scale, 0.0)


def _store_head_pair(o_ref, pair, even, odd):
    o_ref[:, pair * LANES:(pair + 1) * LANES] = (even + pltpu.roll(odd, A_HEAD_DIM, 1)).astype(o_ref.dtype)


def _win_bias_table(t, span, window):
    delta = np.arange(t)[:, None] - np.arange(span)[None, :]
    dist = np.stack([n * t + delta for n in range(span // t)])
    return np.where((dist >= 0) & (dist < window), 0.0, NEG).astype(np.float32)


def _interleave(order, *gens):
    results = [None] * len(gens)
    done = [False] * len(gens)

    def step(n):
        if not done[n]:
            try:
                next(gens[n])
            except StopIteration as stop:
                results[n], done[n] = stop.value, True
    for n in order:
        step(n)
    for n in range(len(gens)):
        while not done[n]:
            step(n)
    return results


def _window_branch(stacked_q, kw_ref, vw_ref, bias_ref, gates, owin_ref, m_scr, acc_scr, s0_scr, s1_scr,
                   qi, t, span):
    n_tiles = span // t
    start = pl.multiple_of(jnp.maximum((qi + 1) * t - span, 0), t)
    variant = jnp.minimum(qi, n_tiles - 1)
    rows_of = lambda ref, w: ref[pl.ds(pl.multiple_of(start + w * t, t), t), :]
    bufs = (s0_scr, s1_scr)

    def score_into(w):
        bufs[w % 2][...] = _scores(stacked_q(), rows_of(kw_ref, w))

    def fold(w):
        s = _add_bias(bufs[w % 2][...], bias_ref[variant, :, w * t:(w + 1) * t], A_REP)
        _flash_update(s, rows_of(vw_ref, w), m_scr, acc_scr, first=(w == 0))

    score_into(0)
    yield
    for w in range(n_tiles):
        if w + 1 < n_tiles:
            score_into(w + 1)
        fold(w)
        yield
    heads = [_gated_head(acc_scr[r * t:(r + 1) * t, :], gates, r) for r in range(A_REP)]
    for pair in range(A_REP // 2):
        _store_head_pair(owin_ref, pair, heads[2 * pair], heads[2 * pair + 1])


def _nsa_kernel(q_ref, kc2_ref, vc2_ref, ovt_ref, ks_ref, vs_ref, kw_ref, vw_ref, ecmp_ref, esel_ref,
                ewin_ref, bias_ref, ocmp_ref, osel_ref, owin_ref, qaug_scr, m_scr, acc_scr, s0_scr, s1_scr,
                mw_scr, accw_scr, sw0_scr, sw1_scr, *, t, span, n_sel_blocks):
    qi = pl.program_id(2)
    rows_of = lambda ref, kj: ref[pl.ds(pl.multiple_of(kj * t, t), t), :]
    gate_blk = rows_of(ks_ref, qi)
    lo = lax.broadcasted_iota(jnp.int32, (t, LANES), 1) < A_HEAD_DIM
    for r in range(A_REP):
        qp = q_ref[:, (r // 2) * LANES:(r // 2 + 1) * LANES]
        if r % 2:
            qp = jnp.concatenate([qp[:, A_HEAD_DIM:], qp[:, :A_HEAD_DIM]], axis=1)
        qaug_scr[r] = jnp.where(lo, qp, jnp.zeros_like(qp))
    stacked_q = lambda: qaug_scr[...].reshape(A_REP * t, LANES)

    cmp = _compressed_branch(q_ref, kc2_ref, vc2_ref, ovt_ref, _branch_gates(gate_blk, ecmp_ref),
                             ocmp_ref, qi, t, n_sel_blocks)
    win = _window_branch(stacked_q, kw_ref, vw_ref, bias_ref, _branch_gates(gate_blk, ewin_ref), owin_ref,
                         mw_scr, accw_scr, sw0_scr, sw1_scr, qi, t, span)
    aug, _ = _interleave(NSA_STAGE_ORDER, cmp, win)
    qaug_scr[...] = qaug_scr[...] + aug[None]
    _flash_init(m_scr, acc_scr)

    def score_into(buf, kj):
        buf[...] = _scores(stacked_q(), rows_of(ks_ref, kj))

    def fold(buf, kj, causal=False):
        s = buf[...]
        if causal:
            delta = (lax.broadcasted_iota(jnp.int32, (t, t), 0)
                     - lax.broadcasted_iota(jnp.int32, (t, t), 1))
            s = _add_bias(s, jnp.where(delta >= 0, 0.0, NEG), A_REP)
        _flash_update(s, rows_of(vs_ref, kj), m_scr, acc_scr)

    first = qi % 2
    score_into(s1_scr, 0)
    score_into(s0_scr, first)

    @pl.when(first == 1)
    def _():
        fold(s1_scr, 0)

    def body(i, carry):
        kj = first + 2 * i
        score_into(s1_scr, kj + 1)
        fold(s0_scr, kj)
        score_into(s0_scr, kj + 2)
        fold(s1_scr, kj + 1)
        return carry
    lax.fori_loop(0, (qi - first) // 2, body, 0)

    fold(s0_scr, qi, causal=True)
    gates = _branch_gates(gate_blk, esel_ref)
    sel_heads = [_gated_head(acc_scr[r * t:(r + 1) * t, :], gates, r) for r in range(A_REP)]
    for pair in range(A_REP // 2):
        _store_head_pair(osel_ref, pair, sel_heads[2 * pair], sel_heads[2 * pair + 1])


def _nsa_attend(slab, kc2, vc2, B, seq):
    t = T_ATT
    nq = seq // t
    G = A_KV_GROUPS
    rows = A_REP * t
    span = min(((WIN_SIZE - 1 + t - 1) // t + 1) * t, seq)
    ovt = jnp.asarray(_overlap_table_t(seq), BF16)
    ecmp, esel, ewin = (jnp.asarray(_gate_expand_table(br), BF16) for br in range(3))
    bias = jnp.asarray(_win_bias_table(t, span, WIN_SIZE))
    kv = lambda col: pl.BlockSpec((seq, LANES), lambda b, g, q: (b, col // LANES + g))
    const = lambda a: pl.BlockSpec(a.shape, lambda b, g, q: (0,) * a.ndim)
    ckv = pl.BlockSpec((None, None) + kc2.shape[2:], lambda b, g, q: (b, g, 0, 0))
    out = jax.ShapeDtypeStruct((B * seq, A_WIDTH), BF16)
    ospec = pl.BlockSpec((t, A_REP * A_HEAD_DIM), lambda b, g, q: (b * nq + q, g))
    return pl.pallas_call(
        functools.partial(_nsa_kernel, t=t, span=span, n_sel_blocks=seq // SEL_BLOCK),
        grid=(B, G, nq),
        in_specs=[pl.BlockSpec((t, A_REP * A_HEAD_DIM), lambda b, g, q: (b * nq + q, g)),
                  ckv, ckv, const(ovt), kv(COL_KS), kv(COL_VS), kv(COL_KW), kv(COL_VW),
                  const(ecmp), const(esel), const(ewin), const(bias)],
        out_specs=[ospec, ospec, ospec],
        out_shape=[out, out, out],
        scratch_shapes=[pltpu.VMEM((A_REP, t, LANES), BF16)]
        + [pltpu.VMEM((rows, LANES), F32), pltpu.VMEM((rows, LANES), F32),
           pltpu.VMEM((rows, t), F32), pltpu.VMEM((rows, t), F32)] * 2,
        compiler_params=_cparams(("arbitrary", "arbitrary", "arbitrary")),
        name="nsa",
    )(slab, kc2, vc2, ovt, slab, slab, slab, slab, ecmp, esel, ewin, bias)


DILATIONS = ((128, 1), (512, 4), (2048, 16))
TQ_DIL = 512
TK_DIL = 256


def _dil_bias_tables(tq, tk, seq):
    w_near = max(w for w, r in DILATIONS if w < seq)
    near_keys = tq + -(-w_near // tk) * tk
    w_far, r_far = DILATIONS[-1]
    assert w_far >= seq and tq % r_far == 0 and tk % r_far == 0 and near_keys == 2 * tq
    assert tq % (2 * tk) == 0
    delta = np.arange(tq)[:, None] - np.arange(near_keys)[None, :]
    near = []
    for base in (near_keys - tq, 0):
        dist = base + delta
        mult = sum(((dist >= 0) & (dist <= w) & (dist % r == 0)).astype(np.float64) for w, r in DILATIONS)
        near.append(np.where(mult > 0, np.log2(np.maximum(mult, 1.0)), NEG))
    far = np.where(delta[:, :tk] % r_far == 0, 0.0, NEG)
    return np.stack(near).astype(np.float32), far.astype(np.float32)


def _dil_kernel(q_ref, k_ref, v_ref, near_ref, far_ref, o_ref, m_scr, acc_scr, l_scr, s0_scr, s1_scr,
                *, tq, tk):
    qi = pl.program_id(2)
    q = q_ref[...]
    near_keys = near_ref.shape[-1]
    _flash_init(m_scr, acc_scr, l_scr)
    update = functools.partial(_flash_update, m_scr=m_scr, acc_scr=acc_scr, l_scr=l_scr)
    rows_of = lambda ref, kj: ref[pl.ds(pl.multiple_of(kj * tk, tk), tk), :]
    variant = ((qi + 1) * tq < near_keys).astype(jnp.int32)
    n_far = jnp.maximum((qi + 1) * tq - near_keys, 0) // tk
    bufs = (s0_scr, s1_scr)

    def score_into(buf, kj):
        buf[...] = _scores(q, rows_of(k_ref, kj))

    def fold(buf, kj):
        update(buf[...] + far_ref[...], rows_of(v_ref, kj))

    score_into(s0_scr, 0)

    def body(i, carry):
        kj = 2 * i
        score_into(s1_scr, kj + 1)
        fold(s0_scr, kj)
        score_into(s0_scr, kj + 2)
        fold(s1_scr, kj + 1)
        return carry
    lax.fori_loop(0, n_far // 2, body, 0)

    n_near = near_keys // tk
    for w in range(n_near):
        if w + 1 < n_near:
            score_into(bufs[(w + 1) % 2], n_far + w + 1)
        update(bufs[w % 2][...] + near_ref[variant, :, w * tk:(w + 1) * tk], rows_of(v_ref, n_far + w))
    l = jnp.sum(l_scr[...], axis=-1, keepdims=True)
    o_ref[...] = (acc_scr[...] / l).astype(o_ref.dtype)


def _dil(slab, B, seq):
    tq, tk = TQ_DIL, TK_DIL
    nq = seq // tq
    H = B_HEADS
    near, far = (jnp.asarray(a) for a in _dil_bias_tables(tq, tk, seq))
    return pl.pallas_call(
        functools.partial(_dil_kernel, tq=tq, tk=tk),
        grid=(B, H, nq),
        in_specs=[pl.BlockSpec((tq, LANES), lambda b, h, q: (b * nq + q, COL_QB // LANES + h)),
                  pl.BlockSpec((seq, LANES), lambda b, h, q: (b, COL_KB // LANES + h)),
                  pl.BlockSpec((seq, LANES), lambda b, h, q: (b, COL_VB // LANES + h)),
                  pl.BlockSpec(near.shape, lambda b, h, q: (0, 0, 0)),
                  pl.BlockSpec(far.shape, lambda b, h, q: (0, 0))],
        out_specs=pl.BlockSpec((tq, LANES), lambda b, h, q: (b * nq + q, h)),
        out_shape=jax.ShapeDtypeStruct((B * seq, B_WIDTH), BF16),
        scratch_shapes=[pltpu.VMEM((tq, LANES), F32)] * 3 + [pltpu.VMEM((tq, tk), F32)] * 2,
        compiler_params=_cparams(("arbitrary", "arbitrary", "arbitrary")),
        name="dilated",
    )(slab, slab, slab, near, far)


def _silu(z):
    return z * jax.nn.sigmoid(z)


def _out_kernel(ocmp_ref, osel_ref, owin_ref, za_ref, ob_ref, zb_ref, ma_ref, mb_ref,
                x_ref, gate_ref, gpost_ref, wa_ref, wb_ref, wo_ref, o_ref):
    oa = ocmp_ref[...].astype(F32) + osel_ref[...].astype(F32) + owin_ref[...].astype(F32)
    a_in = (oa * _silu(za_ref[...].astype(F32))).astype(BF16)
    ya = jnp.dot(a_in, wa_ref[...], preferred_element_type=F32)
    b_in = (ob_ref[...].astype(F32) * _silu(zb_ref[...].astype(F32))).astype(BF16)
    yb = jnp.dot(b_in, wb_ref[...], preferred_element_type=F32)
    merged = (jax.nn.sigmoid(ma_ref[...].astype(F32)) * ya
              + jax.nn.sigmoid(mb_ref[...].astype(F32)) * yb)
    out = jnp.dot(merged.astype(BF16), wo_ref[...], preferred_element_type=F32)
    y = out * lax.rsqrt(jnp.mean(out * out, axis=-1, keepdims=True) + EPS) * gpost_ref[...]
    o_ref[...] = x_ref[...] + gate_ref[0] * y


def _out(ocmp, osel, owin, ob, slab, x2d, ada3, g_post, wa, wb, wo, seq):
    M, D = x2d.shape
    tm = 256
    per_b = seq // tm
    const = lambda a: pl.BlockSpec(a.shape, lambda i: (0,) * a.ndim, pipeline_mode=pl.Buffered(1))
    row = lambda w, cb: pl.BlockSpec((tm, w), lambda i: (i, cb))
    return pl.pallas_call(
        _out_kernel,
        grid=(M // tm,),
        in_specs=[row(A_WIDTH, 0), row(A_WIDTH, 0), row(A_WIDTH, 0),
                  row(A_WIDTH, COL_ZA // A_WIDTH), row(B_WIDTH, 0), row(B_WIDTH, COL_ZB // B_WIDTH),
                  row(D_MODEL, COL_MA // D_MODEL), row(D_MODEL, COL_MB // D_MODEL),
                  row(D, 0),
                  pl.BlockSpec((1, 1, D), lambda i: (i // per_b, 0, 2)),
                  pl.BlockSpec((1, D), lambda i: (0, 0)),
                  const(wa), const(wb), const(wo)],
        out_specs=pl.BlockSpec((tm, D), lambda i: (i, 0)),
        out_shape=jax.ShapeDtypeStruct((M, D), F32),
        compiler_params=_cparams(("arbitrary",)),
        name="out",
    )(ocmp, osel, owin, slab, ob, slab, slab, slab, x2d, ada3, g_post.reshape(1, D), wa, wb, wo)


def _layer(x, c, positions, w_ada, b_ada, g_pre, g_post, w_in, pe_ck, pe_cv, w_ck1, w_ck2,
           w_cv1, w_cv2, w_br_a, w_br_b, w_out):
    B, S, D = x.shape
    x2d = x.reshape(B * S, D)
    ada3 = _ada(c, w_ada, b_ada).reshape(B, 1, 3 * D)
    slab, kcv = _proj(x2d, ada3, g_pre, positions.reshape(B * S, 1), _build_w_slab_t(jnp.swapaxes(w_in, 0, 1)), S)

    nblk = S // CMP_STRIDE
    assert (S - CMP_BLOCK) // CMP_STRIDE + 1 == nblk - 1
    cmp_end = np.minimum(np.arange(nblk) * CMP_STRIDE + CMP_BLOCK - 1, S - 1)
    posc = positions[:, cmp_end].reshape(B, nblk, 1)
    dup = lambda w: jnp.concatenate([w, w], axis=1).astype(BF16)
    pe2 = jnp.concatenate([pe_ck, pe_cv], axis=1)
    kc2, vc2 = _cmp_kv(kcv, pe2, posc, _build_cmp_w1(w_ck1, w_cv1), dup(w_ck2), dup(w_cv2), B, S)

    ocmp, osel, owin = _nsa_attend(slab, kc2, vc2, B, S)
    ob = _dil(slab, B, S)
    out = _out(ocmp, osel, owin, ob, slab, x2d, ada3, g_post,
               w_br_a.astype(BF16), w_br_b.astype(BF16), w_out.astype(BF16), S)
    return out.reshape(B, S, D)


def kernel(x, c, positions, w_ada, b_ada, g_pre, g_post, w_in, pe_ck, pe_cv, w_ck1, w_ck2, w_cv1, w_cv2, w_br_a, w_br_b, w_out):
    h = x
    for layer in range(w_ada.shape[0]):
        h = _layer(h, c, positions, w_ada[layer], b_ada[layer], g_pre[layer], g_post[layer],
                   w_in[layer], pe_ck[layer], pe_cv[layer], w_ck1[layer], w_ck2[layer],
                   w_cv1[layer], w_cv2[layer], w_br_a[layer], w_br_b[layer], w_out[layer])
    return h
```

```python
import functools

import numpy as np
import jax
import jax.numpy as jnp
from jax import lax
from jax.experimental import pallas as pl
from jax.experimental.pallas import tpu as pltpu

F32 = jnp.float32
BF16 = jnp.bfloat16

D_MODEL = 2048
A_HEADS = 16
A_HEAD_DIM = 64
A_KV_GROUPS = 4
A_REP = A_HEADS // A_KV_GROUPS
A_WIDTH = A_HEADS * A_HEAD_DIM
A_KV_WIDTH = A_KV_GROUPS * A_HEAD_DIM
CMP_BLOCK = 32
CMP_STRIDE = 16
CMP_HIDDEN = 4 * A_HEAD_DIM
SEL_BLOCK = 64
SEL_TOPK = 16
WIN_SIZE = 512
FORCE_BONUS = 1.0e4
B_HEADS = 8
B_HEAD_DIM = 128
B_WIDTH = B_HEADS * B_HEAD_DIM
ROPE_THETA = 500000.0
EPS = 1e-6
NEG = -1e30
M_INIT = -1.0e38
LOG2E = 1.4426950408889634

LANES = 128
MXU_N = 256
VMEM_LIMIT = 56 * 1024 * 1024

TN = 2048
COL_QA = 0
COL_KS = 1024
COL_KW = 1536
COL_QB = 2048
COL_KB = 3072
COL_VB = 4096
COL_ZA = 5120
COL_ZB = 6144
COL_VS = 7168
COL_VW = 7680
COL_MA = 8192
COL_MB = 10240
N_SLAB = 12288
COL_KCV = N_SLAB
N_PROJ = N_SLAB + 2 * A_KV_WIDTH
GATE_LANE = 96
AUG_LANE = 64
ONES_LANE = 64

T_ATT = 256
NSA_STAGE_ORDER = (0, 0, 0, 0, 0)


def _cparams(sem):
    return pltpu.CompilerParams(dimension_semantics=sem, vmem_limit_bytes=VMEM_LIMIT)


def _ada_kernel(c_ref, w_ref, b_ref, o_ref):
    o_ref[...] = jnp.dot(c_ref[...], w_ref[...], preferred_element_type=F32,
                         precision=lax.Precision.HIGHEST) + b_ref[...]


def _ada(c, w_ada, b_ada):
    B, D = c.shape
    N = w_ada.shape[1]
    tn = 768
    return pl.pallas_call(
        _ada_kernel,
        grid=(N // tn,),
        in_specs=[pl.BlockSpec((B, D), lambda j: (0, 0)),
                  pl.BlockSpec((D, tn), lambda j: (0, j)),
                  pl.BlockSpec((1, tn), lambda j: (0, j))],
        out_specs=pl.BlockSpec((B, tn), lambda j: (0, j)),
        out_shape=jax.ShapeDtypeStruct((B, N), F32),
        compiler_params=_cparams(("arbitrary",)),
        name="ada",
    )(c, w_ada, b_ada.reshape(1, N))


def _rope_inv_tables():
    lane = np.arange(LANES)
    inv_a8 = (ROPE_THETA ** (-np.arange(0, 16, 2) / 16)).astype(np.float32)
    inv_b16 = (ROPE_THETA ** (-np.arange(0, 32, 2) / 32)).astype(np.float32)
    la = lane % A_HEAD_DIM
    inv_a = np.where(la < 16, inv_a8[la % 8], 0.0).astype(np.float32)
    inv_b = np.where(lane < 32, inv_b16[lane % 16], 0.0).astype(np.float32)
    return inv_a.reshape(1, LANES), inv_b.reshape(1, LANES)


def _rope_tables(pos_f, inv, period, half):
    ang = pos_f * inv
    c, s = jnp.cos(ang), jnp.sin(ang)
    lane = lax.broadcasted_iota(jnp.int32, ang.shape, 1) % period
    cos_t = jnp.where(lane < 2 * half, c, 1.0)
    sin_t = jnp.where(lane < half, -s, jnp.where(lane < 2 * half, s, 0.0))
    return cos_t, sin_t


A_TABLE_LANE0 = 32


def _rope_inv_combined():
    inv_a, inv_b = _rope_inv_tables()
    comb = inv_b.copy()
    comb[0, A_TABLE_LANE0:A_TABLE_LANE0 + 16] = inv_a[0, :16]
    return comb


def _rope_tables_ab(pos_f, inv_comb):
    ang = pos_f * inv_comb
    c, s = jnp.cos(ang), jnp.sin(ang)
    lane = lax.broadcasted_iota(jnp.int32, ang.shape, 1)
    cos_b = jnp.where(lane < 32, c, 1.0)
    sin_b = jnp.where(lane < 16, -s, jnp.where(lane < 32, s, 0.0))
    la = lane % A_HEAD_DIM
    src = jnp.where(la < 16, A_TABLE_LANE0 + la, lane)
    ca, sa = jnp.take_along_axis(c, src, axis=1), jnp.take_along_axis(s, src, axis=1)
    cos_a = jnp.where(la < 16, ca, 1.0)
    sin_a = jnp.where(la < 8, -sa, jnp.where(la < 16, sa, 0.0))
    return cos_a, sin_a, cos_b, sin_b


def _rope_apply(x, cos_t, sin_t, half):
    lane = lax.broadcasted_iota(jnp.int32, x.shape, 1)
    partner = jnp.take_along_axis(x, lane ^ half, axis=1)
    return x * cos_t + partner * sin_t


def _proj_kernel(x_ref, shift_ref, scale_ref, gpre_ref, pos_ref, inv_ref, w_ref,
                 o_ref, kcv_ref, h_scr, ca, sa, cb, sb, *, tm, seq):
    i = pl.program_id(0)
    j = pl.program_id(1)

    @pl.when(j == 0)
    def _():
        x = x_ref[...]
        ms = jnp.mean(x * x, axis=-1, keepdims=True)
        y = x * lax.rsqrt(ms + EPS) * gpre_ref[...]
        h = y * (1.0 + scale_ref[0]) + shift_ref[0]
        h_scr[...] = h.astype(BF16)
        pos_f = pos_ref[...].astype(F32)
        ca[...], sa[...], cb[...], sb[...] = _rope_tables_ab(pos_f, inv_ref[...])

    def tile(epilogues):
        for part, epilogue in enumerate(epilogues):
            c0 = part * MXU_N
            acc = lax.dot_general(h_scr[...], w_ref[c0:c0 + MXU_N, :], (((1,), (1,)), ((), ())),
                                  preferred_element_type=F32)
            epilogue(acc, c0)

    def rope_epilogue(tabs, half, mul, first_head_only, onehot):
        def epilogue(acc, c0):
            cos_t, sin_t = tabs[0][...], tabs[1][...]
            lane = lax.broadcasted_iota(jnp.int32, (tm, LANES), 1)
            if first_head_only:
                keep = lane < A_HEAD_DIM
                cos_t = jnp.where(keep, cos_t, 1.0)
                sin_t = jnp.where(keep, sin_t, 0.0)
            if onehot:
                t = (i * tm + lax.broadcasted_iota(jnp.int32, (tm, LANES), 0)) % seq
                hot = (lane >= AUG_LANE) & (lane < AUG_LANE + seq // SEL_BLOCK) & \
                      ((t // SEL_BLOCK) == (lane - AUG_LANE))
            for cidx in range(MXU_N // LANES):
                xc = acc[:, cidx * LANES:(cidx + 1) * LANES]
                r = _rope_apply(xc, cos_t, sin_t, half)
                if mul != 1.0:
                    r = r * mul
                if onehot:
                    r = jnp.where(hot, 1.0, r)
                o_ref[:, c0 + cidx * LANES:c0 + (cidx + 1) * LANES] = r.astype(o_ref.dtype)
        return epilogue

    def plain_epilogue(acc, c0):
        o_ref[:, c0:c0 + MXU_N] = acc.astype(o_ref.dtype)

    def ones_lane_epilogue(acc, c0):
        lane = lax.broadcasted_iota(jnp.int32, acc.shape, 1) % LANES
        o_ref[:, c0:c0 + MXU_N] = jnp.where(lane == ONES_LANE, 1.0, acc).astype(o_ref.dtype)

    def kcv_epilogue(acc, c0):
        kcv_ref[:, c0:c0 + MXU_N] = acc

    ta = (ca, sa)
    tb = (cb, sb)
    parts = lambda col0, col1, epi: [((col0 // MXU_N + n), epi) for n in range((col1 - col0) // MXU_N)]
    kinds = dict(
        parts(COL_QA, COL_KS, rope_epilogue(ta, 8, A_HEAD_DIM ** -0.5 * LOG2E, False, False))
        + parts(COL_KS, COL_KW, rope_epilogue(ta, 8, 1.0, True, True))
        + parts(COL_KW, COL_QB, rope_epilogue(ta, 8, 1.0, True, False))
        + parts(COL_QB, COL_KB, rope_epilogue(tb, 16, B_HEAD_DIM ** -0.5 * LOG2E, False, False))
        + parts(COL_KB, COL_VB, rope_epilogue(tb, 16, 1.0, False, False))
        + parts(COL_VB, COL_VS, plain_epilogue)
        + parts(COL_VS, COL_MA, ones_lane_epilogue)
        + parts(COL_MA, COL_KCV, plain_epilogue)
        + parts(COL_KCV, N_PROJ, None))
    per_tile = TN // MXU_N
    steps = {}
    for jt in range(pl.cdiv(N_PROJ, TN)):
        tile_kinds = tuple(kinds[p] for p in range(jt * per_tile, (jt + 1) * per_tile) if p in kinds)
        if tile_kinds[0] is None:
            assert all(k is None for k in tile_kinds) and len(tile_kinds) * MXU_N == kcv_ref.shape[1]
            tile_kinds = (kcv_epilogue,) * len(tile_kinds)
        steps.setdefault(tile_kinds, []).append(jt)
    for tile_kinds, jts in steps.items():
        pl.when(functools.reduce(jnp.logical_or, [j == jt for jt in jts]))(
            functools.partial(tile, tile_kinds))


def _proj(x2d, ada3, g_pre, pos2d, w_slab_t, seq):
    M, D = x2d.shape
    tm = 1024
    per_b = seq // tm
    tab = pltpu.VMEM((tm, LANES), F32)
    last_slab_tile = N_SLAB // TN - 1
    n_kcv = N_PROJ - N_SLAB
    return pl.pallas_call(
        functools.partial(_proj_kernel, tm=tm, seq=seq),
        grid=(M // tm, pl.cdiv(N_PROJ, TN)),
        in_specs=[pl.BlockSpec((tm, D), lambda i, j: (i, 0)),
                  pl.BlockSpec((1, 1, D), lambda i, j: (i // per_b, 0, 0)),
                  pl.BlockSpec((1, 1, D), lambda i, j: (i // per_b, 0, 1)),
                  pl.BlockSpec((1, D), lambda i, j: (0, 0)),
                  pl.BlockSpec((tm, 1), lambda i, j: (i, 0)),
                  pl.BlockSpec((1, LANES), lambda i, j: (0, 0)),
                  pl.BlockSpec((TN, D), lambda i, j: (j, 0))],
        out_specs=[pl.BlockSpec((tm, TN), lambda i, j: (i, jnp.minimum(j, last_slab_tile))),
                   pl.BlockSpec((tm, n_kcv), lambda i, j: (i, 0))],
        out_shape=[jax.ShapeDtypeStruct((M, N_SLAB), BF16),
                   jax.ShapeDtypeStruct((M, n_kcv), F32)],
        scratch_shapes=[pltpu.VMEM((tm, D), BF16), tab, tab, tab, tab],
        compiler_params=_cparams(("arbitrary", "arbitrary")),
        name="proj",
    )(x2d, ada3, ada3, g_pre.reshape(1, D), pos2d, jnp.asarray(_rope_inv_combined()), w_slab_t)


_SRC_SIZES = (A_WIDTH,) + (A_KV_WIDTH,) * 6 + (A_HEADS * 3, A_WIDTH) + (B_WIDTH,) * 4 + (D_MODEL, D_MODEL)
_SRC = [int(v) for v in np.concatenate([[0], np.cumsum(_SRC_SIZES)])]
(SRC_QA, SRC_KC, SRC_VC, SRC_KS, SRC_VS, SRC_KW, SRC_VW, SRC_GA, SRC_ZA, SRC_QB, SRC_KB, SRC_VB,
 SRC_ZB, SRC_MA, SRC_MB, SRC_END) = _SRC


def _w_slab_kernel(w_ref, o_ref):
    cols = w_ref.shape[1]

    def put(dst, src, width):
        o_ref[dst:dst + width, :] = w_ref[src:src + width, :].astype(o_ref.dtype)

    def clear(dst, width):
        o_ref[dst:dst + width, :] = jnp.zeros((width, cols), o_ref.dtype)

    put(COL_QA, SRC_QA, A_WIDTH)
    put(COL_QB, SRC_QB, B_WIDTH)
    put(COL_KB, SRC_KB, B_WIDTH)
    put(COL_VB, SRC_VB, B_WIDTH)
    put(COL_ZA, SRC_ZA, A_WIDTH)
    put(COL_ZB, SRC_ZB, B_WIDTH)
    put(COL_MA, SRC_MA, D_MODEL)
    put(COL_MB, SRC_MB, D_MODEL)
    ngate = A_REP * 3
    for g in range(A_KV_GROUPS):
        d = g * LANES
        s = g * A_HEAD_DIM
        put(COL_KS + d, SRC_KS + s, A_HEAD_DIM)
        clear(COL_KS + d + A_HEAD_DIM, GATE_LANE - A_HEAD_DIM)
        put(COL_KS + d + GATE_LANE, SRC_GA + g * ngate, ngate)
        clear(COL_KS + d + GATE_LANE + ngate, LANES - GATE_LANE - ngate)
        for col, src in ((COL_KW, SRC_KW), (COL_VS, SRC_VS), (COL_VW, SRC_VW)):
            put(col + d, src + s, A_HEAD_DIM)
            clear(col + d + A_HEAD_DIM, LANES - A_HEAD_DIM)
        put(COL_KCV + d, SRC_KC + s, A_HEAD_DIM)
        put(COL_KCV + d + A_HEAD_DIM, SRC_VC + s, A_HEAD_DIM)
    clear(N_PROJ, o_ref.shape[0] - N_PROJ)


def _build_w_slab_t(w_in_t):
    n_in, D = w_in_t.shape
    assert n_in == SRC_END
    cols = 256
    n_rows = pl.cdiv(N_PROJ, TN) * TN
    return pl.pallas_call(
        _w_slab_kernel,
        grid=(D // cols,),
        in_specs=[pl.BlockSpec((n_in, cols), lambda i: (0, i))],
        out_specs=pl.BlockSpec((n_rows, cols), lambda i: (0, i)),
        out_shape=jax.ShapeDtypeStruct((n_rows, D), BF16),
        compiler_params=_cparams(("arbitrary",)),
        name="w_slab",
    )(w_in_t)


def _cmp_kv_kernel(*refs):
    groups = A_KV_GROUPS
    t_refs = refs[:groups]
    pe_ref, posc_ref, inva_ref, w1_ref, wk2_ref, wv2_ref, kc2_ref, vc2_ref = refs[groups:]
    nblk = t_refs[0].shape[0] // CMP_STRIDE
    rows = groups * nblk
    hidden2 = w1_ref.shape[-1]
    p_lo = jnp.zeros((rows, hidden2), F32)
    p_hi = jnp.zeros((rows, hidden2), F32)
    def tokens(l):
        return jnp.concatenate([t_ref[pl.ds(l, nblk, stride=CMP_STRIDE), :] for t_ref in t_refs], axis=0)

    def pair(x0, x1, l):
        xs = jnp.concatenate([x0 + pe_ref[l:l + 1, :], x1 + pe_ref[l + 1:l + 2, :]], axis=1)
        return jnp.dot(xs.astype(BF16), w1_ref[l:l + 2].reshape(2 * LANES, hidden2),
                       preferred_element_type=F32)

    for l in range(0, CMP_STRIDE, 2):
        x0, x1 = tokens(l), tokens(l + 1)
        p_lo += pair(x0, x1, l)
        p_hi += pair(x0, x1, CMP_STRIDE + l)
    hid = p_lo + pltpu.roll(p_hi, rows - 1, 0)
    act = jax.nn.gelu(hid).astype(BF16)
    kk = jnp.dot(act[:, :CMP_HIDDEN], wk2_ref[...], preferred_element_type=F32)
    vv = jnp.dot(act[:, CMP_HIDDEN:], wv2_ref[...], preferred_element_type=F32)
    pos_f = jnp.concatenate([posc_ref[...]] * groups, axis=0).astype(F32)
    cos_t, sin_t = _rope_tables(pos_f, inva_ref[...], A_HEAD_DIM, 8)
    kk = _rope_apply(kk, cos_t, sin_t, 8)
    lane = lax.broadcasted_iota(jnp.int32, (nblk, LANES), 1)
    lo = lane < A_HEAD_DIM
    for g in range(groups):
        kg = kk[g * nblk:(g + 1) * nblk]
        vg = vv[g * nblk:(g + 1) * nblk]
        kc2_ref[g, 0:nblk, :] = jnp.where(lo, kg, 0.0).astype(kc2_ref.dtype)
        kc2_ref[g, nblk:2 * nblk, :] = jnp.where(lo, 0.0, kg).astype(kc2_ref.dtype)
        vc2_ref[g, 0:nblk, :] = jnp.where(lo, vg, 0.0).astype(vc2_ref.dtype)
        vc2_ref[g, nblk:2 * nblk, :] = jnp.where(lo, 0.0, vg).astype(vc2_ref.dtype)


def _cmp_kv(kcv, pe2, posc, w1, wk2d, wv2d, B, seq):
    G = A_KV_GROUPS
    nblk = seq // CMP_STRIDE
    inv_a, _ = _rope_inv_tables()
    full = lambda a: pl.BlockSpec(a.shape, lambda b: (0,) * a.ndim)
    out = jax.ShapeDtypeStruct((B, G, 2 * nblk, LANES), BF16)
    ospec = pl.BlockSpec((None, G, 2 * nblk, LANES), lambda b: (b, 0, 0, 0))
    return pl.pallas_call(
        _cmp_kv_kernel,
        grid=(B,),
        in_specs=[pl.BlockSpec((seq, LANES), lambda b, g=g: (b, g)) for g in range(G)] + [
                  full(pe2),
                  pl.BlockSpec((None, nblk, 1), lambda b: (b, 0, 0)),
                  pl.BlockSpec((1, LANES), lambda b: (0, 0)),
                  full(w1), full(wk2d), full(wv2d)],
        out_specs=[ospec, ospec],
        out_shape=[out, out],
        compiler_params=_cparams(("arbitrary",)),
        name="cmp_kv",
    )(*([kcv] * G), pe2, posc, jnp.asarray(inv_a), w1, wk2d, wv2d)


def _build_cmp_w1(w_ck1, w_cv1):
    wk = w_ck1.reshape(CMP_BLOCK, A_HEAD_DIM, CMP_HIDDEN)
    wv = w_cv1.reshape(CMP_BLOCK, A_HEAD_DIM, CMP_HIDDEN)
    z = jnp.zeros_like(wk)
    return jnp.concatenate([jnp.concatenate([wk, z], axis=2),
                            jnp.concatenate([z, wv], axis=2)], axis=1).astype(BF16)


def _gate_expand_table(branch):
    e = np.zeros((LANES, A_REP * LANES), np.float32)
    for r in range(A_REP):
        e[GATE_LANE + 3 * r + branch, r * LANES:(r + 1) * LANES] = 1.0
    return e


def _branch_gates(gate_blk, e_ref):
    sg = jax.nn.sigmoid(gate_blk.astype(F32)).astype(BF16)
    return jnp.dot(sg, e_ref[...], preferred_element_type=F32)


def _overlap_table_t(seq):
    n_c = (seq - CMP_BLOCK) // CMP_STRIDE + 1
    n_s = seq // SEL_BLOCK
    cs = np.arange(n_c) * CMP_STRIDE
    ss = np.arange(n_s) * SEL_BLOCK
    ov = np.clip(np.minimum(cs[:, None] + CMP_BLOCK, ss[None, :] + SEL_BLOCK)
                 - np.maximum(cs[:, None], ss[None, :]), 0, None).astype(np.float32) / CMP_BLOCK
    full = np.zeros((LANES, LANES), np.float32)
    full[AUG_LANE:AUG_LANE + n_s, :n_c] = ov.T
    return full


def _split3(x):
    hi = x.astype(BF16)
    r1 = x - hi.astype(F32)
    mid = r1.astype(BF16)
    lo = (r1 - mid.astype(F32)).astype(BF16)
    return hi, mid, lo


def _compressed_branch(q_ref, kc2_ref, vc2_ref, ovt_ref, gates, ocmp_ref, qi, tq, n_sel_blocks):
    t = qi * tq + lax.broadcasted_iota(jnp.int32, (tq, LANES), 0)
    lane = lax.broadcasted_iota(jnp.int32, (tq, LANES), 1)
    cmask = (CMP_STRIDE * lane + (CMP_BLOCK - 1)) <= t
    cmask_f = cmask.astype(F32)
    nt = (((1,), (1,)), ((), ()))
    psum = jnp.zeros((tq, LANES), F32)
    for pair in range(A_REP // 2):
        qp = q_ref[:, pair * LANES:(pair + 1) * LANES]
        s2 = lax.dot_general(qp, kc2_ref[...], nt, preferred_element_type=F32)
        probs = []
        for hh in range(2):
            s = jnp.where(cmask, s2[:, hh * LANES:(hh + 1) * LANES], NEG)
            mx = jnp.max(s, axis=-1, keepdims=True)
            e = jnp.exp2(s - mx) * cmask_f
            pr = e / jnp.maximum(jnp.sum(e, axis=-1, keepdims=True), 1.0)
            probs.append(pr)
            psum = psum + pr
        p2 = jnp.concatenate(probs, axis=1).astype(BF16)
        o_pair = jnp.dot(p2, vc2_ref[...], preferred_element_type=F32)
        gate = jnp.where(lane < A_HEAD_DIM, gates[:, (2 * pair) * LANES:(2 * pair + 1) * LANES],
                         gates[:, (2 * pair + 1) * LANES:(2 * pair + 2) * LANES])
        ocmp_ref[:, pair * LANES:(pair + 1) * LANES] = (gate * o_pair).astype(ocmp_ref.dtype)
        yield

    ovt = ovt_ref[...]
    imp_t = sum(lax.dot_general(ovt, part, nt, preferred_element_type=F32) for part in _split3(psum))
    imp_t = imp_t[AUG_LANE:AUG_LANE + n_sel_blocks, :]
    yield
    jb = lax.broadcasted_iota(jnp.int32, imp_t.shape, 0)
    tcol = qi * tq + lax.broadcasted_iota(jnp.int32, imp_t.shape, 1)
    cur = tcol // SEL_BLOCK
    valid = (SEL_BLOCK * jb) <= tcol
    forced = valid & ((jb == 0) | (jb == cur) | (jb == cur - 1))
    score = jnp.where(valid, imp_t + FORCE_BONUS * forced.astype(F32), NEG)
    rank = jnp.zeros(imp_t.shape, F32)
    for i in range(n_sel_blocks):
        other = score[i:i + 1, :]
        beats = (other > score) | ((other == score) & (jb > i))
        rank = rank + beats.astype(F32)
    selb = jnp.where(rank < float(min(SEL_TOPK, n_sel_blocks)), 0.0, NEG)
    yield
    aug_t = jnp.concatenate([jnp.zeros((AUG_LANE, tq), F32), selb,
                             jnp.zeros((LANES - AUG_LANE - n_sel_blocks, tq), F32)], axis=0)
    return aug_t.T.astype(BF16)


def _flash_init(m_scr, acc_scr, l_scr=None):
    m_scr[...] = jnp.full(m_scr.shape, M_INIT, F32)
    acc_scr[...] = jnp.zeros(acc_scr.shape, F32)
    if l_scr is not None:
        l_scr[...] = jnp.zeros(l_scr.shape, F32)


def _scores(q, k):
    return lax.dot_general(q, k, (((1,), (1,)), ((), ())), preferred_element_type=F32)


def _add_bias(s, bias, reps):
    tq, tk = bias.shape
    return (s.reshape(reps, tq, tk) + bias[None]).reshape(reps * tq, tk)


def _flash_update(s, v, m_scr, acc_scr, l_scr=None, first=False):
    chunks = [s[:, c * LANES:(c + 1) * LANES] for c in range(s.shape[1] // LANES)]
    mx = functools.reduce(jnp.maximum, chunks)
    if first:
        assert l_scr is None
        m_new = jnp.broadcast_to(jnp.max(mx, axis=-1, keepdims=True), mx.shape)
        p = jnp.concatenate([jnp.exp2((ch - m_new).astype(v.dtype)) for ch in chunks], axis=1)
        acc_scr[...] = jnp.dot(p, v, preferred_element_type=F32)
        m_scr[...] = m_new
        return
    m_prev = m_scr[...]
    m_new = jnp.maximum(m_prev, jnp.max(mx, axis=-1, keepdims=True))
    alpha = jnp.exp2(m_prev - m_new)
    if l_scr is None:
        p = jnp.concatenate([jnp.exp2((ch - m_new).astype(v.dtype)) for ch in chunks], axis=1)
    else:
        ps = [jnp.exp2(ch - m_new) for ch in chunks]
        p = jnp.concatenate(ps, axis=1).astype(v.dtype)
        l_scr[...] = alpha * l_scr[...] + functools.reduce(jnp.add, ps)
    acc_scr[...] = alpha * acc_scr[...] + jnp.dot(p, v, preferred_element_type=F32)
    m_scr[...] = m_new


def _gated_head(acc, gates, r):
    lane = lax.broadcasted_iota(jnp.int32, acc.shape, 1)
    scale = gates[:, r * LANES:(r + 1) * LANES] / acc[:, ONES_LANE:ONES_LANE + 1]
    return jnp.where(lane < A_HEAD_DIM, acc * scale, 0.0)


def _store_head_pair(o_ref, pair, even, odd):
    o_ref[:, pair * LANES:(pair + 1) * LANES] = (even + pltpu.roll(odd, A_HEAD_DIM, 1)).astype(o_ref.dtype)


def _win_bias_table(t, span, window):
    delta = np.arange(t)[:, None] - np.arange(span)[None, :]
    dist = np.stack([n * t + delta for n in range(span // t)])
    return np.where((dist >= 0) & (dist < window), 0.0, NEG).astype(np.float32)


def _interleave(order, *gens):
    results = [None] * len(gens)
    done = [False] * len(gens)

    def step(n):
        if not done[n]:
            try:
                next(gens[n])
            except StopIteration as stop:
                results[n], done[n] = stop.value, True
    for n in order:
        step(n)
    for n in range(len(gens)):
        while not done[n]:
            step(n)
    return results


def _window_branch(stacked_q, kw_ref, vw_ref, bias_ref, gates, owin_ref, m_scr, acc_scr, s0_scr, s1_scr,
                   qi, t, span):
    n_tiles = span // t
    start = pl.multiple_of(jnp.maximum((qi + 1) * t - span, 0), t)
    variant = jnp.minimum(qi, n_tiles - 1)
    rows_of = lambda ref, w: ref[pl.ds(pl.multiple_of(start + w * t, t), t), :]
    bufs = (s0_scr, s1_scr)

    def score_into(w):
        bufs[w % 2][...] = _scores(stacked_q(), rows_of(kw_ref, w))

    def fold(w):
        s = _add_bias(bufs[w % 2][...], bias_ref[variant, :, w * t:(w + 1) * t], A_REP)
        _flash_update(s, rows_of(vw_ref, w), m_scr, acc_scr, first=(w == 0))

    score_into(0)
    yield
    for w in range(n_tiles):
        if w + 1 < n_tiles:
            score_into(w + 1)
        fold(w)
        yield
    heads = [_gated_head(acc_scr[r * t:(r + 1) * t, :], gates, r) for r in range(A_REP)]
    for pair in range(A_REP // 2):
        _store_head_pair(owin_ref, pair, heads[2 * pair], heads[2 * pair + 1])


def _nsa_kernel(q_ref, kc2_ref, vc2_ref, ovt_ref, ks_ref, vs_ref, kw_ref, vw_ref, ecmp_ref, esel_ref,
                ewin_ref, bias_ref, ocmp_ref, osel_ref, owin_ref, qaug_scr, m_scr, acc_scr, s0_scr, s1_scr,
                mw_scr, accw_scr, sw0_scr, sw1_scr, *, t, span, n_sel_blocks):
    qi = pl.program_id(2)
    rows_of = lambda ref, kj: ref[pl.ds(pl.multiple_of(kj * t, t), t), :]
    gate_blk = rows_of(ks_ref, qi)
    lo = lax.broadcasted_iota(jnp.int32, (t, LANES), 1) < A_HEAD_DIM
    for r in range(A_REP):
        qp = q_ref[:, (r // 2) * LANES:(r // 2 + 1) * LANES]
        if r % 2:
            qp = jnp.concatenate([qp[:, A_HEAD_DIM:], qp[:, :A_HEAD_DIM]], axis=1)
        qaug_scr[r] = jnp.where(lo, qp, jnp.zeros_like(qp))
    stacked_q = lambda: qaug_scr[...].reshape(A_REP * t, LANES)

    cmp = _compressed_branch(q_ref, kc2_ref, vc2_ref, ovt_ref, _branch_gates(gate_blk, ecmp_ref),
                             ocmp_ref, qi, t, n_sel_blocks)
    win = _window_branch(stacked_q, kw_ref, vw_ref, bias_ref, _branch_gates(gate_blk, ewin_ref), owin_ref,
                         mw_scr, accw_scr, sw0_scr, sw1_scr, qi, t, span)
    aug, _ = _interleave(NSA_STAGE_ORDER, cmp, win)
    qaug_scr[...] = qaug_scr[...] + aug[None]
    _flash_init(m_scr, acc_scr)

    def score_into(buf, kj):
        buf[...] = _scores(stacked_q(), rows_of(ks_ref, kj))

    def fold(buf, kj, causal=False):
        s = buf[...]
        if causal:
            delta = (lax.broadcasted_iota(jnp.int32, (t, t), 0)
                     - lax.broadcasted_iota(jnp.int32, (t, t), 1))
            s = _add_bias(s, jnp.where(delta >= 0, 0.0, NEG), A_REP)
        _flash_update(s, rows_of(vs_ref, kj), m_scr, acc_scr)

    first = qi % 2
    score_into(s1_scr, 0)
    score_into(s0_scr, first)

    @pl.when(first == 1)
    def _():
        fold(s1_scr, 0)

    def body(i, carry):
        kj = first + 2 * i
        score_into(s1_scr, kj + 1)
        fold(s0_scr, kj)
        score_into(s0_scr, kj + 2)
        fold(s1_scr, kj + 1)
        return carry
    lax.fori_loop(0, (qi - first) // 2, body, 0)

    fold(s0_scr, qi, causal=True)
    gates = _branch_gates(gate_blk, esel_ref)
    sel_heads = [_gated_head(acc_scr[r * t:(r + 1) * t, :], gates, r) for r in range(A_REP)]
    for pair in range(A_REP // 2):
        _store_head_pair(osel_ref, pair, sel_heads[2 * pair], sel_heads[2 * pair + 1])


def _nsa_attend(slab, kc2, vc2, B, seq):
    t = T_ATT
    nq = seq // t
    G = A_KV_GROUPS
    rows = A_REP * t
    span = min(((WIN_SIZE - 1 + t - 1) // t + 1) * t, seq)
    ovt = jnp.asarray(_overlap_table_t(seq), BF16)
    ecmp, esel, ewin = (jnp.asarray(_gate_expand_table(br), BF16) for br in range(3))
    bias = jnp.asarray(_win_bias_table(t, span, WIN_SIZE))
    kv = lambda col: pl.BlockSpec((seq, LANES), lambda b, g, q: (b, col // LANES + g))
    const = lambda a: pl.BlockSpec(a.shape, lambda b, g, q: (0,) * a.ndim)
    ckv = pl.BlockSpec((None, None) + kc2.shape[2:], lambda b, g, q: (b, g, 0, 0))
    out = jax.ShapeDtypeStruct((B * seq, A_WIDTH), BF16)
    ospec = pl.BlockSpec((t, A_REP * A_HEAD_DIM), lambda b, g, q: (b * nq + q, g))
    return pl.pallas_call(
        functools.partial(_nsa_kernel, t=t, span=span, n_sel_blocks=seq // SEL_BLOCK),
        grid=(B, G, nq),
        in_specs=[pl.BlockSpec((t, A_REP * A_HEAD_DIM), lambda b, g, q: (b * nq + q, g)),
                  ckv, ckv, const(ovt), kv(COL_KS), kv(COL_VS), kv(COL_KW), kv(COL_VW),
                  const(ecmp), const(esel), const(ewin), const(bias)],
        out_specs=[ospec, ospec, ospec],
        out_shape=[out, out, out],
        scratch_shapes=[pltpu.VMEM((A_REP, t, LANES), BF16)]
        + [pltpu.VMEM((rows, LANES), F32), pltpu.VMEM((rows, LANES), F32),
           pltpu.VMEM((rows, t), F32), pltpu.VMEM((rows, t), F32)] * 2,
        compiler_params=_cparams(("arbitrary", "arbitrary", "arbitrary")),
        name="nsa",
    )(slab, kc2, vc2, ovt, slab, slab, slab, slab, ecmp, esel, ewin, bias)


DILATIONS = ((128, 1), (512, 4), (2048, 16))
TQ_DIL = 512
TK_DIL = 256


def _dil_bias_tables(tq, tk, seq):
    w_near = max(w for w, r in DILATIONS if w < seq)
    near_keys = tq + -(-w_near // tk) * tk
    w_far, r_far = DILATIONS[-1]
    assert w_far >= seq and tq % r_far == 0 and tk % r_far == 0 and near_keys == 2 * tq
    assert tq % (2 * tk) == 0
    delta = np.arange(tq)[:, None] - np.arange(near_keys)[None, :]
    near = []
    for base in (near_keys - tq, 0):
        dist = base + delta
        mult = sum(((dist >= 0) & (dist <= w) & (dist % r == 0)).astype(np.float64) for w, r in DILATIONS)
        near.append(np.where(mult > 0, np.log2(np.maximum(mult, 1.0)), NEG))
    far = np.where(delta[:, :tk] % r_far == 0, 0.0, NEG)
    return np.stack(near).astype(np.float32), far.astype(np.float32)


def _dil_kernel(q_ref, k_ref, v_ref, near_ref, far_ref, o_ref, m_scr, acc_scr, l_scr, s0_scr, s1_scr,
                *, tq, tk):
    qi = pl.program_id(2)
    q = q_ref[...]
    near_keys = near_ref.shape[-1]
    _flash_init(m_scr, acc_scr, l_scr)
    update = functools.partial(_flash_update, m_scr=m_scr, acc_scr=acc_scr, l_scr=l_scr)
    rows_of = lambda ref, kj: ref[pl.ds(pl.multiple_of(kj * tk, tk), tk), :]
    variant = ((qi + 1) * tq < near_keys).astype(jnp.int32)
    n_far = jnp.maximum((qi + 1) * tq - near_keys, 0) // tk
    bufs = (s0_scr, s1_scr)

    def score_into(buf, kj):
        buf[...] = _scores(q, rows_of(k_ref, kj))

    def fold(buf, kj):
        update(buf[...] + far_ref[...], rows_of(v_ref, kj))

    score_into(s0_scr, 0)

    def body(i, carry):
        kj = 2 * i
        score_into(s1_scr, kj + 1)
        fold(s0_scr, kj)
        score_into(s0_scr, kj + 2)
        fold(s1_scr, kj + 1)
        return carry
    lax.fori_loop(0, n_far // 2, body, 0)

    n_near = near_keys // tk
    for w in range(n_near):
        if w + 1 < n_near:
            score_into(bufs[(w + 1) % 2], n_far + w + 1)
        update(bufs[w % 2][...] + near_ref[variant, :, w * tk:(w + 1) * tk], rows_of(v_ref, n_far + w))
    l = jnp.sum(l_scr[...], axis=-1, keepdims=True)
    o_ref[...] = (acc_scr[...] / l).astype(o_ref.dtype)


def _dil(slab, B, seq):
    tq, tk = TQ_DIL, TK_DIL
    nq = seq // tq
    H = B_HEADS
    near, far = (jnp.asarray(a) for a in _dil_bias_tables(tq, tk, seq))
    return pl.pallas_call(
        functools.partial(_dil_kernel, tq=tq, tk=tk),
        grid=(B, H, nq),
        in_specs=[pl.BlockSpec((tq, LANES), lambda b, h, q: (b * nq + q, COL_QB // LANES + h)),
                  pl.BlockSpec((seq, LANES), lambda b, h, q: (b, COL_KB // LANES + h)),
                  pl.BlockSpec((seq, LANES), lambda b, h, q: (b, COL_VB // LANES + h)),
                  pl.BlockSpec(near.shape, lambda b, h, q: (0, 0, 0)),
                  pl.BlockSpec(far.shape, lambda b, h, q: (0, 0))],
        out_specs=pl.BlockSpec((tq, LANES), lambda b, h, q: (b * nq + q, h)),
        out_shape=jax.ShapeDtypeStruct((B * seq, B_WIDTH), BF16),
        scratch_shapes=[pltpu.VMEM((tq, LANES), F32)] * 3 + [pltpu.VMEM((tq, tk), F32)] * 2,
        compiler_params=_cparams(("arbitrary", "arbitrary", "arbitrary")),
        name="dilated",
    )(slab, slab, slab, near, far)


def _silu(z):
    return z * jax.nn.sigmoid(z)


def _out_kernel(ocmp_ref, osel_ref, owin_ref, za_ref, ob_ref, zb_ref, ma_ref, mb_ref,
                x_ref, gate_ref, gpost_ref, wa_ref, wb_ref, wo_ref, o_ref):
    oa = ocmp_ref[...].astype(F32) + osel_ref[...].astype(F32) + owin_ref[...].astype(F32)
    a_in = (oa * _silu(za_ref[...].astype(F32))).astype(BF16)
    ya = jnp.dot(a_in, wa_ref[...], preferred_element_type=F32)
    b_in = (ob_ref[...].astype(F32) * _silu(zb_ref[...].astype(F32))).astype(BF16)
    yb = jnp.dot(b_in, wb_ref[...], preferred_element_type=F32)
    merged = (jax.nn.sigmoid(ma_ref[...].astype(F32)) * ya
              + jax.nn.sigmoid(mb_ref[...].astype(F32)) * yb)
    out = jnp.dot(merged.astype(BF16), wo_ref[...], preferred_element_type=F32)
    y = out * lax.rsqrt(jnp.mean(out * out, axis=-1, keepdims=True) + EPS) * gpost_ref[...]
    o_ref[...] = x_ref[...] + gate_ref[0] * y


def _out(ocmp, osel, owin, ob, slab, x2d, ada3, g_post, wa, wb, wo, seq):
    M, D = x2d.shape
    tm = 256
    per_b = seq // tm
    const = lambda a: pl.BlockSpec(a.shape, lambda i: (0,) * a.ndim, pipeline_mode=pl.Buffered(1))
    row = lambda w, cb: pl.BlockSpec((tm, w), lambda i: (i, cb))
    return pl.pallas_call(
        _out_kernel,
        grid=(M // tm,),
        in_specs=[row(A_WIDTH, 0), row(A_WIDTH, 0), row(A_WIDTH, 0),
                  row(A_WIDTH, COL_ZA // A_WIDTH), row(B_WIDTH, 0), row(B_WIDTH, COL_ZB // B_WIDTH),
                  row(D_MODEL, COL_MA // D_MODEL), row(D_MODEL, COL_MB // D_MODEL),
                  row(D, 0),
                  pl.BlockSpec((1, 1, D), lambda i: (i // per_b, 0, 2)),
                  pl.BlockSpec((1, D), lambda i: (0, 0)),
                  const(wa), const(wb), const(wo)],
        out_specs=pl.BlockSpec((tm, D), lambda i: (i, 0)),
        out_shape=jax.ShapeDtypeStruct((M, D), F32),
        compiler_params=_cparams(("arbitrary",)),
        name="out",
    )(ocmp, osel, owin, slab, ob, slab, slab, slab, x2d, ada3, g_post.reshape(1, D), wa, wb, wo)


def _layer(x, c, positions, w_ada, b_ada, g_pre, g_post, w_in, pe_ck, pe_cv, w_ck1, w_ck2,
           w_cv1, w_cv2, w_br_a, w_br_b, w_out):
    B, S, D = x.shape
    x2d = x.reshape(B * S, D)
    ada3 = _ada(c, w_ada, b_ada).reshape(B, 1, 3 * D)
    slab, kcv = _proj(x2d, ada3, g_pre, positions.reshape(B * S, 1), _build_w_slab_t(jnp.swapaxes(w_in, 0, 1)), S)

    nblk = S // CMP_STRIDE
    assert (S - CMP_BLOCK) // CMP_STRIDE + 1 == nblk - 1
    cmp_end = np.minimum(np.arange(nblk) * CMP_STRIDE + CMP_BLOCK - 1, S - 1)
    posc = positions[:, cmp_end].reshape(B, nblk, 1)
    dup = lambda w: jnp.concatenate([w, w], axis=1).astype(BF16)
    pe2 = jnp.concatenate([pe_ck, pe_cv], axis=1)
    kc2, vc2 = _cmp_kv(kcv, pe2, posc, _build_cmp_w1(w_ck1, w_cv1), dup(w_ck2), dup(w_cv2), B, S)

    ocmp, osel, owin = _nsa_attend(slab, kc2, vc2, B, S)
    ob = _dil(slab, B, S)
    out = _out(ocmp, osel, owin, ob, slab, x2d, ada3, g_post,
               w_br_a.astype(BF16), w_br_b.astype(BF16), w_out.astype(BF16), S)
    return out.reshape(B, S, D)


def kernel(x, c, positions, w_ada, b_ada, g_pre, g_post, w_in, pe_ck, pe_cv, w_ck1, w_ck2, w_cv1, w_cv2, w_br_a, w_br_b, w_out):
    h = x
    for layer in range(w_ada.shape[0]):
        h = _layer(h, c, positions, w_ada[layer], b_ada[layer], g_pre[layer], g_post[layer],
                   w_in[layer], pe_ck[layer], pe_cv[layer], w_ck1[layer], w_ck2[layer],
                   w_cv1[layer], w_cv2[layer], w_br_a[layer], w_br_b[layer], w_out[layer])
    return h
```

```python
import functools

import numpy as np
import jax
import jax.numpy as jnp
from jax import lax
from jax.experimental import pallas as pl
from jax.experimental.pallas import tpu as pltpu

F32 = jnp.float32
BF16 = jnp.bfloat16

D_MODEL = 2048
A_HEADS = 16
A_HEAD_DIM = 64
A_KV_GROUPS = 4
A_REP = A_HEADS // A_KV_GROUPS
A_WIDTH = A_HEADS * A_HEAD_DIM
A_KV_WIDTH = A_KV_GROUPS * A_HEAD_DIM
CMP_BLOCK = 32
CMP_STRIDE = 16
CMP_HIDDEN = 4 * A_HEAD_DIM
SEL_BLOCK = 64
SEL_TOPK = 16
WIN_SIZE = 512
FORCE_BONUS = 1.0e4
B_HEADS = 8
B_HEAD_DIM = 128
B_WIDTH = B_HEADS * B_HEAD_DIM
ROPE_THETA = 500000.0
EPS = 1e-6
NEG = -1e30
M_INIT = -1.0e38
LOG2E = 1.4426950408889634

LANES = 128
MXU_N = 256
VMEM_LIMIT = 56 * 1024 * 1024

TN = 1024
COL_QA = 0
COL_KS = 1024
COL_KW = 1536
COL_QB = 2048
COL_KB = 3072
COL_VB = 4096
COL_ZA = 5120
COL_ZB = 6144
COL_VS = 7168
COL_VW = 7680
COL_MA = 8192
COL_MB = 10240
N_SLAB = 12288
COL_KCV = N_SLAB
N_PROJ = N_SLAB + 2 * A_KV_WIDTH
GATE_LANE = 96
AUG_LANE = 64
ONES_LANE = 64

T_ATT = 256
NSA_STAGE_ORDER = (0, 0, 0, 0, 0)


def _cparams(sem):
    return pltpu.CompilerParams(dimension_semantics=sem, vmem_limit_bytes=VMEM_LIMIT)


def _ada_kernel(c_ref, w_ref, b_ref, o_ref):
    o_ref[...] = jnp.dot(c_ref[...], w_ref[...], preferred_element_type=F32,
                         precision=lax.Precision.HIGHEST) + b_ref[...]


def _ada(c, w_ada, b_ada):
    B, D = c.shape
    N = w_ada.shape[1]
    tn = 768
    return pl.pallas_call(
        _ada_kernel,
        grid=(N // tn,),
        in_specs=[pl.BlockSpec((B, D), lambda j: (0, 0)),
                  pl.BlockSpec((D, tn), lambda j: (0, j)),
                  pl.BlockSpec((1, tn), lambda j: (0, j))],
        out_specs=pl.BlockSpec((B, tn), lambda j: (0, j)),
        out_shape=jax.ShapeDtypeStruct((B, N), F32),
        compiler_params=_cparams(("arbitrary",)),
        name="ada",
    )(c, w_ada, b_ada.reshape(1, N))


def _rope_inv_tables():
    lane = np.arange(LANES)
    inv_a8 = (ROPE_THETA ** (-np.arange(0, 16, 2) / 16)).astype(np.float32)
    inv_b16 = (ROPE_THETA ** (-np.arange(0, 32, 2) / 32)).astype(np.float32)
    la = lane % A_HEAD_DIM
    inv_a = np.where(la < 16, inv_a8[la % 8], 0.0).astype(np.float32)
    inv_b = np.where(lane < 32, inv_b16[lane % 16], 0.0).astype(np.float32)
    return inv_a.reshape(1, LANES), inv_b.reshape(1, LANES)


def _rope_tables(pos_f, inv, period, half):
    ang = pos_f * inv
    c, s = jnp.cos(ang), jnp.sin(ang)
    lane = lax.broadcasted_iota(jnp.int32, ang.shape, 1) % period
    cos_t = jnp.where(lane < 2 * half, c, 1.0)
    sin_t = jnp.where(lane < half, -s, jnp.where(lane < 2 * half, s, 0.0))
    return cos_t, sin_t


A_TABLE_LANE0 = 32


def _rope_inv_combined():
    inv_a, inv_b = _rope_inv_tables()
    comb = inv_b.copy()
    comb[0, A_TABLE_LANE0:A_TABLE_LANE0 + 16] = inv_a[0, :16]
    return comb


def _rope_tables_ab(pos_f, inv_comb):
    ang = pos_f * inv_comb
    c, s = jnp.cos(ang), jnp.sin(ang)
    lane = lax.broadcasted_iota(jnp.int32, ang.shape, 1)
    cos_b = jnp.where(lane < 32, c, 1.0)
    sin_b = jnp.where(lane < 16, -s, jnp.where(lane < 32, s, 0.0))
    la = lane % A_HEAD_DIM
    src = jnp.where(la < 16, A_TABLE_LANE0 + la, lane)
    ca, sa = jnp.take_along_axis(c, src, axis=1), jnp.take_along_axis(s, src, axis=1)
    cos_a = jnp.where(la < 16, ca, 1.0)
    sin_a = jnp.where(la < 8, -sa, jnp.where(la < 16, sa, 0.0))
    return cos_a, sin_a, cos_b, sin_b


def _rope_apply(x, cos_t, sin_t, half):
    lane = lax.broadcasted_iota(jnp.int32, x.shape, 1)
    partner = jnp.take_along_axis(x, lane ^ half, axis=1)
    return x * cos_t + partner * sin_t


def _proj_kernel(x_ref, shift_ref, scale_ref, gpre_ref, pos_ref, inv_ref, w_ref,
                 o_ref, kcv_ref, h_scr, ca, sa, cb, sb, *, tm, seq):
    i = pl.program_id(0)
    j = pl.program_id(1)

    @pl.when(j == 0)
    def _():
        x = x_ref[...]
        ms = jnp.mean(x * x, axis=-1, keepdims=True)
        y = x * lax.rsqrt(ms + EPS) * gpre_ref[...]
        h = y * (1.0 + scale_ref[0]) + shift_ref[0]
        h_scr[...] = h.astype(BF16)
        pos_f = pos_ref[...].astype(F32)
        ca[...], sa[...], cb[...], sb[...] = _rope_tables_ab(pos_f, inv_ref[...])

    def tile(epilogues):
        for part, epilogue in enumerate(epilogues):
            c0 = part * MXU_N
            acc = lax.dot_general(h_scr[...], w_ref[c0:c0 + MXU_N, :], (((1,), (1,)), ((), ())),
                                  preferred_element_type=F32)
            epilogue(acc, c0)

    def rope_epilogue(tabs, half, mul, first_head_only, onehot):
        def epilogue(acc, c0):
            cos_t, sin_t = tabs[0][...], tabs[1][...]
            lane = lax.broadcasted_iota(jnp.int32, (tm, LANES), 1)
            if first_head_only:
                keep = lane < A_HEAD_DIM
                cos_t = jnp.where(keep, cos_t, 1.0)
                sin_t = jnp.where(keep, sin_t, 0.0)
            if onehot:
                t = (i * tm + lax.broadcasted_iota(jnp.int32, (tm, LANES), 0)) % seq
                hot = (lane >= AUG_LANE) & (lane < AUG_LANE + seq // SEL_BLOCK) & \
                      ((t // SEL_BLOCK) == (lane - AUG_LANE))
            for cidx in range(MXU_N // LANES):
                xc = acc[:, cidx * LANES:(cidx + 1) * LANES]
                r = _rope_apply(xc, cos_t, sin_t, half)
                if mul != 1.0:
                    r = r * mul
                if onehot:
                    r = jnp.where(hot, 1.0, r)
                o_ref[:, c0 + cidx * LANES:c0 + (cidx + 1) * LANES] = r.astype(o_ref.dtype)
        return epilogue

    def plain_epilogue(acc, c0):
        o_ref[:, c0:c0 + MXU_N] = acc.astype(o_ref.dtype)

    def ones_lane_epilogue(acc, c0):
        lane = lax.broadcasted_iota(jnp.int32, acc.shape, 1) % LANES
        o_ref[:, c0:c0 + MXU_N] = jnp.where(lane == ONES_LANE, 1.0, acc).astype(o_ref.dtype)

    def kcv_epilogue(acc, c0):
        kcv_ref[:, c0:c0 + MXU_N] = acc

    ta = (ca, sa)
    tb = (cb, sb)
    parts = lambda col0, col1, epi: [((col0 // MXU_N + n), epi) for n in range((col1 - col0) // MXU_N)]
    kinds = dict(
        parts(COL_QA, COL_KS, rope_epilogue(ta, 8, A_HEAD_DIM ** -0.5 * LOG2E, False, False))
        + parts(COL_KS, COL_KW, rope_epilogue(ta, 8, 1.0, True, True))
        + parts(COL_KW, COL_QB, rope_epilogue(ta, 8, 1.0, True, False))
        + parts(COL_QB, COL_KB, rope_epilogue(tb, 16, B_HEAD_DIM ** -0.5 * LOG2E, False, False))
        + parts(COL_KB, COL_VB, rope_epilogue(tb, 16, 1.0, False, False))
        + parts(COL_VB, COL_VS, plain_epilogue)
        + parts(COL_VS, COL_MA, ones_lane_epilogue)
        + parts(COL_MA, COL_KCV, plain_epilogue)
        + parts(COL_KCV, N_PROJ, None))
    per_tile = TN // MXU_N
    steps = {}
    for jt in range(pl.cdiv(N_PROJ, TN)):
        tile_kinds = tuple(kinds[p] for p in range(jt * per_tile, (jt + 1) * per_tile) if p in kinds)
        if tile_kinds[0] is None:
            assert all(k is None for k in tile_kinds) and len(tile_kinds) * MXU_N == kcv_ref.shape[1]
            tile_kinds = (kcv_epilogue,) * len(tile_kinds)
        steps.setdefault(tile_kinds, []).append(jt)
    for tile_kinds, jts in steps.items():
        pl.when(functools.reduce(jnp.logical_or, [j == jt for jt in jts]))(
            functools.partial(tile, tile_kinds))


def _proj(x2d, ada3, g_pre, pos2d, w_slab_t, seq):
    M, D = x2d.shape
    tm = 1024
    per_b = seq // tm
    tab = pltpu.VMEM((tm, LANES), F32)
    last_slab_tile = N_SLAB // TN - 1
    n_kcv = N_PROJ - N_SLAB
    return pl.pallas_call(
        functools.partial(_proj_kernel, tm=tm, seq=seq),
        grid=(M // tm, pl.cdiv(N_PROJ, TN)),
        in_specs=[pl.BlockSpec((tm, D), lambda i, j: (i, 0)),
                  pl.BlockSpec((1, 1, D), lambda i, j: (i // per_b, 0, 0)),
                  pl.BlockSpec((1, 1, D), lambda i, j: (i // per_b, 0, 1)),
                  pl.BlockSpec((1, D), lambda i, j: (0, 0)),
                  pl.BlockSpec((tm, 1), lambda i, j: (i, 0)),
                  pl.BlockSpec((1, LANES), lambda i, j: (0, 0)),
                  pl.BlockSpec((TN, D), lambda i, j: (j, 0))],
        out_specs=[pl.BlockSpec((tm, TN), lambda i, j: (i, jnp.minimum(j, last_slab_tile))),
                   pl.BlockSpec((tm, n_kcv), lambda i, j: (i, 0))],
        out_shape=[jax.ShapeDtypeStruct((M, N_SLAB), BF16),
                   jax.ShapeDtypeStruct((M, n_kcv), F32)],
        scratch_shapes=[pltpu.VMEM((tm, D), BF16), tab, tab, tab, tab],
        compiler_params=_cparams(("arbitrary", "arbitrary")),
        name="proj",
    )(x2d, ada3, ada3, g_pre.reshape(1, D), pos2d, jnp.asarray(_rope_inv_combined()), w_slab_t)


_SRC_SIZES = (A_WIDTH,) + (A_KV_WIDTH,) * 6 + (A_HEADS * 3, A_WIDTH) + (B_WIDTH,) * 4 + (D_MODEL, D_MODEL)
_SRC = [int(v) for v in np.concatenate([[0], np.cumsum(_SRC_SIZES)])]
(SRC_QA, SRC_KC, SRC_VC, SRC_KS, SRC_VS, SRC_KW, SRC_VW, SRC_GA, SRC_ZA, SRC_QB, SRC_KB, SRC_VB,
 SRC_ZB, SRC_MA, SRC_MB, SRC_END) = _SRC


def _w_slab_kernel(w_ref, o_ref):
    cols = w_ref.shape[1]

    def put(dst, src, width):
        o_ref[dst:dst + width, :] = w_ref[src:src + width, :].astype(o_ref.dtype)

    def clear(dst, width):
        o_ref[dst:dst + width, :] = jnp.zeros((width, cols), o_ref.dtype)

    put(COL_QA, SRC_QA, A_WIDTH)
    put(COL_QB, SRC_QB, B_WIDTH)
    put(COL_KB, SRC_KB, B_WIDTH)
    put(COL_VB, SRC_VB, B_WIDTH)
    put(COL_ZA, SRC_ZA, A_WIDTH)
    put(COL_ZB, SRC_ZB, B_WIDTH)
    put(COL_MA, SRC_MA, D_MODEL)
    put(COL_MB, SRC_MB, D_MODEL)
    ngate = A_REP * 3
    for g in range(A_KV_GROUPS):
        d = g * LANES
        s = g * A_HEAD_DIM
        put(COL_KS + d, SRC_KS + s, A_HEAD_DIM)
        clear(COL_KS + d + A_HEAD_DIM, GATE_LANE - A_HEAD_DIM)
        put(COL_KS + d + GATE_LANE, SRC_GA + g * ngate, ngate)
        clear(COL_KS + d + GATE_LANE + ngate, LANES - GATE_LANE - ngate)
        for col, src in ((COL_KW, SRC_KW), (COL_VS, SRC_VS), (COL_VW, SRC_VW)):
            put(col + d, src + s, A_HEAD_DIM)
            clear(col + d + A_HEAD_DIM, LANES - A_HEAD_DIM)
        put(COL_KCV + d, SRC_KC + s, A_HEAD_DIM)
        put(COL_KCV + d + A_HEAD_DIM, SRC_VC + s, A_HEAD_DIM)
    clear(N_PROJ, o_ref.shape[0] - N_PROJ)


def _build_w_slab_t(w_in_t):
    n_in, D = w_in_t.shape
    assert n_in == SRC_END
    cols = 256
    n_rows = pl.cdiv(N_PROJ, TN) * TN
    return pl.pallas_call(
        _w_slab_kernel,
        grid=(D // cols,),
        in_specs=[pl.BlockSpec((n_in, cols), lambda i: (0, i))],
        out_specs=pl.BlockSpec((n_rows, cols), lambda i: (0, i)),
        out_shape=jax.ShapeDtypeStruct((n_rows, D), BF16),
        compiler_params=_cparams(("arbitrary",)),
        name="w_slab",
    )(w_in_t)


def _cmp_kv_kernel(*refs):
    groups = A_KV_GROUPS
    t_refs = refs[:groups]
    pe_ref, posc_ref, inva_ref, w1_ref, wk2_ref, wv2_ref, kc2_ref, vc2_ref = refs[groups:]
    nblk = t_refs[0].shape[0] // CMP_STRIDE
    rows = groups * nblk
    hidden2 = w1_ref.shape[-1]
    p_lo = jnp.zeros((rows, hidden2), F32)
    p_hi = jnp.zeros((rows, hidden2), F32)
    def tokens(l):
        return jnp.concatenate([t_ref[pl.ds(l, nblk, stride=CMP_STRIDE), :] for t_ref in t_refs], axis=0)

    def pair(x0, x1, l):
        xs = jnp.concatenate([x0 + pe_ref[l:l + 1, :], x1 + pe_ref[l + 1:l + 2, :]], axis=1)
        return jnp.dot(xs.astype(BF16), w1_ref[l:l + 2].reshape(2 * LANES, hidden2),
                       preferred_element_type=F32)

    for l in range(0, CMP_STRIDE, 2):
        x0, x1 = tokens(l), tokens(l + 1)
        p_lo += pair(x0, x1, l)
        p_hi += pair(x0, x1, CMP_STRIDE + l)
    hid = p_lo + pltpu.roll(p_hi, rows - 1, 0)
    act = jax.nn.gelu(hid).astype(BF16)
    kk = jnp.dot(act[:, :CMP_HIDDEN], wk2_ref[...], preferred_element_type=F32)
    vv = jnp.dot(act[:, CMP_HIDDEN:], wv2_ref[...], preferred_element_type=F32)
    pos_f = jnp.concatenate([posc_ref[...]] * groups, axis=0).astype(F32)
    cos_t, sin_t = _rope_tables(pos_f, inva_ref[...], A_HEAD_DIM, 8)
    kk = _rope_apply(kk, cos_t, sin_t, 8)
    lane = lax.broadcasted_iota(jnp.int32, (nblk, LANES), 1)
    lo = lane < A_HEAD_DIM
    for g in range(groups):
        kg = kk[g * nblk:(g + 1) * nblk]
        vg = vv[g * nblk:(g + 1) * nblk]
        kc2_ref[g, 0:nblk, :] = jnp.where(lo, kg, 0.0).astype(kc2_ref.dtype)
        kc2_ref[g, nblk:2 * nblk, :] = jnp.where(lo, 0.0, kg).astype(kc2_ref.dtype)
        vc2_ref[g, 0:nblk, :] = jnp.where(lo, vg, 0.0).astype(vc2_ref.dtype)
        vc2_ref[g, nblk:2 * nblk, :] = jnp.where(lo, 0.0, vg).astype(vc2_ref.dtype)


def _cmp_kv(kcv, pe2, posc, w1, wk2d, wv2d, B, seq):
    G = A_KV_GROUPS
    nblk = seq // CMP_STRIDE
    inv_a, _ = _rope_inv_tables()
    full = lambda a: pl.BlockSpec(a.shape, lambda b: (0,) * a.ndim)
    out = jax.ShapeDtypeStruct((B, G, 2 * nblk, LANES), BF16)
    ospec = pl.BlockSpec((None, G, 2 * nblk, LANES), lambda b: (b, 0, 0, 0))
    return pl.pallas_call(
        _cmp_kv_kernel,
        grid=(B,),
        in_specs=[pl.BlockSpec((seq, LANES), lambda b, g=g: (b, g)) for g in range(G)] + [
                  full(pe2),
                  pl.BlockSpec((None, nblk, 1), lambda b: (b, 0, 0)),
                  pl.BlockSpec((1, LANES), lambda b: (0, 0)),
                  full(w1), full(wk2d), full(wv2d)],
        out_specs=[ospec, ospec],
        out_shape=[out, out],
        compiler_params=_cparams(("arbitrary",)),
        name="cmp_kv",
    )(*([kcv] * G), pe2, posc, jnp.asarray(inv_a), w1, wk2d, wv2d)


def _build_cmp_w1(w_ck1, w_cv1):
    wk = w_ck1.reshape(CMP_BLOCK, A_HEAD_DIM, CMP_HIDDEN)
    wv = w_cv1.reshape(CMP_BLOCK, A_HEAD_DIM, CMP_HIDDEN)
    z = jnp.zeros_like(wk)
    return jnp.concatenate([jnp.concatenate([wk, z], axis=2),
                            jnp.concatenate([z, wv], axis=2)], axis=1).astype(BF16)


def _gate_expand_table(branch):
    e = np.zeros((LANES, A_REP * LANES), np.float32)
    for r in range(A_REP):
        e[GATE_LANE + 3 * r + branch, r * LANES:(r + 1) * LANES] = 1.0
    return e


def _branch_gates(gate_blk, e_ref):
    sg = jax.nn.sigmoid(gate_blk.astype(F32)).astype(BF16)
    return jnp.dot(sg, e_ref[...], preferred_element_type=F32)


def _overlap_table_t(seq):
    n_c = (seq - CMP_BLOCK) // CMP_STRIDE + 1
    n_s = seq // SEL_BLOCK
    cs = np.arange(n_c) * CMP_STRIDE
    ss = np.arange(n_s) * SEL_BLOCK
    ov = np.clip(np.minimum(cs[:, None] + CMP_BLOCK, ss[None, :] + SEL_BLOCK)
                 - np.maximum(cs[:, None], ss[None, :]), 0, None).astype(np.float32) / CMP_BLOCK
    full = np.zeros((LANES, LANES), np.float32)
    full[AUG_LANE:AUG_LANE + n_s, :n_c] = ov.T
    return full


def _split3(x):
    hi = x.astype(BF16)
    r1 = x - hi.astype(F32)
    mid = r1.astype(BF16)
    lo = (r1 - mid.astype(F32)).astype(BF16)
    return hi, mid, lo


def _compressed_branch(q_ref, kc2_ref, vc2_ref, ovt_ref, gates, ocmp_ref, qi, tq, n_sel_blocks):
    t = qi * tq + lax.broadcasted_iota(jnp.int32, (tq, LANES), 0)
    lane = lax.broadcasted_iota(jnp.int32, (tq, LANES), 1)
    cmask = (CMP_STRIDE * lane + (CMP_BLOCK - 1)) <= t
    cmask_f = cmask.astype(F32)
    nt = (((1,), (1,)), ((), ()))
    psum = jnp.zeros((tq, LANES), F32)
    for pair in range(A_REP // 2):
        qp = q_ref[:, pair * LANES:(pair + 1) * LANES]
        s2 = lax.dot_general(qp, kc2_ref[...], nt, preferred_element_type=F32)
        probs = []
        for hh in range(2):
            s = jnp.where(cmask, s2[:, hh * LANES:(hh + 1) * LANES], NEG)
            mx = jnp.max(s, axis=-1, keepdims=True)
            e = jnp.exp2(s - mx) * cmask_f
            pr = e / jnp.maximum(jnp.sum(e, axis=-1, keepdims=True), 1.0)
            probs.append(pr)
            psum = psum + pr
        p2 = jnp.concatenate(probs, axis=1).astype(BF16)
        o_pair = jnp.dot(p2, vc2_ref[...], preferred_element_type=F32)
        gate = jnp.where(lane < A_HEAD_DIM, gates[:, (2 * pair) * LANES:(2 * pair + 1) * LANES],
                         gates[:, (2 * pair + 1) * LANES:(2 * pair + 2) * LANES])
        ocmp_ref[:, pair * LANES:(pair + 1) * LANES] = (gate * o_pair).astype(ocmp_ref.dtype)
        yield

    ovt = ovt_ref[...]
    imp_t = sum(lax.dot_general(ovt, part, nt, preferred_element_type=F32) for part in _split3(psum))
    imp_t = imp_t[AUG_LANE:AUG_LANE + n_sel_blocks, :]
    yield
    jb = lax.broadcasted_iota(jnp.int32, imp_t.shape, 0)
    tcol = qi * tq + lax.broadcasted_iota(jnp.int32, imp_t.shape, 1)
    cur = tcol // SEL_BLOCK
    valid = (SEL_BLOCK * jb) <= tcol
    forced = valid & ((jb == 0) | (jb == cur) | (jb == cur - 1))
    score = jnp.where(valid, imp_t + FORCE_BONUS * forced.astype(F32), NEG)
    rank = jnp.zeros(imp_t.shape, F32)
    for i in range(n_sel_blocks):
        other = score[i:i + 1, :]
        beats = (other > score) | ((other == score) & (jb > i))
        rank = rank + beats.astype(F32)
    selb = jnp.where(rank < float(min(SEL_TOPK, n_sel_blocks)), 0.0, NEG)
    yield
    aug_t = jnp.concatenate([jnp.zeros((AUG_LANE, tq), F32), selb,
                             jnp.zeros((LANES - AUG_LANE - n_sel_blocks, tq), F32)], axis=0)
    return aug_t.T.astype(BF16)


def _flash_init(m_scr, acc_scr, l_scr=None):
    m_scr[...] = jnp.full(m_scr.shape, M_INIT, F32)
    acc_scr[...] = jnp.zeros(acc_scr.shape, F32)
    if l_scr is not None:
        l_scr[...] = jnp.zeros(l_scr.shape, F32)


def _scores(q, k):
    return lax.dot_general(q, k, (((1,), (1,)), ((), ())), preferred_element_type=F32)


def _add_bias(s, bias, reps):
    tq, tk = bias.shape
    return (s.reshape(reps, tq, tk) + bias[None]).reshape(reps * tq, tk)


def _flash_update(s, v, m_scr, acc_scr, l_scr=None, first=False):
    chunks = [s[:, c * LANES:(c + 1) * LANES] for c in range(s.shape[1] // LANES)]
    mx = functools.reduce(jnp.maximum, chunks)
    if first:
        assert l_scr is None
        m_new = jnp.broadcast_to(jnp.max(mx, axis=-1, keepdims=True), mx.shape)
        p = jnp.concatenate([jnp.exp2((ch - m_new).astype(v.dtype)) for ch in chunks], axis=1)
        acc_scr[...] = jnp.dot(p, v, preferred_element_type=F32)
        m_scr[...] = m_new
        return
    m_prev = m_scr[...]
    m_new = jnp.maximum(m_prev, jnp.max(mx, axis=-1, keepdims=True))
    alpha = jnp.exp2(m_prev - m_new)
    if l_scr is None:
        p = jnp.concatenate([jnp.exp2((ch - m_new).astype(v.dtype)) for ch in chunks], axis=1)
    else:
        ps = [jnp.exp2(ch - m_new) for ch in chunks]
        p = jnp.concatenate(ps, axis=1).astype(v.dtype)
        l_scr[...] = alpha * l_scr[...] + functools.reduce(jnp.add, ps)
    acc_scr[...] = alpha * acc_scr[...] + jnp.dot(p, v, preferred_element_type=F32)
    m_scr[...] = m_new


def _gated_head(acc, gates, r):
    lane = lax.broadcasted_iota(jnp.int32, acc.shape, 1)
    scale = gates[:, r * LANES:(r + 1) * LANES] / acc[:, ONES_LANE:ONES_LANE + 1]
    return jnp.where(lane < A_HEAD_DIM, acc * scale, 0.0)


def _store_head_pair(o_ref, pair, even, odd):
    o_ref[:, pair * LANES:(pair + 1) * LANES] = (even + pltpu.roll(odd, A_HEAD_DIM, 1)).astype(o_ref.dtype)


def _win_bias_table(t, span, window):
    delta = np.arange(t)[:, None] - np.arange(span)[None, :]
    dist = np.stack([n * t + delta for n in range(span // t)])
    return np.where((dist >= 0) & (dist < window), 0.0, NEG).astype(np.float32)


def _interleave(order, *gens):
    results = [None] * len(gens)
    done = [False] * len(gens)

    def step(n):
        if not done[n]:
            try:
                next(gens[n])
            except StopIteration as stop:
                results[n], done[n] = stop.value, True
    for n in order:
        step(n)
    for n in range(len(gens)):
        while not done[n]:
            step(n)
    return results


def _window_branch(stacked_q, kw_ref, vw_ref, bias_ref, gates, owin_ref, m_scr, acc_scr, s0_scr, s1_scr,
                   qi, t, span):
    n_tiles = span // t
    start = pl.multiple_of(jnp.maximum((qi + 1) * t - span, 0), t)
    variant = jnp.minimum(qi, n_tiles - 1)
    rows_of = lambda ref, w: ref[pl.ds(pl.multiple_of(start + w * t, t), t), :]
    bufs = (s0_scr, s1_scr)

    def score_into(w):
        bufs[w % 2][...] = _scores(stacked_q(), rows_of(kw_ref, w))

    def fold(w):
        s = _add_bias(bufs[w % 2][...], bias_ref[variant, :, w * t:(w + 1) * t], A_REP)
        _flash_update(s, rows_of(vw_ref, w), m_scr, acc_scr, first=(w == 0))

    score_into(0)
    yield
    for w in range(n_tiles):
        if w + 1 < n_tiles:
            score_into(w + 1)
        fold(w)
        yield
    heads = [_gated_head(acc_scr[r * t:(r + 1) * t, :], gates, r) for r in range(A_REP)]
    for pair in range(A_REP // 2):
        _store_head_pair(owin_ref, pair, heads[2 * pair], heads[2 * pair + 1])


def _nsa_kernel(q_ref, kc2_ref, vc2_ref, ovt_ref, ks_ref, vs_ref, kw_ref, vw_ref, ecmp_ref, esel_ref,
                ewin_ref, bias_ref, ocmp_ref, osel_ref, owin_ref, qaug_scr, m_scr, acc_scr, s0_scr, s1_scr,
                mw_scr, accw_scr, sw0_scr, sw1_scr, *, t, span, n_sel_blocks):
    qi = pl.program_id(2)
    rows_of = lambda ref, kj: ref[pl.ds(pl.multiple_of(kj * t, t), t), :]
    gate_blk = rows_of(ks_ref, qi)
    lo = lax.broadcasted_iota(jnp.int32, (t, LANES), 1) < A_HEAD_DIM
    for r in range(A_REP):
        qp = q_ref[:, (r // 2) * LANES:(r // 2 + 1) * LANES]
        if r % 2:
            qp = jnp.concatenate([qp[:, A_HEAD_DIM:], qp[:, :A_HEAD_DIM]], axis=1)
        qaug_scr[r] = jnp.where(lo, qp, jnp.zeros_like(qp))
    stacked_q = lambda: qaug_scr[...].reshape(A_REP * t, LANES)

    cmp = _compressed_branch(q_ref, kc2_ref, vc2_ref, ovt_ref, _branch_gates(gate_blk, ecmp_ref),
                             ocmp_ref, qi, t, n_sel_blocks)
    win = _window_branch(stacked_q, kw_ref, vw_ref, bias_ref, _branch_gates(gate_blk, ewin_ref), owin_ref,
                         mw_scr, accw_scr, sw0_scr, sw1_scr, qi, t, span)
    aug, _ = _interleave(NSA_STAGE_ORDER, cmp, win)
    qaug_scr[...] = qaug_scr[...] + aug[None]
    _flash_init(m_scr, acc_scr)

    def score_into(buf, kj):
        buf[...] = _scores(stacked_q(), rows_of(ks_ref, kj))

    def fold(buf, kj, causal=False):
        s = buf[...]
        if causal:
            delta = (lax.broadcasted_iota(jnp.int32, (t, t), 0)
                     - lax.broadcasted_iota(jnp.int32, (t, t), 1))
            s = _add_bias(s, jnp.where(delta >= 0, 0.0, NEG), A_REP)
        _flash_update(s, rows_of(vs_ref, kj), m_scr, acc_scr)

    first = qi % 2
    score_into(s1_scr, 0)
    score_into(s0_scr, first)

    @pl.when(first == 1)
    def _():
        fold(s1_scr, 0)

    def body(i, carry):
        kj = first + 2 * i
        score_into(s1_scr, kj + 1)
        fold(s0_scr, kj)
        score_into(s0_scr, kj + 2)
        fold(s1_scr, kj + 1)
        return carry
    lax.fori_loop(0, (qi - first) // 2, body, 0)

    fold(s0_scr, qi, causal=True)
    gates = _branch_gates(gate_blk, esel_ref)
    sel_heads = [_gated_head(acc_scr[r * t:(r + 1) * t, :], gates, r) for r in range(A_REP)]
    for pair in range(A_REP // 2):
        _store_head_pair(osel_ref, pair, sel_heads[2 * pair], sel_heads[2 * pair + 1])


def _nsa_attend(slab, kc2, vc2, B, seq):
    t = T_ATT
    nq = seq // t
    G = A_KV_GROUPS
    rows = A_REP * t
    span = min(((WIN_SIZE - 1 + t - 1) // t + 1) * t, seq)
    ovt = jnp.asarray(_overlap_table_t(seq), BF16)
    ecmp, esel, ewin = (jnp.asarray(_gate_expand_table(br), BF16) for br in range(3))
    bias = jnp.asarray(_win_bias_table(t, span, WIN_SIZE))
    kv = lambda col: pl.BlockSpec((seq, LANES), lambda b, g, q: (b, col // LANES + g))
    const = lambda a: pl.BlockSpec(a.shape, lambda b, g, q: (0,) * a.ndim)
    ckv = pl.BlockSpec((None, None) + kc2.shape[2:], lambda b, g, q: (b, g, 0, 0))
    out = jax.ShapeDtypeStruct((B * seq, A_WIDTH), BF16)
    ospec = pl.BlockSpec((t, A_REP * A_HEAD_DIM), lambda b, g, q: (b * nq + q, g))
    return pl.pallas_call(
        functools.partial(_nsa_kernel, t=t, span=span, n_sel_blocks=seq // SEL_BLOCK),
        grid=(B, G, nq),
        in_specs=[pl.BlockSpec((t, A_REP * A_HEAD_DIM), lambda b, g, q: (b * nq + q, g)),
                  ckv, ckv, const(ovt), kv(COL_KS), kv(COL_VS), kv(COL_KW), kv(COL_VW),
                  const(ecmp), const(esel), const(ewin), const(bias)],
        out_specs=[ospec, ospec, ospec],
        out_shape=[out, out, out],
        scratch_shapes=[pltpu.VMEM((A_REP, t, LANES), BF16)]
        + [pltpu.VMEM((rows, LANES), F32), pltpu.VMEM((rows, LANES), F32),
           pltpu.VMEM((rows, t), F32), pltpu.VMEM((rows, t), F32)] * 2,
        compiler_params=_cparams(("arbitrary", "arbitrary", "arbitrary")),
        name="nsa",
    )(slab, kc2, vc2, ovt, slab, slab, slab, slab, ecmp, esel, ewin, bias)


DILATIONS = ((128, 1), (512, 4), (2048, 16))
TQ_DIL = 512
TK_DIL = 256


def _dil_bias_tables(tq, tk, seq):
    w_near = max(w for w, r in DILATIONS if w < seq)
    near_keys = tq + -(-w_near // tk) * tk
    w_far, r_far = DILATIONS[-1]
    assert w_far >= seq and tq % r_far == 0 and tk % r_far == 0 and near_keys == 2 * tq
    assert tq % (2 * tk) == 0
    delta = np.arange(tq)[:, None] - np.arange(near_keys)[None, :]
    near = []
    for base in (near_keys - tq, 0):
        dist = base + delta
        mult = sum(((dist >= 0) & (dist <= w) & (dist % r == 0)).astype(np.float64) for w, r in DILATIONS)
        near.append(np.where(mult > 0, np.log2(np.maximum(mult, 1.0)), NEG))
    far = np.where(delta[:, :tk] % r_far == 0, 0.0, NEG)
    return np.stack(near).astype(np.float32), far.astype(np.float32)


def _dil_kernel(q_ref, k_ref, v_ref, near_ref, far_ref, o_ref, m_scr, acc_scr, l_scr, s0_scr, s1_scr,
                *, tq, tk):
    qi = pl.program_id(2)
    q = q_ref[...]
    near_keys = near_ref.shape[-1]
    _flash_init(m_scr, acc_scr, l_scr)
    update = functools.partial(_flash_update, m_scr=m_scr, acc_scr=acc_scr, l_scr=l_scr)
    rows_of = lambda ref, kj: ref[pl.ds(pl.multiple_of(kj * tk, tk), tk), :]
    variant = ((qi + 1) * tq < near_keys).astype(jnp.int32)
    n_far = jnp.maximum((qi + 1) * tq - near_keys, 0) // tk
    bufs = (s0_scr, s1_scr)

    def score_into(buf, kj):
        buf[...] = _scores(q, rows_of(k_ref, kj))

    def fold(buf, kj):
        update(buf[...] + far_ref[...], rows_of(v_ref, kj))

    score_into(s0_scr, 0)

    def body(i, carry):
        kj = 2 * i
        score_into(s1_scr, kj + 1)
        fold(s0_scr, kj)
        score_into(s0_scr, kj + 2)
        fold(s1_scr, kj + 1)
        return carry
    lax.fori_loop(0, n_far // 2, body, 0)

    n_near = near_keys // tk
    for w in range(n_near):
        if w + 1 < n_near:
            score_into(bufs[(w + 1) % 2], n_far + w + 1)
        update(bufs[w % 2][...] + near_ref[variant, :, w * tk:(w + 1) * tk], rows_of(v_ref, n_far + w))
    l = jnp.sum(l_scr[...], axis=-1, keepdims=True)
    o_ref[...] = (acc_scr[...] / l).astype(o_ref.dtype)


def _dil(slab, B, seq):
    tq, tk = TQ_DIL, TK_DIL
    nq = seq // tq
    H = B_HEADS
    near, far = (jnp.asarray(a) for a in _dil_bias_tables(tq, tk, seq))
    return pl.pallas_call(
        functools.partial(_dil_kernel, tq=tq, tk=tk),
        grid=(B, H, nq),
        in_specs=[pl.BlockSpec((tq, LANES), lambda b, h, q: (b * nq + q, COL_QB // LANES + h)),
                  pl.BlockSpec((seq, LANES), lambda b, h, q: (b, COL_KB // LANES + h)),
                  pl.BlockSpec((seq, LANES), lambda b, h, q: (b, COL_VB // LANES + h)),
                  pl.BlockSpec(near.shape, lambda b, h, q: (0, 0, 0)),
                  pl.BlockSpec(far.shape, lambda b, h, q: (0, 0))],
        out_specs=pl.BlockSpec((tq, LANES), lambda b, h, q: (b * nq + q, h)),
        out_shape=jax.ShapeDtypeStruct((B * seq, B_WIDTH), BF16),
        scratch_shapes=[pltpu.VMEM((tq, LANES), F32)] * 3 + [pltpu.VMEM((tq, tk), F32)] * 2,
        compiler_params=_cparams(("arbitrary", "arbitrary", "arbitrary")),
        name="dilated",
    )(slab, slab, slab, near, far)


def _silu(z):
    return z * jax.nn.sigmoid(z)


def _out_kernel(ocmp_ref, osel_ref, owin_ref, za_ref, ob_ref, zb_ref, ma_ref, mb_ref,
                x_ref, gate_ref, gpost_ref, wa_ref, wb_ref, wo_ref, o_ref):
    oa = ocmp_ref[...].astype(F32) + osel_ref[...].astype(F32) + owin_ref[...].astype(F32)
    a_in = (oa * _silu(za_ref[...].astype(F32))).astype(BF16)
    ya = jnp.dot(a_in, wa_ref[...], preferred_element_type=F32)
    b_in = (ob_ref[...].astype(F32) * _silu(zb_ref[...].astype(F32))).astype(BF16)
    yb = jnp.dot(b_in, wb_ref[...], preferred_element_type=F32)
    merged = (jax.nn.sigmoid(ma_ref[...].astype(F32)) * ya
              + jax.nn.sigmoid(mb_ref[...].astype(F32)) * yb)
    out = jnp.dot(merged.astype(BF16), wo_ref[...], preferred_element_type=F32)
    y = out * lax.rsqrt(jnp.mean(out * out, axis=-1, keepdims=True) + EPS) * gpost_ref[...]
    o_ref[...] = x_ref[...] + gate_ref[0] * y


def _out(ocmp, osel, owin, ob, slab, x2d, ada3, g_post, wa, wb, wo, seq):
    M, D = x2d.shape
    tm = 256
    per_b = seq // tm
    const = lambda a: pl.BlockSpec(a.shape, lambda i: (0,) * a.ndim, pipeline_mode=pl.Buffered(1))
    row = lambda w, cb: pl.BlockSpec((tm, w), lambda i: (i, cb))
    return pl.pallas_call(
        _out_kernel,
        grid=(M // tm,),
        in_specs=[row(A_WIDTH, 0), row(A_WIDTH, 0), row(A_WIDTH, 0),
                  row(A_WIDTH, COL_ZA // A_WIDTH), row(B_WIDTH, 0), row(B_WIDTH, COL_ZB // B_WIDTH),
                  row(D_MODEL, COL_MA // D_MODEL), row(D_MODEL, COL_MB // D_MODEL),
                  row(D, 0),
                  pl.BlockSpec((1, 1, D), lambda i: (i // per_b, 0, 2)),
                  pl.BlockSpec((1, D), lambda i: (0, 0)),
                  const(wa), const(wb), const(wo)],
        out_specs=pl.BlockSpec((tm, D), lambda i: (i, 0)),
        out_shape=jax.ShapeDtypeStruct((M, D), F32),
        compiler_params=_cparams(("arbitrary",)),
        name="out",
    )(ocmp, osel, owin, slab, ob, slab, slab, slab, x2d, ada3, g_post.reshape(1, D), wa, wb, wo)


def _layer(x, c, positions, w_ada, b_ada, g_pre, g_post, w_in, pe_ck, pe_cv, w_ck1, w_ck2,
           w_cv1, w_cv2, w_br_a, w_br_b, w_out):
    B, S, D = x.shape
    x2d = x.reshape(B * S, D)
    ada3 = _ada(c, w_ada, b_ada).reshape(B, 1, 3 * D)
    slab, kcv = _proj(x2d, ada3, g_pre, positions.reshape(B * S, 1), _build_w_slab_t(jnp.swapaxes(w_in, 0, 1)), S)

    nblk = S // CMP_STRIDE
    assert (S - CMP_BLOCK) // CMP_STRIDE + 1 == nblk - 1
    cmp_end = np.minimum(np.arange(nblk) * CMP_STRIDE + CMP_BLOCK - 1, S - 1)
    posc = positions[:, cmp_end].reshape(B, nblk, 1)
    dup = lambda w: jnp.concatenate([w, w], axis=1).astype(BF16)
    pe2 = jnp.concatenate([pe_ck, pe_cv], axis=1)
    kc2, vc2 = _cmp_kv(kcv, pe2, posc, _build_cmp_w1(w_ck1, w_cv1), dup(w_ck2), dup(w_cv2), B, S)

    ocmp, osel, owin = _nsa_attend(slab, kc2, vc2, B, S)
    ob = _dil(slab, B, S)
    out = _out(ocmp, osel, owin, ob, slab, x2d, ada3, g_post,
               w_br_a.astype(BF16), w_br_b.astype(BF16), w_out.astype(BF16), S)
    return out.reshape(B, S, D)


def kernel(x, c, positions, w_ada, b_ada, g_pre, g_post, w_in, pe_ck, pe_cv, w_ck1, w_ck2, w_cv1, w_cv2, w_br_a, w_br_b, w_out):
    h = x
    for layer in range(w_ada.shape[0]):
        h = _layer(h, c, positions, w_ada[layer], b_ada[layer], g_pre[layer], g_post[layer],
                   w_in[layer], pe_ck[layer], pe_cv[layer], w_ck1[layer], w_ck2[layer],
                   w_cv1[layer], w_cv2[layer], w_br_a[layer], w_br_b[layer], w_out[layer])
    return h
```
